```python
import math
import jax
import jax.numpy as jnp
from jax import lax
import numpy as np

D_MODEL = 2048
BATCH = 8
SEQ = 2048
DEPTH = 1
DEC_BATCH = 8
DEC_SEQ = 64
PAST_LEN = 1024

CHUNK = 64
Q_BLOCK = 128
D_FF = 5632
SSM_WIDTH = D_MODEL // 2
SSM_GROUP = 16
SSM_GROUPS = SSM_WIDTH // SSM_GROUP
SSM_STATE = 64
MLA_HEADS = 16
QK_NOPE = 128
QK_ROPE = 64
V_DIM = 128
Q_LORA = 768
KV_LORA = 512
ROPE_THETA = 10000.0
MEM_TOKENS = 256
MEM_HEADS = 4
MEM_HEAD_DIM = 128
MEM_WIDTH = MEM_HEADS * MEM_HEAD_DIM
N_BRANCHES = 3
IN_SPLITS = (SSM_WIDTH,
             SSM_WIDTH + Q_LORA,
             SSM_WIDTH + Q_LORA + KV_LORA,
             SSM_WIDTH + Q_LORA + KV_LORA + QK_ROPE,
             SSM_WIDTH + Q_LORA + KV_LORA + QK_ROPE + MEM_WIDTH)
IN_COLS = IN_SPLITS[-1] + N_BRANCHES * D_MODEL
RMS_EPS = 1e-6
NEG_INF = -1e30
MLA_SCALE = (QK_NOPE + QK_ROPE) ** -0.5
MEM_SCALE = MEM_HEAD_DIM ** -0.5

kernel_name = "chunk_causal_s5_mla_memory_hybrid_step"


def _rmsnorm(x, g):
    x32 = x.astype(jnp.float32)
    y = x32 * lax.rsqrt(jnp.mean(x32 * x32, axis=-1, keepdims=True) + RMS_EPS)
    return (y * g.astype(jnp.float32)).astype(x.dtype)


def _swiglu(h, w_gate, w_up, w_down):
    return (jax.nn.silu(h @ w_gate) * (h @ w_up)) @ w_down


def _rope(t, pos):
    half = QK_ROPE // 2
    inv_freq = ROPE_THETA ** (-jnp.arange(half, dtype=jnp.float32) / half)
    ang = pos.astype(jnp.float32)[:, None] * inv_freq[None, :]
    cos = jnp.cos(ang)[:, None, :]
    sin = jnp.sin(ang)[:, None, :]
    t32 = t.astype(jnp.float32)
    t1, t2 = t32[..., :half], t32[..., half:]
    return jnp.concatenate([t1 * cos - t2 * sin, t2 * cos + t1 * sin], axis=-1).astype(t.dtype)


def _mla_attend(q_nope, q_rope, q_pos, k_nope, k_rope, v, k_pos):
    s = (jnp.einsum("bqhd,bkhd->bhqk", q_nope, k_nope)
         + jnp.einsum("bqhr,bkr->bhqk", q_rope, k_rope)).astype(jnp.float32) * MLA_SCALE
    visible = (k_pos[None, :] // CHUNK) <= (q_pos[:, None] // CHUNK)
    s = jnp.where(visible[None, None], s, NEG_INF)
    p = jax.nn.softmax(s, axis=-1).astype(v.dtype)
    return jnp.einsum("bhqk,bkhd->bqhd", p, v)


def _mla_attend_blocked(q_nope, q_rope, q_pos, k_nope, k_rope, v, k_pos):
    bsz, seqlen = q_nope.shape[:2]
    n_blk = seqlen // Q_BLOCK

    def blocks(t):
        return jnp.moveaxis(t.reshape((bsz, n_blk, Q_BLOCK) + t.shape[2:]), 1, 0)

    def one_block(args):
        qn, qr, qp = args
        return _mla_attend(qn, qr, qp, k_nope, k_rope, v, k_pos)

    out = lax.map(one_block, (blocks(q_nope), blocks(q_rope), q_pos.reshape(n_blk, Q_BLOCK)))
    return jnp.moveaxis(out, 0, 1).reshape((bsz, seqlen) + out.shape[3:])


def _s5(u, h0_re, h0_im, a_re, a_im, log_dt, b_re, b_im, c_re, c_im, d):
    f32 = jnp.float32
    u32 = u.astype(f32)
    a_re, a_im = a_re.astype(f32), a_im.astype(f32)
    dt = jnp.exp(log_dt.astype(f32))[:, None]
    mag = jnp.exp(a_re * dt)
    phase = a_im * dt
    lb_re, lb_im = mag * jnp.cos(phase), mag * jnp.sin(phase)
    den = a_re * a_re + a_im * a_im
    nr, ni = lb_re - 1.0, lb_im
    z_re = (nr * a_re + ni * a_im) / den
    z_im = (ni * a_re - nr * a_im) / den
    b_re, b_im = b_re.astype(f32), b_im.astype(f32)
    bb_re = z_re[..., None] * b_re - z_im[..., None] * b_im
    bb_im = z_re[..., None] * b_im + z_im[..., None] * b_re
    bu_re = jnp.einsum("blgh,gph->blgp", u32, bb_re)
    bu_im = jnp.einsum("blgh,gph->blgp", u32, bb_im)
    if h0_re is not None:
        h0_re, h0_im = h0_re.astype(f32), h0_im.astype(f32)
        bu_re = bu_re.at[:, 0].add(lb_re * h0_re - lb_im * h0_im)
        bu_im = bu_im.at[:, 0].add(lb_re * h0_im + lb_im * h0_re)
    seqlen = u.shape[1]
    A_re = jnp.broadcast_to(lb_re, (1, seqlen) + lb_re.shape)
    A_im = jnp.broadcast_to(lb_im, (1, seqlen) + lb_im.shape)

    def combine(e1, e2):
        ar1, ai1, br1, bi1 = e1
        ar2, ai2, br2, bi2 = e2
        return (ar1 * ar2 - ai1 * ai2,
                ar1 * ai2 + ai1 * ar2,
                ar2 * br1 - ai2 * bi1 + br2,
                ar2 * bi1 + ai2 * br1 + bi2)

    _, _, x_re, x_im = lax.associative_scan(combine, (A_re, A_im, bu_re, bu_im), axis=1)
    y = (jnp.einsum("blgp,ghp->blgh", x_re, c_re.astype(f32))
         - jnp.einsum("blgp,ghp->blgh", x_im, c_im.astype(f32))
         + d.astype(f32) * u32)
    return y, x_re[:, -1], x_im[:, -1]


def _mem_kv(mem, g, w_k, w_v):
    bsz, n_mem, _ = mem.shape
    mn = _rmsnorm(mem, g)
    k = (mn @ w_k).reshape(bsz, n_mem, MEM_HEADS, MEM_HEAD_DIM)
    v = (mn @ w_v).reshape(bsz, n_mem, MEM_HEADS, MEM_HEAD_DIM)
    return k, v


def _mem_attend(q, k, v):
    s = jnp.einsum("bqhd,bkhd->bhqk", q, k.astype(q.dtype)).astype(jnp.float32) * MEM_SCALE
    p = jax.nn.softmax(s, axis=-1).astype(q.dtype)
    return jnp.einsum("bhqk,bkhd->bqhd", p, v.astype(q.dtype))


def _layer(x, pos, mem_k, mem_v, lat_past, kr_past, h0_re, h0_im, p, blocked):
    bsz, seqlen, _ = x.shape
    x = x + 0.5 * _swiglu(_rmsnorm(x, p["ffn1_norm"]), p["ffn1_w_gate"], p["ffn1_w_up"], p["ffn1_w_down"])
    h = _rmsnorm(x, p["mix_norm"])
    z = h @ p["w_in"]
    u_ssm, c_q, c_kv, k_r, q_mem, gate_pre = jnp.split(z, IN_SPLITS, axis=-1)

    y_ssm, h_re, h_im = _s5(u_ssm.reshape(bsz, seqlen, SSM_GROUPS, SSM_GROUP), h0_re, h0_im,
                            p["ssm_a_re"], p["ssm_a_im"], p["ssm_log_dt"], p["ssm_b_re"], p["ssm_b_im"],
                            p["ssm_c_re"], p["ssm_c_im"], p["ssm_d"])
    y_ssm = jax.nn.gelu(y_ssm.reshape(bsz, seqlen, SSM_WIDTH).astype(x.dtype))
    y_ssm = y_ssm * jax.nn.sigmoid(y_ssm @ p["ssm_w_glu"] + p["ssm_b_glu"])

    q = (_rmsnorm(c_q, p["q_norm"]) @ p["w_uq"]).reshape(bsz, seqlen, MLA_HEADS, QK_NOPE + QK_ROPE)
    q_nope = q[..., :QK_NOPE]
    q_rope = _rope(q[..., QK_NOPE:], pos)
    lat_new = _rmsnorm(c_kv, p["kv_norm"])
    kr_new = _rope(k_r[:, :, None, :], pos)[:, :, 0, :]
    if lat_past is None:
        lat, kr = lat_new, kr_new
    else:
        lat = jnp.concatenate([lat_past.astype(lat_new.dtype), lat_new], axis=1)
        kr = jnp.concatenate([kr_past.astype(kr_new.dtype), kr_new], axis=1)
    n_keys = lat.shape[1]
    k_pos = jnp.arange(n_keys)
    k_nope = (lat @ p["w_uk"]).reshape(bsz, n_keys, MLA_HEADS, QK_NOPE)
    v = (lat @ p["w_uv"]).reshape(bsz, n_keys, MLA_HEADS, V_DIM)
    attend = _mla_attend_blocked if blocked else _mla_attend
    y_mla = attend(q_nope, q_rope, pos, k_nope, kr, v, k_pos).reshape(bsz, seqlen, MLA_HEADS * V_DIM)

    y_mem = _mem_attend(q_mem.reshape(bsz, seqlen, MEM_HEADS, MEM_HEAD_DIM), mem_k, mem_v)
    y_mem = y_mem.reshape(bsz, seqlen, MEM_WIDTH)

    g_ssm, g_mla, g_mem = jnp.split(jax.nn.sigmoid(gate_pre), N_BRANCHES, axis=-1)
    merged = (g_ssm * (y_ssm @ p["w_br_ssm"])
              + g_mla * (y_mla @ p["w_br_mla"])
              + g_mem * (y_mem @ p["w_br_mem"]))
    x = x + merged @ p["w_out"]
    x = x + 0.5 * _swiglu(_rmsnorm(x, p["ffn2_norm"]), p["ffn2_w_gate"], p["ffn2_w_up"], p["ffn2_w_down"])
    return x, lat_new, kr_new, h_re, h_im


def setup_inputs(seed: int = 0) -> dict:
    key = jax.random.key(seed)
    keys = jax.random.split(key, 64)
    ks = (keys[i] for i in range(64))
    f32 = jnp.float32
    L = DEPTH

    def nrm(shape, scale):
        return jax.random.normal(next(ks), shape, f32) * scale

    def gain(shape):
        return 1.0 + 0.01 * jax.random.normal(next(ks), shape, f32)

    G, P, Hc = SSM_GROUPS, SSM_STATE, SSM_GROUP
    return {
        "x_prompt": nrm((BATCH, SEQ, D_MODEL), 1.0),
        "x_sample": nrm((DEC_BATCH, DEC_SEQ, D_MODEL), 1.0),
        "cache_kv_latent": nrm((L, DEC_BATCH, PAST_LEN, KV_LORA), 1.0),
        "cache_k_rope": nrm((L, DEC_BATCH, PAST_LEN, QK_ROPE), 1.0),
        "cache_mem_k": nrm((L, DEC_BATCH, MEM_TOKENS, MEM_HEADS, MEM_HEAD_DIM), 1.0),
        "cache_mem_v": nrm((L, DEC_BATCH, MEM_TOKENS, MEM_HEADS, MEM_HEAD_DIM), 1.0),
        "state_ssm_re": nrm((L, DEC_BATCH, G, P), 0.1),
        "state_ssm_im": nrm((L, DEC_BATCH, G, P), 0.1),
        "mem_prompt": nrm((BATCH, MEM_TOKENS, D_MODEL), 1.0),
        "ffn1_norm": gain((L, D_MODEL)),
        "ffn1_w_gate": nrm((L, D_MODEL, D_FF), D_MODEL ** -0.5),
        "ffn1_w_up": nrm((L, D_MODEL, D_FF), D_MODEL ** -0.5),
        "ffn1_w_down": nrm((L, D_FF, D_MODEL), D_FF ** -0.5),
        "mix_norm": gain((L, D_MODEL)),
        "w_in": nrm((L, D_MODEL, IN_COLS), D_MODEL ** -0.5),
        "q_norm": gain((L, Q_LORA)),
        "w_uq": nrm((L, Q_LORA, MLA_HEADS * (QK_NOPE + QK_ROPE)), Q_LORA ** -0.5),
        "kv_norm": gain((L, KV_LORA)),
        "w_uk": nrm((L, KV_LORA, MLA_HEADS * QK_NOPE), KV_LORA ** -0.5),
        "w_uv": nrm((L, KV_LORA, MLA_HEADS * V_DIM), KV_LORA ** -0.5),
        "ssm_a_re": -0.5 * (1.0 + 0.05 * nrm((L, G, P), 1.0)),
        "ssm_a_im": jnp.tile(math.pi * jnp.arange(P, dtype=f32), (L, G, 1)),
        "ssm_log_dt": jax.random.uniform(next(ks), (L, G), f32, math.log(1e-3), math.log(1e-1)),
        "ssm_b_re": nrm((L, G, P, Hc), (2.0 * Hc) ** -0.5),
        "ssm_b_im": nrm((L, G, P, Hc), (2.0 * Hc) ** -0.5),
        "ssm_c_re": nrm((L, G, Hc, P), 0.5),
        "ssm_c_im": nrm((L, G, Hc, P), 0.5),
        "ssm_d": nrm((L, G, Hc), 1.0),
        "ssm_w_glu": nrm((L, SSM_WIDTH, SSM_WIDTH), SSM_WIDTH ** -0.5),
        "ssm_b_glu": nrm((L, SSM_WIDTH), 0.01),
        "mem_norm": gain((L, D_MODEL)),
        "w_mem_k": nrm((L, D_MODEL, MEM_WIDTH), D_MODEL ** -0.5),
        "w_mem_v": nrm((L, D_MODEL, MEM_WIDTH), D_MODEL ** -0.5),
        "w_br_ssm": nrm((L, SSM_WIDTH, D_MODEL), SSM_WIDTH ** -0.5),
        "w_br_mla": nrm((L, MLA_HEADS * V_DIM, D_MODEL), (MLA_HEADS * V_DIM) ** -0.5),
        "w_br_mem": nrm((L, MEM_WIDTH, D_MODEL), MEM_WIDTH ** -0.5),
        "w_out": nrm((L, D_MODEL, D_MODEL), D_MODEL ** -0.5),
        "ffn2_norm": gain((L, D_MODEL)),
        "ffn2_w_gate": nrm((L, D_MODEL, D_FF), D_MODEL ** -0.5),
        "ffn2_w_up": nrm((L, D_MODEL, D_FF), D_MODEL ** -0.5),
        "ffn2_w_down": nrm((L, D_FF, D_MODEL), D_FF ** -0.5),
        "final_norm": gain((D_MODEL,)),
    }


def reference(x_prompt, x_sample, cache_kv_latent, cache_k_rope, cache_mem_k, cache_mem_v,
              state_ssm_re, state_ssm_im, mem_prompt,
              ffn1_norm, ffn1_w_gate, ffn1_w_up, ffn1_w_down, mix_norm, w_in,
              q_norm, w_uq, kv_norm, w_uk, w_uv,
              ssm_a_re, ssm_a_im, ssm_log_dt, ssm_b_re, ssm_b_im, ssm_c_re, ssm_c_im, ssm_d,
              ssm_w_glu, ssm_b_glu, mem_norm, w_mem_k, w_mem_v,
              w_br_ssm, w_br_mla, w_br_mem, w_out,
              ffn2_norm, ffn2_w_gate, ffn2_w_up, ffn2_w_down, final_norm):
    xp, xs = x_prompt, x_sample
    past = cache_kv_latent.shape[2]
    pos_p = jnp.arange(xp.shape[1])
    pos_s = past + jnp.arange(xs.shape[1])
    lat_p_l, kr_p_l, mk_p_l, mv_p_l, sre_p_l, sim_p_l = [], [], [], [], [], []
    lat_s_l, kr_s_l, sre_s_l, sim_s_l = [], [], [], []
    for l in range(DEPTH):
        p = {
            "ffn1_norm": ffn1_norm[l], "ffn1_w_gate": ffn1_w_gate[l], "ffn1_w_up": ffn1_w_up[l],
            "ffn1_w_down": ffn1_w_down[l], "mix_norm": mix_norm[l], "w_in": w_in[l],
            "q_norm": q_norm[l], "w_uq": w_uq[l], "kv_norm": kv_norm[l], "w_uk": w_uk[l], "w_uv": w_uv[l],
            "ssm_a_re": ssm_a_re[l], "ssm_a_im": ssm_a_im[l], "ssm_log_dt": ssm_log_dt[l],
            "ssm_b_re": ssm_b_re[l], "ssm_b_im": ssm_b_im[l], "ssm_c_re": ssm_c_re[l],
            "ssm_c_im": ssm_c_im[l], "ssm_d": ssm_d[l], "ssm_w_glu": ssm_w_glu[l], "ssm_b_glu": ssm_b_glu[l],
            "w_br_ssm": w_br_ssm[l], "w_br_mla": w_br_mla[l], "w_br_mem": w_br_mem[l], "w_out": w_out[l],
            "ffn2_norm": ffn2_norm[l], "ffn2_w_gate": ffn2_w_gate[l], "ffn2_w_up": ffn2_w_up[l],
            "ffn2_w_down": ffn2_w_down[l],
        }
        mk_p, mv_p = _mem_kv(mem_prompt, mem_norm[l], w_mem_k[l], w_mem_v[l])
        xp, lat_p, kr_p, sre_p, sim_p = _layer(xp, pos_p, mk_p, mv_p, None, None, None, None, p, True)
        xs, lat_s, kr_s, sre_s, sim_s = _layer(xs, pos_s, cache_mem_k[l], cache_mem_v[l],
                                               cache_kv_latent[l], cache_k_rope[l],
                                               state_ssm_re[l], state_ssm_im[l], p, False)
        lat_p_l.append(lat_p); kr_p_l.append(kr_p); mk_p_l.append(mk_p); mv_p_l.append(mv_p)
        sre_p_l.append(sre_p); sim_p_l.append(sim_p)
        lat_s_l.append(lat_s); kr_s_l.append(kr_s); sre_s_l.append(sre_s); sim_s_l.append(sim_s)
    y_prompt = _rmsnorm(xp, final_norm)
    y_sample = _rmsnorm(xs, final_norm)
    return (y_prompt, y_sample,
            jnp.stack(lat_p_l), jnp.stack(kr_p_l), jnp.stack(mk_p_l), jnp.stack(mv_p_l),
            jnp.stack(sre_p_l), jnp.stack(sim_p_l),
            jnp.stack(lat_s_l), jnp.stack(kr_s_l), jnp.stack(sre_s_l), jnp.stack(sim_s_l))
```

```python
import functools
import math

import jax
import jax.numpy as jnp
from jax import lax
from jax.experimental import pallas as pl
from jax.experimental.pallas import tpu as pltpu

F32 = jnp.float32
BF16 = jnp.bfloat16

D_MODEL = 2048
D_FF = 5632
CHUNK = 64
SSM_WIDTH = D_MODEL // 2
SSM_GROUP = 16
SSM_GROUPS = SSM_WIDTH // SSM_GROUP
SSM_STATE = 64
MLA_HEADS = 16
QK_NOPE = 128
QK_ROPE = 64
V_DIM = 128
Q_LORA = 768
KV_LORA = 512
ROPE_THETA = 10000.0
MEM_TOKENS = 256
MEM_HEADS = 4
MEM_HEAD_DIM = 128
MEM_WIDTH = MEM_HEADS * MEM_HEAD_DIM
RMS_EPS = 1e-6
NEG_INF = -1e30
MLA_SCALE = (QK_NOPE + QK_ROPE) ** -0.5
MEM_SCALE = MEM_HEAD_DIM ** -0.5

LANES = 128
HEAD_PAD = 2 * LANES
SSM_CH = 128
SSM_CG = SSM_CH // SSM_GROUP
SSM_CS = SSM_CG * SSM_STATE
MIB = 1024 * 1024


def _cp(sem, vmem_mib):
    return pltpu.CompilerParams(dimension_semantics=sem, vmem_limit_bytes=int(vmem_mib * MIB))


def _dot(a, b):
    return jnp.dot(a, b, preferred_element_type=F32)


def _rms(x, g):
    return x * lax.rsqrt(jnp.mean(x * x, axis=-1, keepdims=True) + RMS_EPS) * g


def _sigmoid(x):
    return 1.0 / (1.0 + jnp.exp(-x))


def _gelu_tanh(x):
    cdf = 0.5 * (1.0 + jnp.tanh(math.sqrt(2.0 / math.pi) * (x + 0.044715 * (x * x * x))))
    return x * cdf


def _ffn_body(x_ref, g_ref, wg_ref, wu_ref, wd_ref, *rest, final):
    if final:
        fg_ref, o_ref, h_scr, acc_scr = rest
    else:
        o_ref, h_scr, acc_scr = rest
    j = pl.program_id(1)

    @pl.when(j == 0)
    def _():
        h_scr[...] = _rms(x_ref[...], g_ref[...]).astype(BF16)
        acc_scr[...] = jnp.zeros_like(acc_scr)

    h = h_scr[...]
    a = _dot(h, wg_ref[...])
    b = _dot(h, wu_ref[...])
    act = ((a * _sigmoid(a)) * b).astype(BF16)
    acc_scr[...] += _dot(act, wd_ref[...])

    @pl.when(j == pl.num_programs(1) - 1)
    def _():
        y = x_ref[...] + 0.5 * acc_scr[...]
        if final:
            y = _rms(y, fg_ref[...])
        o_ref[...] = y


def _ffn(x, g, wg, wu, wd, final_g=None, *, tm=512, tf=512):
    rows = x.shape[0]
    final = final_g is not None
    in_specs = [
        pl.BlockSpec((tm, D_MODEL), lambda i, j: (i, 0)),
        pl.BlockSpec((1, D_MODEL), lambda i, j: (0, 0)),
        pl.BlockSpec((D_MODEL, tf), lambda i, j: (0, j)),
        pl.BlockSpec((D_MODEL, tf), lambda i, j: (0, j)),
        pl.BlockSpec((tf, D_MODEL), lambda i, j: (j, 0)),
    ]
    args = [x, g, wg, wu, wd]
    if final:
        in_specs.append(pl.BlockSpec((1, D_MODEL), lambda i, j: (0, 0)))
        args.append(final_g)
    return pl.pallas_call(
        functools.partial(_ffn_body, final=final),
        grid=(rows // tm, D_FF // tf),
        in_specs=in_specs,
        out_specs=pl.BlockSpec((tm, D_MODEL), lambda i, j: (i, 0)),
        out_shape=jax.ShapeDtypeStruct((rows, D_MODEL), F32),
        scratch_shapes=[pltpu.VMEM((tm, D_MODEL), BF16), pltpu.VMEM((tm, D_MODEL), F32)],
        compiler_params=_cp(("parallel", "arbitrary"), 48),
        name="ffn",
    )(*args)


def _nmm_body(x_ref, g_ref, w_ref, o_ref, h_scr, *, act):
    @pl.when(pl.program_id(1) == 0)
    def _():
        h_scr[...] = _rms(x_ref[...], g_ref[...]).astype(BF16)

    z = _dot(h_scr[...], w_ref[...])
    if act == "sigmoid":
        z = _sigmoid(z)
    o_ref[...] = z.astype(o_ref.dtype)


def _nmm(x, g, w, *, act, out_dtype, tm, tn, name):
    rows, k = x.shape
    n = w.shape[1]
    return pl.pallas_call(
        functools.partial(_nmm_body, act=act),
        grid=(rows // tm, n // tn),
        in_specs=[
            pl.BlockSpec((tm, k), lambda i, j: (i, 0)),
            pl.BlockSpec((1, k), lambda i, j: (0, 0)),
            pl.BlockSpec((k, tn), lambda i, j: (0, j)),
        ],
        out_specs=pl.BlockSpec((tm, tn), lambda i, j: (i, j)),
        out_shape=jax.ShapeDtypeStruct((rows, n), out_dtype),
        scratch_shapes=[pltpu.VMEM((tm, k), BF16)],
        compiler_params=_cp(("parallel", "arbitrary"), 40),
        name=name,
    )(x, g, w)


_C_U = SSM_WIDTH
_C_Q = _C_U + Q_LORA
_C_KV = _C_Q + KV_LORA
_C_KX = _C_KV + LANES
_C_KS = _C_KX + LANES
_C_QM = _C_KS + MEM_WIDTH


def _inproj_body(x_ref, g_ref, w_ref, qg_ref, kvg_ref, cos_ref, sin_ref,
                 u_ref, cq_ref, lat_ref, kr_ref, krp_ref, qm_ref):
    h = _rms(x_ref[...], g_ref[...]).astype(BF16)
    u_ref[...] = _dot(h, w_ref[:, 0:_C_U])
    cq_ref[...] = _rms(_dot(h, w_ref[:, _C_U:_C_Q]), qg_ref[...]).astype(BF16)
    lat_ref[...] = _rms(_dot(h, w_ref[:, _C_Q:_C_KV]), kvg_ref[...])
    kx = _dot(h, w_ref[:, _C_KV:_C_KX])
    ks = _dot(h, w_ref[:, _C_KX:_C_KS])
    r = kx * cos_ref[...] + ks * sin_ref[...]
    krp_ref[...] = r
    kr_ref[...] = r[:, :QK_ROPE]
    qm_ref[...] = _dot(h, w_ref[:, _C_KS:_C_QM]).astype(BF16)


def _inproj(x, g, w, qg, kvg, cos, sin, *, bsz, seqlen, tm):
    rows = bsz * seqlen
    nt = seqlen // tm
    full = lambda i: (0, 0)
    row = lambda i: (i, 0)
    return pl.pallas_call(
        _inproj_body,
        grid=(rows // tm,),
        in_specs=[
            pl.BlockSpec((tm, D_MODEL), row),
            pl.BlockSpec((1, D_MODEL), full),
            pl.BlockSpec((D_MODEL, _C_QM), full),
            pl.BlockSpec((1, Q_LORA), full),
            pl.BlockSpec((1, KV_LORA), full),
            pl.BlockSpec((tm, LANES), lambda i: (i % nt, 0)),
            pl.BlockSpec((tm, LANES), lambda i: (i % nt, 0)),
        ],
        out_specs=[
            pl.BlockSpec((tm, SSM_WIDTH), lambda i: (i % nt, i // nt)),
            pl.BlockSpec((tm, Q_LORA), row),
            pl.BlockSpec((tm, KV_LORA), row),
            pl.BlockSpec((tm, QK_ROPE), row),
            pl.BlockSpec((tm, LANES), row),
            pl.BlockSpec((tm, MEM_WIDTH), row),
        ],
        out_shape=[
            jax.ShapeDtypeStruct((seqlen, bsz * SSM_WIDTH), F32),
            jax.ShapeDtypeStruct((rows, Q_LORA), BF16),
            jax.ShapeDtypeStruct((rows, KV_LORA), F32),
            jax.ShapeDtypeStruct((rows, QK_ROPE), F32),
            jax.ShapeDtypeStruct((rows, LANES), F32),
            jax.ShapeDtypeStruct((rows, MEM_WIDTH), BF16),
        ],
        compiler_params=_cp(("parallel",), 56),
        name="inproj",
    )(x, g, w, qg, kvg, cos, sin)


def _qproj_body(cq_ref, w_ref, cos_ref, sin_ref, q_ref):
    cq = cq_ref[...]
    hw = MLA_HEADS * LANES
    nope = _dot(cq, w_ref[:, 0:hw])
    for h in range(MLA_HEADS):
        q_ref[:, h * HEAD_PAD:h * HEAD_PAD + LANES] = nope[:, h * LANES:(h + 1) * LANES].astype(BF16)
    rx = _dot(cq, w_ref[:, hw:2 * hw])
    rs = _dot(cq, w_ref[:, 2 * hw:3 * hw])
    c = cos_ref[...]
    s = sin_ref[...]
    for h in range(MLA_HEADS):
        sl = slice(h * LANES, (h + 1) * LANES)
        q_ref[:, h * HEAD_PAD + LANES:(h + 1) * HEAD_PAD] = (rx[:, sl] * c + rs[:, sl] * s).astype(BF16)


def _qproj(cq, w, cos, sin, *, seqlen, tm):
    rows = cq.shape[0]
    nt = seqlen // tm
    return pl.pallas_call(
        _qproj_body,
        grid=(rows // tm,),
        in_specs=[
            pl.BlockSpec((tm, Q_LORA), lambda i: (i, 0)),
            pl.BlockSpec((Q_LORA, 3 * MLA_HEADS * LANES), lambda i: (0, 0)),
            pl.BlockSpec((tm, LANES), lambda i: (i % nt, 0)),
            pl.BlockSpec((tm, LANES), lambda i: (i % nt, 0)),
        ],
        out_specs=pl.BlockSpec((tm, MLA_HEADS * HEAD_PAD), lambda i: (i, 0)),
        out_shape=jax.ShapeDtypeStruct((rows, MLA_HEADS * HEAD_PAD), BF16),
        compiler_params=_cp(("parallel",), 48),
        name="qproj",
    )(cq, w, cos, sin)


def _kvproj_t_body(lat_ref, krp_ref, wukt_ref, wuv_ref, kt_ref, v_ref):
    lat = lat_ref[...]
    v_ref[...] = _dot(lat.astype(BF16), wuv_ref[...]).astype(BF16)
    kt = _dot(wukt_ref[...], lat.T.astype(BF16))
    krt = krp_ref[...].T.astype(BF16)
    for h in range(MLA_HEADS):
        kt_ref[h * HEAD_PAD:h * HEAD_PAD + LANES, :] = kt[h * LANES:(h + 1) * LANES, :].astype(BF16)
        kt_ref[h * HEAD_PAD + LANES:(h + 1) * HEAD_PAD, :] = krt


def _kvproj_t(lat, krp, wukt, wuv, *, bsz, seqlen, tm):
    rows = bsz * seqlen
    nt = seqlen // tm
    return pl.pallas_call(
        _kvproj_t_body,
        grid=(rows // tm,),
        in_specs=[
            pl.BlockSpec((tm, KV_LORA), lambda i: (i, 0)),
            pl.BlockSpec((tm, LANES), lambda i: (i, 0)),
            pl.BlockSpec((MLA_HEADS * QK_NOPE, KV_LORA), lambda i: (0, 0)),
            pl.BlockSpec((KV_LORA, MLA_HEADS * V_DIM), lambda i: (0, 0)),
        ],
        out_specs=[
            pl.BlockSpec((None, MLA_HEADS * HEAD_PAD, tm), lambda i: (i // nt, 0, i % nt)),
            pl.BlockSpec((tm, MLA_HEADS * V_DIM), lambda i: (i, 0)),
        ],
        out_shape=[
            jax.ShapeDtypeStruct((bsz, MLA_HEADS * HEAD_PAD, seqlen), BF16),
            jax.ShapeDtypeStruct((rows, MLA_HEADS * V_DIM), BF16),
        ],
        compiler_params=_cp(("parallel",), 48),
        name="kvproj_t",
    )(lat, krp, wukt, wuv)


def _kvproj_body(lat_ref, krp_ref, wuk_ref, wuv_ref, k_ref, v_ref):
    lat = lat_ref[...].astype(BF16)
    v_ref[...] = _dot(lat, wuv_ref[...]).astype(BF16)
    kn = _dot(lat, wuk_ref[...])
    kr = krp_ref[...].astype(BF16)
    for h in range(MLA_HEADS):
        k_ref[:, h * HEAD_PAD:h * HEAD_PAD + LANES] = kn[:, h * LANES:(h + 1) * LANES].astype(BF16)
        k_ref[:, h * HEAD_PAD + LANES:(h + 1) * HEAD_PAD] = kr


def _kvproj(lat, krp, wuk, wuv, *, tm):
    rows = lat.shape[0]
    return pl.pallas_call(
        _kvproj_body,
        grid=(rows // tm,),
        in_specs=[
            pl.BlockSpec((tm, KV_LORA), lambda i: (i, 0)),
            pl.BlockSpec((tm, LANES), lambda i: (i, 0)),
            pl.BlockSpec((KV_LORA, MLA_HEADS * QK_NOPE), lambda i: (0, 0)),
            pl.BlockSpec((KV_LORA, MLA_HEADS * V_DIM), lambda i: (0, 0)),
        ],
        out_specs=[
            pl.BlockSpec((tm, MLA_HEADS * HEAD_PAD), lambda i: (i, 0)),
            pl.BlockSpec((tm, MLA_HEADS * V_DIM), lambda i: (i, 0)),
        ],
        out_shape=[
            jax.ShapeDtypeStruct((rows, MLA_HEADS * HEAD_PAD), BF16),
            jax.ShapeDtypeStruct((rows, MLA_HEADS * V_DIM), BF16),
        ],
        compiler_params=_cp(("parallel",), 48),
        name="kvproj",
    )(lat, krp, wuk, wuv)


def _chunk_mask(tq, tk, q0, k0):
    r = lax.broadcasted_iota(jnp.int32, (tq, tk), 0) + q0
    c = lax.broadcasted_iota(jnp.int32, (tq, tk), 1) + k0
    return (c // CHUNK) <= (r // CHUNK)


def _attn_body(q_ref, kt_ref, v_ref, o_ref, *, tq, hg, nq):
    qi = pl.program_id(2)
    for vq in range(nq):
        @pl.when(qi == vq)
        def _(vq=vq):
            n_past = vq * tq
            visible = _chunk_mask(tq, tq, 0, 0)
            for h in range(hg):
                q = q_ref[:, h * HEAD_PAD:(h + 1) * HEAD_PAD]
                hk = slice(h * HEAD_PAD, (h + 1) * HEAD_PAD)
                hv = slice(h * V_DIM, (h + 1) * V_DIM)
                s_d = _dot(q, kt_ref[hk, n_past:n_past + tq]) * MLA_SCALE
                s_d = jnp.where(visible, s_d, NEG_INF)
                m = jnp.max(s_d, axis=-1, keepdims=True)
                if n_past:
                    s_p = _dot(q, kt_ref[hk, 0:n_past]) * MLA_SCALE
                    m = jnp.maximum(m, jnp.max(s_p, axis=-1, keepdims=True))
                p_d = jnp.exp(s_d - m)
                l = jnp.sum(p_d, axis=-1, keepdims=True)
                acc = _dot(p_d.astype(BF16), v_ref[n_past:n_past + tq, hv])
                if n_past:
                    p_p = jnp.exp(s_p - m)
                    l = l + jnp.sum(p_p, axis=-1, keepdims=True)
                    acc = acc + _dot(p_p.astype(BF16), v_ref[0:n_past, hv])
                o_ref[:, hv] = (acc / l).astype(BF16)


def _attn(q, kt, v, *, bsz, seqlen, tq, hg):
    nq = seqlen // tq
    return pl.pallas_call(
        functools.partial(_attn_body, tq=tq, hg=hg, nq=nq),
        grid=(bsz, MLA_HEADS // hg, nq),
        in_specs=[
            pl.BlockSpec((tq, hg * HEAD_PAD), lambda b, g, i: (b * nq + i, g)),
            pl.BlockSpec((None, hg * HEAD_PAD, seqlen), lambda b, g, i: (b, g, 0)),
            pl.BlockSpec((seqlen, hg * V_DIM), lambda b, g, i: (b, g)),
        ],
        out_specs=pl.BlockSpec((tq, hg * V_DIM), lambda b, g, i: (b * nq + i, g)),
        out_shape=jax.ShapeDtypeStruct((bsz * seqlen, MLA_HEADS * V_DIM), BF16),
        compiler_params=_cp(("parallel", "parallel", "arbitrary"), 40),
        name="attn",
    )(q, kt, v)


def _attn_cached_body(q_ref, k_ref, v_ref, o_ref, *, n_q, n_k, past):
    visible = _chunk_mask(n_q, n_k, past, 0)
    for h in range(MLA_HEADS):
        q = q_ref[:, h * HEAD_PAD:(h + 1) * HEAD_PAD]
        k = k_ref[:, h * HEAD_PAD:(h + 1) * HEAD_PAD]
        s = lax.dot_general(q, k, (((1,), (1,)), ((), ())), preferred_element_type=F32) * MLA_SCALE
        s = jnp.where(visible, s, NEG_INF)
        m = jnp.max(s, axis=-1, keepdims=True)
        p = jnp.exp(s - m)
        l = jnp.sum(p, axis=-1, keepdims=True)
        acc = _dot(p.astype(BF16), v_ref[:, h * V_DIM:(h + 1) * V_DIM])
        o_ref[:, h * V_DIM:(h + 1) * V_DIM] = (acc / l).astype(BF16)


def _attn_cached(q, k, v, *, bsz, n_q, n_k, past):
    return pl.pallas_call(
        functools.partial(_attn_cached_body, n_q=n_q, n_k=n_k, past=past),
        grid=(bsz,),
        in_specs=[
            pl.BlockSpec((n_q, MLA_HEADS * HEAD_PAD), lambda b: (b, 0)),
            pl.BlockSpec((n_k, MLA_HEADS * HEAD_PAD), lambda b: (b, 0)),
            pl.BlockSpec((n_k, MLA_HEADS * V_DIM), lambda b: (b, 0)),
        ],
        out_specs=pl.BlockSpec((n_q, MLA_HEADS * V_DIM), lambda b: (b, 0)),
        out_shape=jax.ShapeDtypeStruct((bsz * n_q, MLA_HEADS * V_DIM), BF16),
        compiler_params=_cp(("parallel",), 48),
        name="attn_cached",
    )(q, k, v)


def _memattn_body(q_ref, k_ref, v_ref, o_ref):
    kt = k_ref[...].T.astype(BF16)
    vb = v_ref[...].astype(BF16)
    for h in range(MEM_HEADS):
        sl = slice(h * MEM_HEAD_DIM, (h + 1) * MEM_HEAD_DIM)
        s = _dot(q_ref[:, sl], kt[sl, :]) * MEM_SCALE
        m = jnp.max(s, axis=-1, keepdims=True)
        p = jnp.exp(s - m)
        l = jnp.sum(p, axis=-1, keepdims=True)
        o_ref[:, sl] = (_dot(p.astype(BF16), vb[:, sl]) / l).astype(BF16)


def _memattn(q, k, v, *, bsz, seqlen, tq):
    nt = seqlen // tq
    return pl.pallas_call(
        _memattn_body,
        grid=(bsz, nt),
        in_specs=[
            pl.BlockSpec((tq, MEM_WIDTH), lambda b, i: (b * nt + i, 0)),
            pl.BlockSpec((None, MEM_TOKENS, MEM_WIDTH), lambda b, i: (b, 0, 0)),
            pl.BlockSpec((None, MEM_TOKENS, MEM_WIDTH), lambda b, i: (b, 0, 0)),
        ],
        out_specs=pl.BlockSpec((tq, MEM_WIDTH), lambda b, i: (b * nt + i, 0)),
        out_shape=jax.ShapeDtypeStruct((bsz * seqlen, MEM_WIDTH), BF16),
        compiler_params=_cp(("parallel", "parallel"), 24),
        name="memattn",
    )(q, k, v)


def _s5_body(u_ref, bm_ref, cm_ref, d_ref, lre_ref, lim_ref, h0re_ref, h0im_ref,
             y_ref, sre_ref, sim_ref, x_scr, st_scr, *, tb_len, bsz):
    tb = pl.program_id(1)

    @pl.when(tb == 0)
    def _():
        st_scr[:, 0:SSM_CS] = h0re_ref[...]
        st_scr[:, SSM_CS:2 * SSM_CS] = h0im_ref[...]

    u = u_ref[...].reshape(tb_len * bsz, SSM_CH)
    x_scr[...] = _dot(u.astype(BF16), bm_ref[0])
    lre = jnp.broadcast_to(lre_ref[...], (bsz, SSM_CS))
    lim = jnp.broadcast_to(lim_ref[...], (bsz, SSM_CS))

    def step(t, carry):
        re, im = carry
        r0 = pl.multiple_of(t * bsz, bsz)
        nre = (lre * re - lim * im) + x_scr[pl.ds(r0, bsz), 0:SSM_CS]
        nim = (lre * im + lim * re) + x_scr[pl.ds(r0, bsz), SSM_CS:2 * SSM_CS]
        x_scr[pl.ds(r0, bsz), 0:SSM_CS] = nre
        x_scr[pl.ds(r0, bsz), SSM_CS:2 * SSM_CS] = nim
        return nre, nim

    re, im = lax.fori_loop(0, tb_len, step,
                           (st_scr[:, 0:SSM_CS], st_scr[:, SSM_CS:2 * SSM_CS]), unroll=8)
    st_scr[:, 0:SSM_CS] = re
    st_scr[:, SSM_CS:2 * SSM_CS] = im

    y = _dot(x_scr[...].astype(BF16), cm_ref[0]) + d_ref[...] * u
    y_ref[...] = _gelu_tanh(y).reshape(tb_len, bsz, SSM_CH)

    @pl.when(tb == pl.num_programs(1) - 1)
    def _():
        sre_ref[...] = re
        sim_ref[...] = im


def _s5(u_tb, bm, cm, d, lre, lim, h0re, h0im, *, bsz, seqlen, tb_len):
    nc = SSM_WIDTH // SSM_CH
    u3 = u_tb.reshape(seqlen, bsz, SSM_WIDTH)
    y, sre, sim = pl.pallas_call(
        functools.partial(_s5_body, tb_len=tb_len, bsz=bsz),
        grid=(nc, seqlen // tb_len),
        in_specs=[
            pl.BlockSpec((tb_len, bsz, SSM_CH), lambda c, t: (t, 0, c)),
            pl.BlockSpec((1, SSM_CH, 2 * SSM_CS), lambda c, t: (c, 0, 0)),
            pl.BlockSpec((1, 2 * SSM_CS, SSM_CH), lambda c, t: (c, 0, 0)),
            pl.BlockSpec((1, SSM_CH), lambda c, t: (0, c)),
            pl.BlockSpec((1, SSM_CS), lambda c, t: (0, c)),
            pl.BlockSpec((1, SSM_CS), lambda c, t: (0, c)),
            pl.BlockSpec((bsz, SSM_CS), lambda c, t: (0, c)),
            pl.BlockSpec((bsz, SSM_CS), lambda c, t: (0, c)),
        ],
        out_specs=[
            pl.BlockSpec((tb_len, bsz, SSM_CH), lambda c, t: (t, 0, c)),
            pl.BlockSpec((bsz, SSM_CS), lambda c, t: (0, c)),
            pl.BlockSpec((bsz, SSM_CS), lambda c, t: (0, c)),
        ],
        out_shape=[
            jax.ShapeDtypeStruct((seqlen, bsz, SSM_WIDTH), F32),
            jax.ShapeDtypeStruct((bsz, SSM_GROUPS * SSM_STATE), F32),
            jax.ShapeDtypeStruct((bsz, SSM_GROUPS * SSM_STATE), F32),
        ],
        scratch_shapes=[
            pltpu.VMEM((tb_len * bsz, 2 * SSM_CS), F32),
            pltpu.VMEM((bsz, 2 * SSM_CS), F32),
        ],
        compiler_params=_cp(("parallel", "arbitrary"), 40),
        name="s5",
    )(u3, bm, cm, d, lre, lim, h0re, h0im)
    return y.reshape(seqlen, bsz * SSM_WIDTH), sre, sim


def _merge_body(ya_ref, wglu_ref, bglu_ref, ymla_ref, ymem_ref, gs_ref, gm_ref, ge_ref,
                wbs_ref, wbm_ref, wbe_ref, o_ref, yg_scr):
    @pl.when(pl.program_id(1) == 0)
    def _():
        y = ya_ref[...]
        gate = _sigmoid(_dot(y.astype(BF16), wglu_ref[...]) + bglu_ref[...])
        yg_scr[...] = (y * gate).astype(BF16)

    m = (gs_ref[...].astype(F32) * _dot(yg_scr[...], wbs_ref[...])
         + gm_ref[...].astype(F32) * _dot(ymla_ref[...], wbm_ref[...])
         + ge_ref[...].astype(F32) * _dot(ymem_ref[...], wbe_ref[...]))
    o_ref[...] = m.astype(BF16)


def _merge(ya_tb, wglu, bglu, ymla, ymem, gates, wbs, wbm, wbe, *, bsz, seqlen, tm, tn):
    rows = bsz * seqlen
    nt = seqlen // tm
    nj = D_MODEL // tn
    return pl.pallas_call(
        _merge_body,
        grid=(rows // tm, nj),
        in_specs=[
            pl.BlockSpec((tm, SSM_WIDTH), lambda i, j: (i % nt, i // nt)),
            pl.BlockSpec((SSM_WIDTH, SSM_WIDTH), lambda i, j: (0, 0)),
            pl.BlockSpec((1, SSM_WIDTH), lambda i, j: (0, 0)),
            pl.BlockSpec((tm, MLA_HEADS * V_DIM), lambda i, j: (i, 0)),
            pl.BlockSpec((tm, MEM_WIDTH), lambda i, j: (i, 0)),
            pl.BlockSpec((tm, tn), lambda i, j: (i, j)),
            pl.BlockSpec((tm, tn), lambda i, j: (i, nj + j)),
            pl.BlockSpec((tm, tn), lambda i, j: (i, 2 * nj + j)),
            pl.BlockSpec((SSM_WIDTH, tn), lambda i, j: (0, j)),
            pl.BlockSpec((MLA_HEADS * V_DIM, tn), lambda i, j: (0, j)),
            pl.BlockSpec((MEM_WIDTH, tn), lambda i, j: (0, j)),
        ],
        out_specs=pl.BlockSpec((tm, tn), lambda i, j: (i, j)),
        out_shape=jax.ShapeDtypeStruct((rows, D_MODEL), BF16),
        scratch_shapes=[pltpu.VMEM((tm, SSM_WIDTH), BF16)],
        compiler_params=_cp(("parallel", "arbitrary"), 40),
        name="merge",
    )(ya_tb, wglu, bglu, ymla, ymem, gates, gates, gates, wbs, wbm, wbe)


def _resmm_body(x_ref, m_ref, w_ref, o_ref):
    o_ref[...] = x_ref[...] + _dot(m_ref[...], w_ref[...])


def _resmm(x, m, w, *, tm, tn):
    rows, n = x.shape
    k = m.shape[1]
    return pl.pallas_call(
        _resmm_body,
        grid=(rows // tm, n // tn),
        in_specs=[
            pl.BlockSpec((tm, tn), lambda i, j: (i, j)),
            pl.BlockSpec((tm, k), lambda i, j: (i, 0)),
            pl.BlockSpec((k, tn), lambda i, j: (0, j)),
        ],
        out_specs=pl.BlockSpec((tm, tn), lambda i, j: (i, j)),
        out_shape=jax.ShapeDtypeStruct((rows, n), F32),
        compiler_params=_cp(("parallel", "arbitrary"), 40),
        name="resmm",
    )(x, m, w)


def _rope_tables(pos):
    half = QK_ROPE // 2
    inv_freq = ROPE_THETA ** (-jnp.arange(half, dtype=F32) / half)
    ang = pos.astype(F32)[:, None] * inv_freq[None, :]
    cos, sin = jnp.cos(ang), jnp.sin(ang)
    zero = jnp.zeros((pos.shape[0], LANES - QK_ROPE), F32)
    return (jnp.concatenate([cos, cos, zero], axis=1),
            jnp.concatenate([-sin, sin, zero], axis=1))


def _s5_params(a_re, a_im, log_dt, b_re, b_im, c_re, c_im, d):
    dt = jnp.exp(log_dt)[:, None]
    mag = jnp.exp(a_re * dt)
    phase = a_im * dt
    lb_re, lb_im = mag * jnp.cos(phase), mag * jnp.sin(phase)
    den = a_re * a_re + a_im * a_im
    nr, ni = lb_re - 1.0, lb_im
    z_re = (nr * a_re + ni * a_im) / den
    z_im = (ni * a_re - nr * a_im) / den
    bb_re = z_re[..., None] * b_re - z_im[..., None] * b_im
    bb_im = z_re[..., None] * b_im + z_im[..., None] * b_re
    nc = SSM_GROUPS // SSM_CG
    eye = jnp.eye(SSM_CG, dtype=F32)

    def blk_b(t):
        t = t.reshape(nc, SSM_CG, SSM_STATE, SSM_GROUP)
        return jnp.einsum("cgph,gk->cghkp", t, eye).reshape(nc, SSM_CH, SSM_CS)

    def blk_c(t):
        t = t.reshape(nc, SSM_CG, SSM_GROUP, SSM_STATE)
        return jnp.einsum("cghp,gk->cgpkh", t, eye).reshape(nc, SSM_CS, SSM_CH)

    bm = jnp.concatenate([blk_b(bb_re), blk_b(bb_im)], axis=2).astype(BF16)
    cm = jnp.concatenate([blk_c(c_re), -blk_c(c_im)], axis=1).astype(BF16)
    return (bm, cm, d.reshape(1, SSM_WIDTH),
            lb_re.reshape(1, SSM_GROUPS * SSM_STATE), lb_im.reshape(1, SSM_GROUPS * SSM_STATE))


def _layer(x, bsz, seqlen, pos0, mem_k, mem_v, lat_past, krp_past, h0re, h0im, p, final_g):
    rows = bsz * seqlen
    tm = min(512, seqlen)
    cos, sin = _rope_tables(pos0 + jnp.arange(seqlen))

    x1 = _ffn(x, p["ffn1_norm"], p["ffn1_wg"], p["ffn1_wu"], p["ffn1_wd"], tm=min(512, rows))
    u_tb, cq, lat, kr, krp, qm = _inproj(x1, p["mix_norm"], p["w_small"], p["q_norm"], p["kv_norm"],
                                         cos, sin, bsz=bsz, seqlen=seqlen, tm=tm)
    gates = _nmm(x1, p["mix_norm"], p["w_gates"], act="sigmoid", out_dtype=BF16,
                 tm=min(512, rows), tn=1024, name="gates")

    ya_tb, sre, sim = _s5(u_tb, p["s5_bm"], p["s5_cm"], p["s5_d"], p["s5_lre"], p["s5_lim"],
                          h0re, h0im, bsz=bsz, seqlen=seqlen, tb_len=min(256, seqlen))

    q = _qproj(cq, p["w_uq"], cos, sin, seqlen=seqlen, tm=min(256, seqlen))
    if lat_past is None:
        kt, v = _kvproj_t(lat, krp, p["w_ukt"], p["w_uv"], bsz=bsz, seqlen=seqlen, tm=tm)
        ymla = _attn(q, kt, v, bsz=bsz, seqlen=seqlen, tq=256, hg=2)
    else:
        past = lat_past.shape[1]
        n_k = past + seqlen
        lat_all = jnp.concatenate([lat_past, lat.reshape(bsz, seqlen, KV_LORA)], axis=1)
        krp_all = jnp.concatenate([krp_past, krp.reshape(bsz, seqlen, LANES)], axis=1)
        k, v = _kvproj(lat_all.reshape(bsz * n_k, KV_LORA), krp_all.reshape(bsz * n_k, LANES),
                       p["w_uk"], p["w_uv"], tm=n_k)
        ymla = _attn_cached(q, k, v, bsz=bsz, n_q=seqlen, n_k=n_k, past=past)

    ymem = _memattn(qm, mem_k, mem_v, bsz=bsz, seqlen=seqlen, tq=tm)

    merged = _merge(ya_tb, p["w_glu"], p["b_glu"], ymla, ymem, gates,
                    p["w_br_ssm"], p["w_br_mla"], p["w_br_mem"],
                    bsz=bsz, seqlen=seqlen, tm=tm, tn=512)
    x2 = _resmm(x1, merged, p["w_out"], tm=min(512, rows), tn=1024)
    y = _ffn(x2, p["ffn2_norm"], p["ffn2_wg"], p["ffn2_wu"], p["ffn2_wd"], final_g, tm=min(512, rows))
    return y, lat, kr, sre, sim


def kernel(x_prompt, x_sample, cache_kv_latent, cache_k_rope, cache_mem_k, cache_mem_v, state_ssm_re, state_ssm_im, mem_prompt, ffn1_norm, ffn1_w_gate, ffn1_w_up, ffn1_w_down, mix_norm, w_in, q_norm, w_uq, kv_norm, w_uk, w_uv, ssm_a_re, ssm_a_im, ssm_log_dt, ssm_b_re, ssm_b_im, ssm_c_re, ssm_c_im, ssm_d, ssm_w_glu, ssm_b_glu, mem_norm, w_mem_k, w_mem_v, w_br_ssm, w_br_mla, w_br_mem, w_out, ffn2_norm, ffn2_w_gate, ffn2_w_up, ffn2_w_down, final_norm):
    bp, lp, _ = x_prompt.shape
    bs, ls, _ = x_sample.shape
    past = cache_kv_latent.shape[2]
    l = 0
    bf = lambda t: t.astype(BF16)

    wi = w_in[l]
    c0 = SSM_WIDTH + Q_LORA + KV_LORA
    w_kr = wi[:, c0:c0 + QK_ROPE]
    half = QK_ROPE // 2
    zpad = jnp.zeros((D_MODEL, LANES - QK_ROPE), F32)
    w_small = jnp.concatenate([
        wi[:, :c0], w_kr, zpad, w_kr[:, half:], w_kr[:, :half], zpad,
        wi[:, c0 + QK_ROPE:c0 + QK_ROPE + MEM_WIDTH]], axis=1)
    w_gates = wi[:, c0 + QK_ROPE + MEM_WIDTH:]

    wq = w_uq[l].reshape(Q_LORA, MLA_HEADS, QK_NOPE + QK_ROPE)
    wq_r = wq[:, :, QK_NOPE:]
    zq = jnp.zeros((Q_LORA, MLA_HEADS, LANES - QK_ROPE), F32)
    hw = MLA_HEADS * LANES
    w_uq3 = jnp.concatenate([
        wq[:, :, :QK_NOPE].reshape(Q_LORA, hw),
        jnp.concatenate([wq_r, zq], axis=-1).reshape(Q_LORA, hw),
        jnp.concatenate([wq_r[..., half:], wq_r[..., :half], zq], axis=-1).reshape(Q_LORA, hw)], axis=1)

    bm, cm, d, lre, lim = _s5_params(ssm_a_re[l], ssm_a_im[l], ssm_log_dt[l], ssm_b_re[l], ssm_b_im[l],
                                     ssm_c_re[l], ssm_c_im[l], ssm_d[l])
    p = {
        "ffn1_norm": ffn1_norm[l][None], "ffn1_wg": bf(ffn1_w_gate[l]), "ffn1_wu": bf(ffn1_w_up[l]),
        "ffn1_wd": bf(ffn1_w_down[l]),
        "mix_norm": mix_norm[l][None], "w_small": bf(w_small), "w_gates": bf(w_gates),
        "q_norm": q_norm[l][None], "kv_norm": kv_norm[l][None],
        "w_uq": bf(w_uq3), "w_uk": bf(w_uk[l]), "w_ukt": bf(w_uk[l].T), "w_uv": bf(w_uv[l]),
        "s5_bm": bm, "s5_cm": cm, "s5_d": d, "s5_lre": lre, "s5_lim": lim,
        "w_glu": bf(ssm_w_glu[l]), "b_glu": ssm_b_glu[l][None],
        "w_br_ssm": bf(w_br_ssm[l]), "w_br_mla": bf(w_br_mla[l]), "w_br_mem": bf(w_br_mem[l]),
        "w_out": bf(w_out[l]),
        "ffn2_norm": ffn2_norm[l][None], "ffn2_wg": bf(ffn2_w_gate[l]), "ffn2_wu": bf(ffn2_w_up[l]),
        "ffn2_wd": bf(ffn2_w_down[l]),
    }
    fg = final_norm[None]

    w_mem = bf(jnp.concatenate([w_mem_k[l], w_mem_v[l]], axis=1))
    mkv = _nmm(mem_prompt.reshape(bp * MEM_TOKENS, D_MODEL), mem_norm[l][None], w_mem,
               act=None, out_dtype=F32, tm=512, tn=2 * MEM_WIDTH, name="memkv")
    mk_p = mkv[:, :MEM_WIDTH].reshape(bp, MEM_TOKENS, MEM_WIDTH)
    mv_p = mkv[:, MEM_WIDTH:].reshape(bp, MEM_TOKENS, MEM_WIDTH)

    n_state = SSM_GROUPS * SSM_STATE
    zero_state = jnp.zeros((bp, n_state), F32)
    yp, lat_p, kr_p, sre_p, sim_p = _layer(
        x_prompt.reshape(bp * lp, D_MODEL), bp, lp, 0, mk_p, mv_p, None, None,
        zero_state, zero_state, p, fg)

    krp_past = jnp.pad(cache_k_rope[l], ((0, 0), (0, 0), (0, LANES - QK_ROPE)))
    ys, lat_s, kr_s, sre_s, sim_s = _layer(
        x_sample.reshape(bs * ls, D_MODEL), bs, ls, past,
        cache_mem_k[l].reshape(bs, MEM_TOKENS, MEM_WIDTH), cache_mem_v[l].reshape(bs, MEM_TOKENS, MEM_WIDTH),
        cache_kv_latent[l], krp_past,
        state_ssm_re[l].reshape(bs, n_state), state_ssm_im[l].reshape(bs, n_state), p, fg)

    st = lambda t, b: t.reshape(1, b, SSM_GROUPS, SSM_STATE)
    return (yp.reshape(bp, lp, D_MODEL), ys.reshape(bs, ls, D_MODEL),
            lat_p.reshape(1, bp, lp, KV_LORA), kr_p.reshape(1, bp, lp, QK_ROPE),
            mk_p.reshape(1, bp, MEM_TOKENS, MEM_HEADS, MEM_HEAD_DIM),
            mv_p.reshape(1, bp, MEM_TOKENS, MEM_HEADS, MEM_HEAD_DIM),
            st(sre_p, bp), st(sim_p, bp),
            lat_s.reshape(1, bs, ls, KV_LORA), kr_s.reshape(1, bs, ls, QK_ROPE),
            st(sre_s, bs), st(sim_s, bs))
```

```python
import functools
import math

import jax
import jax.numpy as jnp
from jax import lax
from jax.experimental import pallas as pl
from jax.experimental.pallas import tpu as pltpu

F32 = jnp.float32
BF16 = jnp.bfloat16

D_MODEL = 2048
D_FF = 5632
CHUNK = 64
SSM_WIDTH = D_MODEL // 2
SSM_GROUP = 16
SSM_GROUPS = SSM_WIDTH // SSM_GROUP
SSM_STATE = 64
MLA_HEADS = 16
QK_NOPE = 128
QK_ROPE = 64
V_DIM = 128
Q_LORA = 768
KV_LORA = 512
ROPE_THETA = 10000.0
MEM_TOKENS = 256
MEM_HEADS = 4
MEM_HEAD_DIM = 128
MEM_WIDTH = MEM_HEADS * MEM_HEAD_DIM
RMS_EPS = 1e-6
NEG_INF = -1e30
MLA_SCALE = (QK_NOPE + QK_ROPE) ** -0.5
MEM_SCALE = MEM_HEAD_DIM ** -0.5

LANES = 128
HEAD_PAD = 2 * LANES
SSM_CH = 128
SSM_CG = SSM_CH // SSM_GROUP
SSM_CS = SSM_CG * SSM_STATE
MIB = 1024 * 1024


def _cp(sem, vmem_mib):
    return pltpu.CompilerParams(dimension_semantics=sem, vmem_limit_bytes=int(vmem_mib * MIB))


def _dot(a, b):
    return jnp.dot(a, b, preferred_element_type=F32)


def _rms(x, g):
    return x * lax.rsqrt(jnp.mean(x * x, axis=-1, keepdims=True) + RMS_EPS) * g


def _sigmoid(x):
    return 1.0 / (1.0 + jnp.exp(-x))


def _gelu_tanh(x):
    cdf = 0.5 * (1.0 + jnp.tanh(math.sqrt(2.0 / math.pi) * (x + 0.044715 * (x * x * x))))
    return x * cdf


def _ffn_body(x_ref, g_ref, wg_ref, wu_ref, wd_ref, *rest, final):
    if final:
        fg_ref, o_ref, h_scr, acc_scr = rest
    else:
        o_ref, h_scr, acc_scr = rest
    j = pl.program_id(1)

    @pl.when(j == 0)
    def _():
        h_scr[...] = _rms(x_ref[...], g_ref[...]).astype(BF16)
        acc_scr[...] = jnp.zeros_like(acc_scr)

    h = h_scr[...]
    a = _dot(h, wg_ref[...])
    b = _dot(h, wu_ref[...])
    act = ((a * _sigmoid(a)) * b).astype(BF16)
    acc_scr[...] += _dot(act, wd_ref[...])

    @pl.when(j == pl.num_programs(1) - 1)
    def _():
        y = x_ref[...] + 0.5 * acc_scr[...]
        if final:
            y = _rms(y, fg_ref[...])
        o_ref[...] = y


def _ffn(x, g, wg, wu, wd, final_g=None, *, tm=512, tf=512):
    rows = x.shape[0]
    final = final_g is not None
    in_specs = [
        pl.BlockSpec((tm, D_MODEL), lambda i, j: (i, 0)),
        pl.BlockSpec((1, D_MODEL), lambda i, j: (0, 0)),
        pl.BlockSpec((D_MODEL, tf), lambda i, j: (0, j)),
        pl.BlockSpec((D_MODEL, tf), lambda i, j: (0, j)),
        pl.BlockSpec((tf, D_MODEL), lambda i, j: (j, 0)),
    ]
    args = [x, g, wg, wu, wd]
    if final:
        in_specs.append(pl.BlockSpec((1, D_MODEL), lambda i, j: (0, 0)))
        args.append(final_g)
    return pl.pallas_call(
        functools.partial(_ffn_body, final=final),
        grid=(rows // tm, D_FF // tf),
        in_specs=in_specs,
        out_specs=pl.BlockSpec((tm, D_MODEL), lambda i, j: (i, 0)),
        out_shape=jax.ShapeDtypeStruct((rows, D_MODEL), F32),
        scratch_shapes=[pltpu.VMEM((tm, D_MODEL), BF16), pltpu.VMEM((tm, D_MODEL), F32)],
        compiler_params=_cp(("parallel", "arbitrary"), 48),
        name="ffn",
    )(*args)


def _nmm_body(x_ref, g_ref, w_ref, o_ref, h_scr, *, act):
    @pl.when(pl.program_id(1) == 0)
    def _():
        h_scr[...] = _rms(x_ref[...], g_ref[...]).astype(BF16)

    z = _dot(h_scr[...], w_ref[...])
    if act == "sigmoid":
        z = _sigmoid(z)
    o_ref[...] = z.astype(o_ref.dtype)


def _nmm(x, g, w, *, act, out_dtype, tm, tn, name):
    rows, k = x.shape
    n = w.shape[1]
    return pl.pallas_call(
        functools.partial(_nmm_body, act=act),
        grid=(rows // tm, n // tn),
        in_specs=[
            pl.BlockSpec((tm, k), lambda i, j: (i, 0)),
            pl.BlockSpec((1, k), lambda i, j: (0, 0)),
            pl.BlockSpec((k, tn), lambda i, j: (0, j)),
        ],
        out_specs=pl.BlockSpec((tm, tn), lambda i, j: (i, j)),
        out_shape=jax.ShapeDtypeStruct((rows, n), out_dtype),
        scratch_shapes=[pltpu.VMEM((tm, k), BF16)],
        compiler_params=_cp(("parallel", "arbitrary"), 40),
        name=name,
    )(x, g, w)


_C_U = SSM_WIDTH
_C_Q = _C_U + Q_LORA
_C_KV = _C_Q + KV_LORA
_C_KX = _C_KV + LANES
_C_KS = _C_KX + LANES
_C_QM = _C_KS + MEM_WIDTH


def _inproj_body(x_ref, g_ref, w_ref, qg_ref, kvg_ref, cos_ref, sin_ref,
                 u_ref, cq_ref, lat_ref, kr_ref, krp_ref, qm_ref):
    h = _rms(x_ref[...], g_ref[...]).astype(BF16)
    u_ref[...] = _dot(h, w_ref[:, 0:_C_U])
    cq_ref[...] = _rms(_dot(h, w_ref[:, _C_U:_C_Q]), qg_ref[...]).astype(BF16)
    lat_ref[...] = _rms(_dot(h, w_ref[:, _C_Q:_C_KV]), kvg_ref[...])
    kx = _dot(h, w_ref[:, _C_KV:_C_KX])
    ks = _dot(h, w_ref[:, _C_KX:_C_KS])
    r = kx * cos_ref[...] + ks * sin_ref[...]
    krp_ref[...] = r
    kr_ref[...] = r[:, :QK_ROPE]
    qm_ref[...] = _dot(h, w_ref[:, _C_KS:_C_QM]).astype(BF16)


def _inproj(x, g, w, qg, kvg, cos, sin, *, bsz, seqlen, tm):
    rows = bsz * seqlen
    nt = seqlen // tm
    full = lambda i: (0, 0)
    row = lambda i: (i, 0)
    return pl.pallas_call(
        _inproj_body,
        grid=(rows // tm,),
        in_specs=[
            pl.BlockSpec((tm, D_MODEL), row),
            pl.BlockSpec((1, D_MODEL), full),
            pl.BlockSpec((D_MODEL, _C_QM), full),
            pl.BlockSpec((1, Q_LORA), full),
            pl.BlockSpec((1, KV_LORA), full),
            pl.BlockSpec((tm, LANES), lambda i: (i % nt, 0)),
            pl.BlockSpec((tm, LANES), lambda i: (i % nt, 0)),
        ],
        out_specs=[
            pl.BlockSpec((tm, SSM_WIDTH), lambda i: (i % nt, i // nt)),
            pl.BlockSpec((tm, Q_LORA), row),
            pl.BlockSpec((tm, KV_LORA), row),
            pl.BlockSpec((tm, QK_ROPE), row),
            pl.BlockSpec((tm, LANES), row),
            pl.BlockSpec((tm, MEM_WIDTH), row),
        ],
        out_shape=[
            jax.ShapeDtypeStruct((seqlen, bsz * SSM_WIDTH), F32),
            jax.ShapeDtypeStruct((rows, Q_LORA), BF16),
            jax.ShapeDtypeStruct((rows, KV_LORA), F32),
            jax.ShapeDtypeStruct((rows, QK_ROPE), F32),
            jax.ShapeDtypeStruct((rows, LANES), F32),
            jax.ShapeDtypeStruct((rows, MEM_WIDTH), BF16),
        ],
        compiler_params=_cp(("parallel",), 56),
        name="inproj",
    )(x, g, w, qg, kvg, cos, sin)


def _qproj_body(cq_ref, w_ref, cos_ref, sin_ref, q_ref):
    cq = cq_ref[...]
    hw = MLA_HEADS * LANES
    nope = _dot(cq, w_ref[:, 0:hw])
    for h in range(MLA_HEADS):
        q_ref[:, h * HEAD_PAD:h * HEAD_PAD + LANES] = nope[:, h * LANES:(h + 1) * LANES].astype(BF16)
    pw = hw // 2
    rx = _dot(cq, w_ref[:, hw:hw + pw])
    rs = _dot(cq, w_ref[:, hw + pw:hw + 2 * pw])
    c = cos_ref[...]
    s = sin_ref[...]
    for j in range(MLA_HEADS // 2):
        sl = slice(j * LANES, (j + 1) * LANES)
        r = (rx[:, sl] * c + rs[:, sl] * s).astype(BF16)
        for h in (2 * j, 2 * j + 1):
            q_ref[:, h * HEAD_PAD + LANES:(h + 1) * HEAD_PAD] = r


def _qproj(cq, w, cos, sin, *, seqlen, tm):
    rows = cq.shape[0]
    nt = seqlen // tm
    return pl.pallas_call(
        _qproj_body,
        grid=(rows // tm,),
        in_specs=[
            pl.BlockSpec((tm, Q_LORA), lambda i: (i, 0)),
            pl.BlockSpec((Q_LORA, 2 * MLA_HEADS * LANES), lambda i: (0, 0)),
            pl.BlockSpec((tm, LANES), lambda i: (i % nt, 0)),
            pl.BlockSpec((tm, LANES), lambda i: (i % nt, 0)),
        ],
        out_specs=pl.BlockSpec((tm, MLA_HEADS * HEAD_PAD), lambda i: (i, 0)),
        out_shape=jax.ShapeDtypeStruct((rows, MLA_HEADS * HEAD_PAD), BF16),
        compiler_params=_cp(("parallel",), 48),
        name="qproj",
    )(cq, w, cos, sin)


def _kvproj_t_body(lat_ref, krp_ref, wukt_ref, wuv_ref, kt_ref, v_ref):
    lat = lat_ref[...]
    v_ref[...] = _dot(lat.astype(BF16), wuv_ref[...]).astype(BF16)
    kt = _dot(wukt_ref[...], lat.T.astype(BF16))
    krt = krp_ref[...].T.astype(BF16)
    krt_odd = jnp.concatenate([krt[QK_ROPE:], krt[:QK_ROPE]], axis=0)
    for h in range(MLA_HEADS):
        kt_ref[h * HEAD_PAD:h * HEAD_PAD + LANES, :] = kt[h * LANES:(h + 1) * LANES, :].astype(BF16)
        kt_ref[h * HEAD_PAD + LANES:(h + 1) * HEAD_PAD, :] = krt_odd if h % 2 else krt


def _kvproj_t(lat, krp, wukt, wuv, *, bsz, seqlen, tm):
    rows = bsz * seqlen
    nt = seqlen // tm
    return pl.pallas_call(
        _kvproj_t_body,
        grid=(rows // tm,),
        in_specs=[
            pl.BlockSpec((tm, KV_LORA), lambda i: (i, 0)),
            pl.BlockSpec((tm, LANES), lambda i: (i, 0)),
            pl.BlockSpec((MLA_HEADS * QK_NOPE, KV_LORA), lambda i: (0, 0)),
            pl.BlockSpec((KV_LORA, MLA_HEADS * V_DIM), lambda i: (0, 0)),
        ],
        out_specs=[
            pl.BlockSpec((None, MLA_HEADS * HEAD_PAD, tm), lambda i: (i // nt, 0, i % nt)),
            pl.BlockSpec((tm, MLA_HEADS * V_DIM), lambda i: (i, 0)),
        ],
        out_shape=[
            jax.ShapeDtypeStruct((bsz, MLA_HEADS * HEAD_PAD, seqlen), BF16),
            jax.ShapeDtypeStruct((rows, MLA_HEADS * V_DIM), BF16),
        ],
        compiler_params=_cp(("parallel",), 48),
        name="kvproj_t",
    )(lat, krp, wukt, wuv)


def _kvproj_body(lat_ref, krp_ref, wuk_ref, wuv_ref, k_ref, v_ref):
    lat = lat_ref[...].astype(BF16)
    v_ref[...] = _dot(lat, wuv_ref[...]).astype(BF16)
    kn = _dot(lat, wuk_ref[...])
    kr = krp_ref[...]
    kr_odd = pltpu.roll(kr, QK_ROPE, axis=1).astype(BF16)
    kr = kr.astype(BF16)
    for h in range(MLA_HEADS):
        k_ref[:, h * HEAD_PAD:h * HEAD_PAD + LANES] = kn[:, h * LANES:(h + 1) * LANES].astype(BF16)
        k_ref[:, h * HEAD_PAD + LANES:(h + 1) * HEAD_PAD] = kr_odd if h % 2 else kr


def _kvproj(lat, krp, wuk, wuv, *, tm):
    rows = lat.shape[0]
    return pl.pallas_call(
        _kvproj_body,
        grid=(rows // tm,),
        in_specs=[
            pl.BlockSpec((tm, KV_LORA), lambda i: (i, 0)),
            pl.BlockSpec((tm, LANES), lambda i: (i, 0)),
            pl.BlockSpec((KV_LORA, MLA_HEADS * QK_NOPE), lambda i: (0, 0)),
            pl.BlockSpec((KV_LORA, MLA_HEADS * V_DIM), lambda i: (0, 0)),
        ],
        out_specs=[
            pl.BlockSpec((tm, MLA_HEADS * HEAD_PAD), lambda i: (i, 0)),
            pl.BlockSpec((tm, MLA_HEADS * V_DIM), lambda i: (i, 0)),
        ],
        out_shape=[
            jax.ShapeDtypeStruct((rows, MLA_HEADS * HEAD_PAD), BF16),
            jax.ShapeDtypeStruct((rows, MLA_HEADS * V_DIM), BF16),
        ],
        compiler_params=_cp(("parallel",), 48),
        name="kvproj",
    )(lat, krp, wuk, wuv)


def _chunk_mask(tq, tk, q0, k0):
    r = lax.broadcasted_iota(jnp.int32, (tq, tk), 0) + q0
    c = lax.broadcasted_iota(jnp.int32, (tq, tk), 1) + k0
    return (c // CHUNK) <= (r // CHUNK)


_MLA_EXP2_SCALE = MLA_SCALE * math.log2(math.e)


def _attn_body(q_ref, kt_ref, v_ref, o_ref, *, tq, hg, nq):
    visible = _chunk_mask(tq, tq, 0, 0)
    for vq in range(nq):
        n_past = vq * tq
        rows = slice(n_past, n_past + tq)
        for h in range(hg):
            q = q_ref[rows, h * HEAD_PAD:(h + 1) * HEAD_PAD]
            hk = slice(h * HEAD_PAD, (h + 1) * HEAD_PAD)
            hv = slice(h * V_DIM, (h + 1) * V_DIM)
            s_d = _dot(q, kt_ref[hk, rows]) * _MLA_EXP2_SCALE
            s_d = jnp.where(visible, s_d, NEG_INF)
            m = jnp.max(s_d, axis=-1, keepdims=True)
            if n_past:
                s_p = _dot(q, kt_ref[hk, 0:n_past]) * _MLA_EXP2_SCALE
                m = jnp.maximum(m, jnp.max(s_p, axis=-1, keepdims=True))
            p_d = jnp.exp2(s_d - m)
            l = jnp.sum(p_d, axis=-1, keepdims=True)
            acc = _dot(p_d.astype(BF16), v_ref[rows, hv])
            if n_past:
                p_p = jnp.exp2(s_p - m)
                l = l + jnp.sum(p_p, axis=-1, keepdims=True)
                acc = acc + _dot(p_p.astype(BF16), v_ref[0:n_past, hv])
            o_ref[rows, hv] = (acc / l).astype(BF16)


def _attn(q, kt, v, *, bsz, seqlen, tq, hg):
    nq = seqlen // tq
    return pl.pallas_call(
        functools.partial(_attn_body, tq=tq, hg=hg, nq=nq),
        grid=(bsz, MLA_HEADS // hg),
        in_specs=[
            pl.BlockSpec((seqlen, hg * HEAD_PAD), lambda b, g: (b, g)),
            pl.BlockSpec((None, hg * HEAD_PAD, seqlen), lambda b, g: (b, g, 0)),
            pl.BlockSpec((seqlen, hg * V_DIM), lambda b, g: (b, g)),
        ],
        out_specs=pl.BlockSpec((seqlen, hg * V_DIM), lambda b, g: (b, g)),
        out_shape=jax.ShapeDtypeStruct((bsz * seqlen, MLA_HEADS * V_DIM), BF16),
        compiler_params=_cp(("parallel", "parallel"), 48),
        name="attn",
    )(q, kt, v)


def _attn_cached_body(q_ref, k_ref, v_ref, o_ref, *, n_q, n_k, past):
    visible = _chunk_mask(n_q, n_k, past, 0)
    for h in range(MLA_HEADS):
        q = q_ref[:, h * HEAD_PAD:(h + 1) * HEAD_PAD]
        k = k_ref[:, h * HEAD_PAD:(h + 1) * HEAD_PAD]
        s = lax.dot_general(q, k, (((1,), (1,)), ((), ())), preferred_element_type=F32) * _MLA_EXP2_SCALE
        s = jnp.where(visible, s, NEG_INF)
        m = jnp.max(s, axis=-1, keepdims=True)
        p = jnp.exp2(s - m)
        l = jnp.sum(p, axis=-1, keepdims=True)
        acc = _dot(p.astype(BF16), v_ref[:, h * V_DIM:(h + 1) * V_DIM])
        o_ref[:, h * V_DIM:(h + 1) * V_DIM] = (acc / l).astype(BF16)


def _attn_cached(q, k, v, *, bsz, n_q, n_k, past):
    return pl.pallas_call(
        functools.partial(_attn_cached_body, n_q=n_q, n_k=n_k, past=past),
        grid=(bsz,),
        in_specs=[
            pl.BlockSpec((n_q, MLA_HEADS * HEAD_PAD), lambda b: (b, 0)),
            pl.BlockSpec((n_k, MLA_HEADS * HEAD_PAD), lambda b: (b, 0)),
            pl.BlockSpec((n_k, MLA_HEADS * V_DIM), lambda b: (b, 0)),
        ],
        out_specs=pl.BlockSpec((n_q, MLA_HEADS * V_DIM), lambda b: (b, 0)),
        out_shape=jax.ShapeDtypeStruct((bsz * n_q, MLA_HEADS * V_DIM), BF16),
        compiler_params=_cp(("parallel",), 48),
        name="attn_cached",
    )(q, k, v)


def _memattn_body(q_ref, k_ref, v_ref, o_ref):
    kt = k_ref[...].T.astype(BF16)
    vb = v_ref[...].astype(BF16)
    for h in range(MEM_HEADS):
        sl = slice(h * MEM_HEAD_DIM, (h + 1) * MEM_HEAD_DIM)
        s = _dot(q_ref[:, sl], kt[sl, :]) * MEM_SCALE
        m = jnp.max(s, axis=-1, keepdims=True)
        p = jnp.exp(s - m)
        l = jnp.sum(p, axis=-1, keepdims=True)
        o_ref[:, sl] = (_dot(p.astype(BF16), vb[:, sl]) / l).astype(BF16)


def _memattn(q, k, v, *, bsz, seqlen, tq):
    nt = seqlen // tq
    return pl.pallas_call(
        _memattn_body,
        grid=(bsz, nt),
        in_specs=[
            pl.BlockSpec((tq, MEM_WIDTH), lambda b, i: (b * nt + i, 0)),
            pl.BlockSpec((None, MEM_TOKENS, MEM_WIDTH), lambda b, i: (b, 0, 0)),
            pl.BlockSpec((None, MEM_TOKENS, MEM_WIDTH), lambda b, i: (b, 0, 0)),
        ],
        out_specs=pl.BlockSpec((tq, MEM_WIDTH), lambda b, i: (b * nt + i, 0)),
        out_shape=jax.ShapeDtypeStruct((bsz * seqlen, MEM_WIDTH), BF16),
        compiler_params=_cp(("parallel", "parallel"), 24),
        name="memattn",
    )(q, k, v)


def _s5_body(u_ref, bm_ref, cm_ref, d_ref, lre_ref, lim_ref, h0re_ref, h0im_ref,
             y_ref, sre_ref, sim_ref, x_scr, st_scr, *, tb_len, bsz):
    tb = pl.program_id(1)

    @pl.when(tb == 0)
    def _():
        st_scr[:, 0:SSM_CS] = h0re_ref[...]
        st_scr[:, SSM_CS:2 * SSM_CS] = h0im_ref[...]

    u = u_ref[...].reshape(tb_len * bsz, SSM_CH)
    x_scr[...] = _dot(u.astype(BF16), bm_ref[0])
    lre = jnp.broadcast_to(lre_ref[...], (bsz, SSM_CS))
    lim = jnp.broadcast_to(lim_ref[...], (bsz, SSM_CS))

    def step(t, carry):
        re, im = carry
        r0 = pl.multiple_of(t * bsz, bsz)
        nre = (lre * re - lim * im) + x_scr[pl.ds(r0, bsz), 0:SSM_CS]
        nim = (lre * im + lim * re) + x_scr[pl.ds(r0, bsz), SSM_CS:2 * SSM_CS]
        x_scr[pl.ds(r0, bsz), 0:SSM_CS] = nre
        x_scr[pl.ds(r0, bsz), SSM_CS:2 * SSM_CS] = nim
        return nre, nim

    re, im = lax.fori_loop(0, tb_len, step,
                           (st_scr[:, 0:SSM_CS], st_scr[:, SSM_CS:2 * SSM_CS]), unroll=8)
    st_scr[:, 0:SSM_CS] = re
    st_scr[:, SSM_CS:2 * SSM_CS] = im

    y = _dot(x_scr[...].astype(BF16), cm_ref[0]) + d_ref[...] * u
    y_ref[...] = _gelu_tanh(y).reshape(tb_len, bsz, SSM_CH)

    @pl.when(tb == pl.num_programs(1) - 1)
    def _():
        sre_ref[...] = re
        sim_ref[...] = im


def _s5(u_tb, bm, cm, d, lre, lim, h0re, h0im, *, bsz, seqlen, tb_len):
    nc = SSM_WIDTH // SSM_CH
    u3 = u_tb.reshape(seqlen, bsz, SSM_WIDTH)
    y, sre, sim = pl.pallas_call(
        functools.partial(_s5_body, tb_len=tb_len, bsz=bsz),
        grid=(nc, seqlen // tb_len),
        in_specs=[
            pl.BlockSpec((tb_len, bsz, SSM_CH), lambda c, t: (t, 0, c)),
            pl.BlockSpec((1, SSM_CH, 2 * SSM_CS), lambda c, t: (c, 0, 0)),
            pl.BlockSpec((1, 2 * SSM_CS, SSM_CH), lambda c, t: (c, 0, 0)),
            pl.BlockSpec((1, SSM_CH), lambda c, t: (0, c)),
            pl.BlockSpec((1, SSM_CS), lambda c, t: (0, c)),
            pl.BlockSpec((1, SSM_CS), lambda c, t: (0, c)),
            pl.BlockSpec((bsz, SSM_CS), lambda c, t: (0, c)),
            pl.BlockSpec((bsz, SSM_CS), lambda c, t: (0, c)),
        ],
        out_specs=[
            pl.BlockSpec((tb_len, bsz, SSM_CH), lambda c, t: (t, 0, c)),
            pl.BlockSpec((bsz, SSM_CS), lambda c, t: (0, c)),
            pl.BlockSpec((bsz, SSM_CS), lambda c, t: (0, c)),
        ],
        out_shape=[
            jax.ShapeDtypeStruct((seqlen, bsz, SSM_WIDTH), F32),
            jax.ShapeDtypeStruct((bsz, SSM_GROUPS * SSM_STATE), F32),
            jax.ShapeDtypeStruct((bsz, SSM_GROUPS * SSM_STATE), F32),
        ],
        scratch_shapes=[
            pltpu.VMEM((tb_len * bsz, 2 * SSM_CS), F32),
            pltpu.VMEM((bsz, 2 * SSM_CS), F32),
        ],
        compiler_params=_cp(("parallel", "arbitrary"), 40),
        name="s5",
    )(u3, bm, cm, d, lre, lim, h0re, h0im)
    return y.reshape(seqlen, bsz * SSM_WIDTH), sre, sim


def _merge_body(ya_ref, wglu_ref, bglu_ref, ymla_ref, ymem_ref, gs_ref, gm_ref, ge_ref,
                wbs_ref, wbm_ref, wbe_ref, o_ref, yg_scr):
    @pl.when(pl.program_id(1) == 0)
    def _():
        y = ya_ref[...]
        gate = _sigmoid(_dot(y.astype(BF16), wglu_ref[...]) + bglu_ref[...])
        yg_scr[...] = (y * gate).astype(BF16)

    m = (gs_ref[...].astype(F32) * _dot(yg_scr[...], wbs_ref[...])
         + gm_ref[...].astype(F32) * _dot(ymla_ref[...], wbm_ref[...])
         + ge_ref[...].astype(F32) * _dot(ymem_ref[...], wbe_ref[...]))
    o_ref[...] = m.astype(BF16)


def _merge(ya_tb, wglu, bglu, ymla, ymem, gates, wbs, wbm, wbe, *, bsz, seqlen, tm, tn):
    rows = bsz * seqlen
    nt = seqlen // tm
    nj = D_MODEL // tn
    return pl.pallas_call(
        _merge_body,
        grid=(rows // tm, nj),
        in_specs=[
            pl.BlockSpec((tm, SSM_WIDTH), lambda i, j: (i % nt, i // nt)),
            pl.BlockSpec((SSM_WIDTH, SSM_WIDTH), lambda i, j: (0, 0)),
            pl.BlockSpec((1, SSM_WIDTH), lambda i, j: (0, 0)),
            pl.BlockSpec((tm, MLA_HEADS * V_DIM), lambda i, j: (i, 0)),
            pl.BlockSpec((tm, MEM_WIDTH), lambda i, j: (i, 0)),
            pl.BlockSpec((tm, tn), lambda i, j: (i, j)),
            pl.BlockSpec((tm, tn), lambda i, j: (i, nj + j)),
            pl.BlockSpec((tm, tn), lambda i, j: (i, 2 * nj + j)),
            pl.BlockSpec((SSM_WIDTH, tn), lambda i, j: (0, j)),
            pl.BlockSpec((MLA_HEADS * V_DIM, tn), lambda i, j: (0, j)),
            pl.BlockSpec((MEM_WIDTH, tn), lambda i, j: (0, j)),
        ],
        out_specs=pl.BlockSpec((tm, tn), lambda i, j: (i, j)),
        out_shape=jax.ShapeDtypeStruct((rows, D_MODEL), BF16),
        scratch_shapes=[pltpu.VMEM((tm, SSM_WIDTH), BF16)],
        compiler_params=_cp(("parallel", "arbitrary"), 40),
        name="merge",
    )(ya_tb, wglu, bglu, ymla, ymem, gates, gates, gates, wbs, wbm, wbe)


def _resmm_body(x_ref, m_ref, w_ref, o_ref):
    o_ref[...] = x_ref[...] + _dot(m_ref[...], w_ref[...])


def _resmm(x, m, w, *, tm, tn):
    rows, n = x.shape
    k = m.shape[1]
    return pl.pallas_call(
        _resmm_body,
        grid=(rows // tm, n // tn),
        in_specs=[
            pl.BlockSpec((tm, tn), lambda i, j: (i, j)),
            pl.BlockSpec((tm, k), lambda i, j: (i, 0)),
            pl.BlockSpec((k, tn), lambda i, j: (0, j)),
        ],
        out_specs=pl.BlockSpec((tm, tn), lambda i, j: (i, j)),
        out_shape=jax.ShapeDtypeStruct((rows, n), F32),
        compiler_params=_cp(("parallel", "arbitrary"), 40),
        name="resmm",
    )(x, m, w)


def _rope_tables(pos):
    half = QK_ROPE // 2
    inv_freq = ROPE_THETA ** (-jnp.arange(half, dtype=F32) / half)
    ang = pos.astype(F32)[:, None] * inv_freq[None, :]
    cos, sin = jnp.cos(ang), jnp.sin(ang)
    zero = jnp.zeros((pos.shape[0], LANES - QK_ROPE), F32)
    k_tabs = (jnp.concatenate([cos, cos, zero], axis=1), jnp.concatenate([-sin, sin, zero], axis=1))
    q_tabs = (jnp.concatenate([cos, cos, cos, cos], axis=1), jnp.concatenate([-sin, sin, -sin, sin], axis=1))
    return k_tabs, q_tabs


def _s5_params(a_re, a_im, log_dt, b_re, b_im, c_re, c_im, d):
    dt = jnp.exp(log_dt)[:, None]
    mag = jnp.exp(a_re * dt)
    phase = a_im * dt
    lb_re, lb_im = mag * jnp.cos(phase), mag * jnp.sin(phase)
    den = a_re * a_re + a_im * a_im
    nr, ni = lb_re - 1.0, lb_im
    z_re = (nr * a_re + ni * a_im) / den
    z_im = (ni * a_re - nr * a_im) / den
    bb_re = z_re[..., None] * b_re - z_im[..., None] * b_im
    bb_im = z_re[..., None] * b_im + z_im[..., None] * b_re
    nc = SSM_GROUPS // SSM_CG
    eye = jnp.eye(SSM_CG, dtype=F32)

    def blk_b(t):
        t = t.reshape(nc, SSM_CG, SSM_STATE, SSM_GROUP)
        return jnp.einsum("cgph,gk->cghkp", t, eye).reshape(nc, SSM_CH, SSM_CS)

    def blk_c(t):
        t = t.reshape(nc, SSM_CG, SSM_GROUP, SSM_STATE)
        return jnp.einsum("cghp,gk->cgpkh", t, eye).reshape(nc, SSM_CS, SSM_CH)

    bm = jnp.concatenate([blk_b(bb_re), blk_b(bb_im)], axis=2).astype(BF16)
    cm = jnp.concatenate([blk_c(c_re), -blk_c(c_im)], axis=1).astype(BF16)
    return (bm, cm, d.reshape(1, SSM_WIDTH),
            lb_re.reshape(1, SSM_GROUPS * SSM_STATE), lb_im.reshape(1, SSM_GROUPS * SSM_STATE))


def _layer(x, bsz, seqlen, pos0, mem_k, mem_v, lat_past, krp_past, h0re, h0im, p, final_g):
    rows = bsz * seqlen
    tm = min(512, seqlen)
    (cos, sin), (qcos, qsin) = _rope_tables(pos0 + jnp.arange(seqlen))

    x1 = _ffn(x, p["ffn1_norm"], p["ffn1_wg"], p["ffn1_wu"], p["ffn1_wd"], tm=min(512, rows))
    u_tb, cq, lat, kr, krp, qm = _inproj(x1, p["mix_norm"], p["w_small"], p["q_norm"], p["kv_norm"],
                                         cos, sin, bsz=bsz, seqlen=seqlen, tm=tm)
    gates = _nmm(x1, p["mix_norm"], p["w_gates"], act="sigmoid", out_dtype=BF16,
                 tm=min(512, rows), tn=1024, name="gates")

    ya_tb, sre, sim = _s5(u_tb, p["s5_bm"], p["s5_cm"], p["s5_d"], p["s5_lre"], p["s5_lim"],
                          h0re, h0im, bsz=bsz, seqlen=seqlen, tb_len=min(256, seqlen))

    q = _qproj(cq, p["w_uq"], qcos, qsin, seqlen=seqlen, tm=min(256, seqlen))
    if lat_past is None:
        kt, v = _kvproj_t(lat, krp, p["w_ukt"], p["w_uv"], bsz=bsz, seqlen=seqlen, tm=tm)
        ymla = _attn(q, kt, v, bsz=bsz, seqlen=seqlen, tq=256, hg=2)
    else:
        past = lat_past.shape[1]
        n_k = past + seqlen
        lat_all = jnp.concatenate([lat_past, lat.reshape(bsz, seqlen, KV_LORA)], axis=1)
        krp_all = jnp.concatenate([krp_past, krp.reshape(bsz, seqlen, LANES)], axis=1)
        k, v = _kvproj(lat_all.reshape(bsz * n_k, KV_LORA), krp_all.reshape(bsz * n_k, LANES),
                       p["w_uk"], p["w_uv"], tm=n_k)
        ymla = _attn_cached(q, k, v, bsz=bsz, n_q=seqlen, n_k=n_k, past=past)

    ymem = _memattn(qm, mem_k, mem_v, bsz=bsz, seqlen=seqlen, tq=tm)

    merged = _merge(ya_tb, p["w_glu"], p["b_glu"], ymla, ymem, gates,
                    p["w_br_ssm"], p["w_br_mla"], p["w_br_mem"],
                    bsz=bsz, seqlen=seqlen, tm=tm, tn=512)
    x2 = _resmm(x1, merged, p["w_out"], tm=min(512, rows), tn=1024)
    y = _ffn(x2, p["ffn2_norm"], p["ffn2_wg"], p["ffn2_wu"], p["ffn2_wd"], final_g, tm=min(512, rows))
    return y, lat, kr, sre, sim


def kernel(x_prompt, x_sample, cache_kv_latent, cache_k_rope, cache_mem_k, cache_mem_v, state_ssm_re, state_ssm_im, mem_prompt, ffn1_norm, ffn1_w_gate, ffn1_w_up, ffn1_w_down, mix_norm, w_in, q_norm, w_uq, kv_norm, w_uk, w_uv, ssm_a_re, ssm_a_im, ssm_log_dt, ssm_b_re, ssm_b_im, ssm_c_re, ssm_c_im, ssm_d, ssm_w_glu, ssm_b_glu, mem_norm, w_mem_k, w_mem_v, w_br_ssm, w_br_mla, w_br_mem, w_out, ffn2_norm, ffn2_w_gate, ffn2_w_up, ffn2_w_down, final_norm):
    bp, lp, _ = x_prompt.shape
    bs, ls, _ = x_sample.shape
    past = cache_kv_latent.shape[2]
    l = 0
    bf = lambda t: t.astype(BF16)

    wi = bf(w_in[l])
    c0 = SSM_WIDTH + Q_LORA + KV_LORA
    w_kr = wi[:, c0:c0 + QK_ROPE]
    half = QK_ROPE // 2
    zpad = jnp.zeros((D_MODEL, LANES - QK_ROPE), BF16)
    w_small = jnp.concatenate([
        wi[:, :c0], w_kr, zpad, w_kr[:, half:], w_kr[:, :half], zpad,
        wi[:, c0 + QK_ROPE:c0 + QK_ROPE + MEM_WIDTH]], axis=1)
    w_gates = wi[:, c0 + QK_ROPE + MEM_WIDTH:]

    wq = w_uq[l].reshape(Q_LORA, MLA_HEADS, QK_NOPE + QK_ROPE)
    wq_r = wq[:, :, QK_NOPE:]
    hw = MLA_HEADS * LANES
    w_uq3 = jnp.concatenate([
        wq[:, :, :QK_NOPE].reshape(Q_LORA, hw),
        wq_r.reshape(Q_LORA, hw // 2),
        jnp.concatenate([wq_r[..., half:], wq_r[..., :half]], axis=-1).reshape(Q_LORA, hw // 2)], axis=1)

    bm, cm, d, lre, lim = _s5_params(ssm_a_re[l], ssm_a_im[l], ssm_log_dt[l], ssm_b_re[l], ssm_b_im[l],
                                     ssm_c_re[l], ssm_c_im[l], ssm_d[l])
    p = {
        "ffn1_norm": ffn1_norm[l][None], "ffn1_wg": bf(ffn1_w_gate[l]), "ffn1_wu": bf(ffn1_w_up[l]),
        "ffn1_wd": bf(ffn1_w_down[l]),
        "mix_norm": mix_norm[l][None], "w_small": bf(w_small), "w_gates": bf(w_gates),
        "q_norm": q_norm[l][None], "kv_norm": kv_norm[l][None],
        "w_uq": bf(w_uq3), "w_uk": bf(w_uk[l]), "w_ukt": bf(w_uk[l].T), "w_uv": bf(w_uv[l]),
        "s5_bm": bm, "s5_cm": cm, "s5_d": d, "s5_lre": lre, "s5_lim": lim,
        "w_glu": bf(ssm_w_glu[l]), "b_glu": ssm_b_glu[l][None],
        "w_br_ssm": bf(w_br_ssm[l]), "w_br_mla": bf(w_br_mla[l]), "w_br_mem": bf(w_br_mem[l]),
        "w_out": bf(w_out[l]),
        "ffn2_norm": ffn2_norm[l][None], "ffn2_wg": bf(ffn2_w_gate[l]), "ffn2_wu": bf(ffn2_w_up[l]),
        "ffn2_wd": bf(ffn2_w_down[l]),
    }
    fg = final_norm[None]

    w_mem = bf(jnp.concatenate([w_mem_k[l], w_mem_v[l]], axis=1))
    mkv = _nmm(mem_prompt.reshape(bp * MEM_TOKENS, D_MODEL), mem_norm[l][None], w_mem,
               act=None, out_dtype=F32, tm=512, tn=2 * MEM_WIDTH, name="memkv")
    mk_p = mkv[:, :MEM_WIDTH].reshape(bp, MEM_TOKENS, MEM_WIDTH)
    mv_p = mkv[:, MEM_WIDTH:].reshape(bp, MEM_TOKENS, MEM_WIDTH)

    n_state = SSM_GROUPS * SSM_STATE
    zero_state = jnp.zeros((bp, n_state), F32)
    yp, lat_p, kr_p, sre_p, sim_p = _layer(
        x_prompt.reshape(bp * lp, D_MODEL), bp, lp, 0, mk_p, mv_p, None, None,
        zero_state, zero_state, p, fg)

    krp_past = jnp.pad(cache_k_rope[l], ((0, 0), (0, 0), (0, LANES - QK_ROPE)))
    ys, lat_s, kr_s, sre_s, sim_s = _layer(
        x_sample.reshape(bs * ls, D_MODEL), bs, ls, past,
        cache_mem_k[l].reshape(bs, MEM_TOKENS, MEM_WIDTH), cache_mem_v[l].reshape(bs, MEM_TOKENS, MEM_WIDTH),
        cache_kv_latent[l], krp_past,
        state_ssm_re[l].reshape(bs, n_state), state_ssm_im[l].reshape(bs, n_state), p, fg)

    st = lambda t, b: t.reshape(1, b, SSM_GROUPS, SSM_STATE)
    return (yp.reshape(bp, lp, D_MODEL), ys.reshape(bs, ls, D_MODEL),
            lat_p.reshape(1, bp, lp, KV_LORA), kr_p.reshape(1, bp, lp, QK_ROPE),
            mk_p.reshape(1, bp, MEM_TOKENS, MEM_HEADS, MEM_HEAD_DIM),
            mv_p.reshape(1, bp, MEM_TOKENS, MEM_HEADS, MEM_HEAD_DIM),
            st(sre_p, bp), st(sim_p, bp),
            lat_s.reshape(1, bs, ls, KV_LORA), kr_s.reshape(1, bs, ls, QK_ROPE),
            st(sre_s, bs), st(sim_s, bs))
```

```python
import functools
import math

import jax
import jax.numpy as jnp
from jax import lax
from jax.experimental import pallas as pl
from jax.experimental.pallas import tpu as pltpu

F32 = jnp.float32
BF16 = jnp.bfloat16

D_MODEL = 2048
D_FF = 5632
CHUNK = 64
SSM_WIDTH = D_MODEL // 2
SSM_GROUP = 16
SSM_GROUPS = SSM_WIDTH // SSM_GROUP
SSM_STATE = 64
MLA_HEADS = 16
QK_NOPE = 128
QK_ROPE = 64
V_DIM = 128
Q_LORA = 768
KV_LORA = 512
ROPE_THETA = 10000.0
MEM_TOKENS = 256
MEM_HEADS = 4
MEM_HEAD_DIM = 128
MEM_WIDTH = MEM_HEADS * MEM_HEAD_DIM
RMS_EPS = 1e-6
NEG_INF = -1e30
MLA_SCALE = (QK_NOPE + QK_ROPE) ** -0.5
MEM_SCALE = MEM_HEAD_DIM ** -0.5

LANES = 128
HEAD_PAD = 2 * LANES
SSM_CH = 128
SSM_CG = SSM_CH // SSM_GROUP
SSM_CS = SSM_CG * SSM_STATE
MIB = 1024 * 1024


def _cp(sem, vmem_mib):
    return pltpu.CompilerParams(dimension_semantics=sem, vmem_limit_bytes=int(vmem_mib * MIB))


def _dot(a, b):
    return jnp.dot(a, b, preferred_element_type=F32)


def _rms(x, g):
    return x * lax.rsqrt(jnp.mean(x * x, axis=-1, keepdims=True) + RMS_EPS) * g


def _sigmoid(x):
    return 1.0 / (1.0 + jnp.exp(-x))


def _gelu_tanh(x):
    cdf = 0.5 * (1.0 + jnp.tanh(math.sqrt(2.0 / math.pi) * (x + 0.044715 * (x * x * x))))
    return x * cdf


def _ffn_body(x_ref, g_ref, wg_ref, wu_ref, wd_ref, *rest, final):
    if final:
        fg_ref, o_ref, h_scr, acc_scr = rest
    else:
        o_ref, h_scr, acc_scr = rest
    j = pl.program_id(1)

    @pl.when(j == 0)
    def _():
        h_scr[...] = _rms(x_ref[...], g_ref[...]).astype(BF16)
        acc_scr[...] = jnp.zeros_like(acc_scr)

    h = h_scr[...]
    a = _dot(h, wg_ref[...])
    b = _dot(h, wu_ref[...])
    act = ((a * _sigmoid(a)) * b).astype(BF16)
    acc_scr[...] += _dot(act, wd_ref[...])

    @pl.when(j == pl.num_programs(1) - 1)
    def _():
        y = x_ref[...] + 0.5 * acc_scr[...]
        if final:
            y = _rms(y, fg_ref[...])
        o_ref[...] = y


def _ffn(x, g, wg, wu, wd, final_g=None, *, tm=512, tf=512):
    rows = x.shape[0]
    final = final_g is not None
    in_specs = [
        pl.BlockSpec((tm, D_MODEL), lambda i, j: (i, 0)),
        pl.BlockSpec((1, D_MODEL), lambda i, j: (0, 0)),
        pl.BlockSpec((D_MODEL, tf), lambda i, j: (0, j)),
        pl.BlockSpec((D_MODEL, tf), lambda i, j: (0, j)),
        pl.BlockSpec((tf, D_MODEL), lambda i, j: (j, 0)),
    ]
    args = [x, g, wg, wu, wd]
    if final:
        in_specs.append(pl.BlockSpec((1, D_MODEL), lambda i, j: (0, 0)))
        args.append(final_g)
    return pl.pallas_call(
        functools.partial(_ffn_body, final=final),
        grid=(rows // tm, D_FF // tf),
        in_specs=in_specs,
        out_specs=pl.BlockSpec((tm, D_MODEL), lambda i, j: (i, 0)),
        out_shape=jax.ShapeDtypeStruct((rows, D_MODEL), F32),
        scratch_shapes=[pltpu.VMEM((tm, D_MODEL), BF16), pltpu.VMEM((tm, D_MODEL), F32)],
        compiler_params=_cp(("parallel", "arbitrary"), 48),
        name="ffn",
    )(*args)


def _nmm_body(x_ref, g_ref, w_ref, o_ref, h_scr, *, act):
    @pl.when(pl.program_id(1) == 0)
    def _():
        h_scr[...] = _rms(x_ref[...], g_ref[...]).astype(BF16)

    z = _dot(h_scr[...], w_ref[...])
    if act == "sigmoid":
        z = _sigmoid(z)
    o_ref[...] = z.astype(o_ref.dtype)


def _nmm(x, g, w, *, act, out_dtype, tm, tn, name):
    rows, k = x.shape
    n = w.shape[1]
    return pl.pallas_call(
        functools.partial(_nmm_body, act=act),
        grid=(rows // tm, n // tn),
        in_specs=[
            pl.BlockSpec((tm, k), lambda i, j: (i, 0)),
            pl.BlockSpec((1, k), lambda i, j: (0, 0)),
            pl.BlockSpec((k, tn), lambda i, j: (0, j)),
        ],
        out_specs=pl.BlockSpec((tm, tn), lambda i, j: (i, j)),
        out_shape=jax.ShapeDtypeStruct((rows, n), out_dtype),
        scratch_shapes=[pltpu.VMEM((tm, k), BF16)],
        compiler_params=_cp(("parallel", "arbitrary"), 40),
        name=name,
    )(x, g, w)


_C_U = SSM_WIDTH
_C_Q = _C_U + Q_LORA
_C_KV = _C_Q + KV_LORA
_C_KX = _C_KV + LANES
_C_KS = _C_KX + LANES
_C_QM = _C_KS + MEM_WIDTH


def _inproj_body(x_ref, g_ref, w_ref, qg_ref, kvg_ref, cos_ref, sin_ref,
                 u_ref, cq_ref, lat_ref, kr_ref, krp_ref, qm_ref):
    h = _rms(x_ref[...], g_ref[...]).astype(BF16)
    u_ref[...] = _dot(h, w_ref[:, 0:_C_U])
    cq_ref[...] = _rms(_dot(h, w_ref[:, _C_U:_C_Q]), qg_ref[...]).astype(BF16)
    lat_ref[...] = _rms(_dot(h, w_ref[:, _C_Q:_C_KV]), kvg_ref[...])
    kx = _dot(h, w_ref[:, _C_KV:_C_KX])
    ks = _dot(h, w_ref[:, _C_KX:_C_KS])
    r = kx * cos_ref[...] + ks * sin_ref[...]
    krp_ref[...] = r
    kr_ref[...] = r[:, :QK_ROPE]
    qm_ref[...] = _dot(h, w_ref[:, _C_KS:_C_QM]).astype(BF16)


def _inproj(x, g, w, qg, kvg, cos, sin, *, bsz, seqlen, tm):
    rows = bsz * seqlen
    nt = seqlen // tm
    full = lambda i: (0, 0)
    row = lambda i: (i, 0)
    return pl.pallas_call(
        _inproj_body,
        grid=(rows // tm,),
        in_specs=[
            pl.BlockSpec((tm, D_MODEL), row),
            pl.BlockSpec((1, D_MODEL), full),
            pl.BlockSpec((D_MODEL, _C_QM), full),
            pl.BlockSpec((1, Q_LORA), full),
            pl.BlockSpec((1, KV_LORA), full),
            pl.BlockSpec((tm, LANES), lambda i: (i % nt, 0)),
            pl.BlockSpec((tm, LANES), lambda i: (i % nt, 0)),
        ],
        out_specs=[
            pl.BlockSpec((tm, SSM_WIDTH), row),
            pl.BlockSpec((tm, Q_LORA), row),
            pl.BlockSpec((tm, KV_LORA), row),
            pl.BlockSpec((tm, QK_ROPE), row),
            pl.BlockSpec((tm, LANES), row),
            pl.BlockSpec((tm, MEM_WIDTH), row),
        ],
        out_shape=[
            jax.ShapeDtypeStruct((rows, SSM_WIDTH), F32),
            jax.ShapeDtypeStruct((rows, Q_LORA), BF16),
            jax.ShapeDtypeStruct((rows, KV_LORA), F32),
            jax.ShapeDtypeStruct((rows, QK_ROPE), F32),
            jax.ShapeDtypeStruct((rows, LANES), F32),
            jax.ShapeDtypeStruct((rows, MEM_WIDTH), BF16),
        ],
        compiler_params=_cp(("parallel",), 56),
        name="inproj",
    )(x, g, w, qg, kvg, cos, sin)


def _qproj_body(cq_ref, w_ref, cos_ref, sin_ref, q_ref):
    cq = cq_ref[...]
    hw = MLA_HEADS * LANES
    nope = _dot(cq, w_ref[:, 0:hw])
    for h in range(MLA_HEADS):
        q_ref[:, h * HEAD_PAD:h * HEAD_PAD + LANES] = nope[:, h * LANES:(h + 1) * LANES].astype(BF16)
    pw = hw // 2
    rx = _dot(cq, w_ref[:, hw:hw + pw])
    rs = _dot(cq, w_ref[:, hw + pw:hw + 2 * pw])
    c = cos_ref[...]
    s = sin_ref[...]
    for j in range(MLA_HEADS // 2):
        sl = slice(j * LANES, (j + 1) * LANES)
        r = (rx[:, sl] * c + rs[:, sl] * s).astype(BF16)
        for h in (2 * j, 2 * j + 1):
            q_ref[:, h * HEAD_PAD + LANES:(h + 1) * HEAD_PAD] = r


def _qproj(cq, w, cos, sin, *, seqlen, tm):
    rows = cq.shape[0]
    nt = seqlen // tm
    return pl.pallas_call(
        _qproj_body,
        grid=(rows // tm,),
        in_specs=[
            pl.BlockSpec((tm, Q_LORA), lambda i: (i, 0)),
            pl.BlockSpec((Q_LORA, 2 * MLA_HEADS * LANES), lambda i: (0, 0)),
            pl.BlockSpec((tm, LANES), lambda i: (i % nt, 0)),
            pl.BlockSpec((tm, LANES), lambda i: (i % nt, 0)),
        ],
        out_specs=pl.BlockSpec((tm, MLA_HEADS * HEAD_PAD), lambda i: (i, 0)),
        out_shape=jax.ShapeDtypeStruct((rows, MLA_HEADS * HEAD_PAD), BF16),
        compiler_params=_cp(("parallel",), 48),
        name="qproj",
    )(cq, w, cos, sin)


def _kvproj_t_body(lat_ref, krp_ref, wukt_ref, wuv_ref, kt_ref, v_ref):
    lat = lat_ref[...]
    v_ref[...] = _dot(lat.astype(BF16), wuv_ref[...]).astype(BF16)
    kt = _dot(wukt_ref[...], lat.T.astype(BF16))
    krt = krp_ref[...].T.astype(BF16)
    krt_odd = jnp.concatenate([krt[QK_ROPE:], krt[:QK_ROPE]], axis=0)
    for h in range(MLA_HEADS):
        kt_ref[h * HEAD_PAD:h * HEAD_PAD + LANES, :] = kt[h * LANES:(h + 1) * LANES, :].astype(BF16)
        kt_ref[h * HEAD_PAD + LANES:(h + 1) * HEAD_PAD, :] = krt_odd if h % 2 else krt


def _kvproj_t(lat, krp, wukt, wuv, *, bsz, seqlen, tm):
    rows = bsz * seqlen
    nt = seqlen // tm
    return pl.pallas_call(
        _kvproj_t_body,
        grid=(rows // tm,),
        in_specs=[
            pl.BlockSpec((tm, KV_LORA), lambda i: (i, 0)),
            pl.BlockSpec((tm, LANES), lambda i: (i, 0)),
            pl.BlockSpec((MLA_HEADS * QK_NOPE, KV_LORA), lambda i: (0, 0)),
            pl.BlockSpec((KV_LORA, MLA_HEADS * V_DIM), lambda i: (0, 0)),
        ],
        out_specs=[
            pl.BlockSpec((None, MLA_HEADS * HEAD_PAD, tm), lambda i: (i // nt, 0, i % nt)),
            pl.BlockSpec((tm, MLA_HEADS * V_DIM), lambda i: (i, 0)),
        ],
        out_shape=[
            jax.ShapeDtypeStruct((bsz, MLA_HEADS * HEAD_PAD, seqlen), BF16),
            jax.ShapeDtypeStruct((rows, MLA_HEADS * V_DIM), BF16),
        ],
        compiler_params=_cp(("parallel",), 48),
        name="kvproj_t",
    )(lat, krp, wukt, wuv)


def _kvproj_body(lat_ref, krp_ref, wuk_ref, wuv_ref, k_ref, v_ref):
    lat = lat_ref[...].astype(BF16)
    v_ref[...] = _dot(lat, wuv_ref[...]).astype(BF16)
    kn = _dot(lat, wuk_ref[...])
    kr = krp_ref[...]
    kr_odd = pltpu.roll(kr, QK_ROPE, axis=1).astype(BF16)
    kr = kr.astype(BF16)
    for h in range(MLA_HEADS):
        k_ref[:, h * HEAD_PAD:h * HEAD_PAD + LANES] = kn[:, h * LANES:(h + 1) * LANES].astype(BF16)
        k_ref[:, h * HEAD_PAD + LANES:(h + 1) * HEAD_PAD] = kr_odd if h % 2 else kr


def _kvproj(lat, krp, wuk, wuv, *, tm):
    rows = lat.shape[0]
    return pl.pallas_call(
        _kvproj_body,
        grid=(rows // tm,),
        in_specs=[
            pl.BlockSpec((tm, KV_LORA), lambda i: (i, 0)),
            pl.BlockSpec((tm, LANES), lambda i: (i, 0)),
            pl.BlockSpec((KV_LORA, MLA_HEADS * QK_NOPE), lambda i: (0, 0)),
            pl.BlockSpec((KV_LORA, MLA_HEADS * V_DIM), lambda i: (0, 0)),
        ],
        out_specs=[
            pl.BlockSpec((tm, MLA_HEADS * HEAD_PAD), lambda i: (i, 0)),
            pl.BlockSpec((tm, MLA_HEADS * V_DIM), lambda i: (i, 0)),
        ],
        out_shape=[
            jax.ShapeDtypeStruct((rows, MLA_HEADS * HEAD_PAD), BF16),
            jax.ShapeDtypeStruct((rows, MLA_HEADS * V_DIM), BF16),
        ],
        compiler_params=_cp(("parallel",), 48),
        name="kvproj",
    )(lat, krp, wuk, wuv)


def _chunk_mask(tq, tk, q0, k0):
    r = lax.broadcasted_iota(jnp.int32, (tq, tk), 0) + q0
    c = lax.broadcasted_iota(jnp.int32, (tq, tk), 1) + k0
    return (c // CHUNK) <= (r // CHUNK)


_MLA_EXP2_SCALE = MLA_SCALE * math.log2(math.e)


def _attn_body(q_ref, kt_ref, v_ref, o_ref, *, tq, hg, nq):
    visible = _chunk_mask(tq, tq, 0, 0)
    for vq in range(nq):
        n_past = vq * tq
        rows = slice(n_past, n_past + tq)
        for h in range(hg):
            q = q_ref[rows, h * HEAD_PAD:(h + 1) * HEAD_PAD]
            hk = slice(h * HEAD_PAD, (h + 1) * HEAD_PAD)
            hv = slice(h * V_DIM, (h + 1) * V_DIM)
            s_d = _dot(q, kt_ref[hk, rows]) * _MLA_EXP2_SCALE
            s_d = jnp.where(visible, s_d, NEG_INF)
            m = jnp.max(s_d, axis=-1, keepdims=True)
            if n_past:
                s_p = _dot(q, kt_ref[hk, 0:n_past]) * _MLA_EXP2_SCALE
                m = jnp.maximum(m, jnp.max(s_p, axis=-1, keepdims=True))
            p_d = jnp.exp2(s_d - m)
            l = jnp.sum(p_d, axis=-1, keepdims=True)
            acc = _dot(p_d.astype(BF16), v_ref[rows, hv])
            if n_past:
                p_p = jnp.exp2(s_p - m)
                l = l + jnp.sum(p_p, axis=-1, keepdims=True)
                acc = acc + _dot(p_p.astype(BF16), v_ref[0:n_past, hv])
            o_ref[rows, hv] = (acc / l).astype(BF16)


def _attn(q, kt, v, *, bsz, seqlen, tq, hg):
    nq = seqlen // tq
    return pl.pallas_call(
        functools.partial(_attn_body, tq=tq, hg=hg, nq=nq),
        grid=(bsz, MLA_HEADS // hg),
        in_specs=[
            pl.BlockSpec((seqlen, hg * HEAD_PAD), lambda b, g: (b, g)),
            pl.BlockSpec((None, hg * HEAD_PAD, seqlen), lambda b, g: (b, g, 0)),
            pl.BlockSpec((seqlen, hg * V_DIM), lambda b, g: (b, g)),
        ],
        out_specs=pl.BlockSpec((seqlen, hg * V_DIM), lambda b, g: (b, g)),
        out_shape=jax.ShapeDtypeStruct((bsz * seqlen, MLA_HEADS * V_DIM), BF16),
        compiler_params=_cp(("parallel", "parallel"), 48),
        name="attn",
    )(q, kt, v)


def _attn_cached_body(q_ref, k_ref, v_ref, o_ref, *, n_q, n_k, past):
    visible = _chunk_mask(n_q, n_k, past, 0)
    for h in range(MLA_HEADS):
        q = q_ref[:, h * HEAD_PAD:(h + 1) * HEAD_PAD]
        k = k_ref[:, h * HEAD_PAD:(h + 1) * HEAD_PAD]
        s = lax.dot_general(q, k, (((1,), (1,)), ((), ())), preferred_element_type=F32) * _MLA_EXP2_SCALE
        s = jnp.where(visible, s, NEG_INF)
        m = jnp.max(s, axis=-1, keepdims=True)
        p = jnp.exp2(s - m)
        l = jnp.sum(p, axis=-1, keepdims=True)
        acc = _dot(p.astype(BF16), v_ref[:, h * V_DIM:(h + 1) * V_DIM])
        o_ref[:, h * V_DIM:(h + 1) * V_DIM] = (acc / l).astype(BF16)


def _attn_cached(q, k, v, *, bsz, n_q, n_k, past):
    return pl.pallas_call(
        functools.partial(_attn_cached_body, n_q=n_q, n_k=n_k, past=past),
        grid=(bsz,),
        in_specs=[
            pl.BlockSpec((n_q, MLA_HEADS * HEAD_PAD), lambda b: (b, 0)),
            pl.BlockSpec((n_k, MLA_HEADS * HEAD_PAD), lambda b: (b, 0)),
            pl.BlockSpec((n_k, MLA_HEADS * V_DIM), lambda b: (b, 0)),
        ],
        out_specs=pl.BlockSpec((n_q, MLA_HEADS * V_DIM), lambda b: (b, 0)),
        out_shape=jax.ShapeDtypeStruct((bsz * n_q, MLA_HEADS * V_DIM), BF16),
        compiler_params=_cp(("parallel",), 48),
        name="attn_cached",
    )(q, k, v)


def _memattn_body(q_ref, k_ref, v_ref, o_ref):
    kt = k_ref[...].T.astype(BF16)
    vb = v_ref[...].astype(BF16)
    for h in range(MEM_HEADS):
        sl = slice(h * MEM_HEAD_DIM, (h + 1) * MEM_HEAD_DIM)
        s = _dot(q_ref[:, sl], kt[sl, :]) * MEM_SCALE
        m = jnp.max(s, axis=-1, keepdims=True)
        p = jnp.exp(s - m)
        l = jnp.sum(p, axis=-1, keepdims=True)
        o_ref[:, sl] = (_dot(p.astype(BF16), vb[:, sl]) / l).astype(BF16)


def _memattn(q, k, v, *, bsz, seqlen, tq):
    nt = seqlen // tq
    return pl.pallas_call(
        _memattn_body,
        grid=(bsz, nt),
        in_specs=[
            pl.BlockSpec((tq, MEM_WIDTH), lambda b, i: (b * nt + i, 0)),
            pl.BlockSpec((None, MEM_TOKENS, MEM_WIDTH), lambda b, i: (b, 0, 0)),
            pl.BlockSpec((None, MEM_TOKENS, MEM_WIDTH), lambda b, i: (b, 0, 0)),
        ],
        out_specs=pl.BlockSpec((tq, MEM_WIDTH), lambda b, i: (b * nt + i, 0)),
        out_shape=jax.ShapeDtypeStruct((bsz * seqlen, MEM_WIDTH), BF16),
        compiler_params=_cp(("parallel", "parallel"), 24),
        name="memattn",
    )(q, k, v)


def _s5_body(*refs, tb_len, bsz):
    u_refs = refs[:bsz]
    (bm_ref, cm_ref, d_ref, lre_ref, lim_ref, h0re_ref, h0im_ref,
     y_ref, sre_ref, sim_ref, u_scr, x_scr, st_scr) = refs[bsz:]
    tb = pl.program_id(1)

    @pl.when(tb == 0)
    def _():
        st_scr[:, 0:SSM_CS] = h0re_ref[...]
        st_scr[:, SSM_CS:2 * SSM_CS] = h0im_ref[...]

    for b in range(bsz):
        u_scr[pl.ds(b, tb_len, stride=bsz), :] = u_refs[b][...]
    u = u_scr[...]
    x_scr[...] = _dot(u.astype(BF16), bm_ref[0])
    lre = jnp.broadcast_to(lre_ref[...], (bsz, SSM_CS))
    lim = jnp.broadcast_to(lim_ref[...], (bsz, SSM_CS))

    def step(t, carry):
        re, im = carry
        r0 = pl.multiple_of(t * bsz, bsz)
        nre = (lre * re - lim * im) + x_scr[pl.ds(r0, bsz), 0:SSM_CS]
        nim = (lre * im + lim * re) + x_scr[pl.ds(r0, bsz), SSM_CS:2 * SSM_CS]
        x_scr[pl.ds(r0, bsz), 0:SSM_CS] = nre
        x_scr[pl.ds(r0, bsz), SSM_CS:2 * SSM_CS] = nim
        return nre, nim

    re, im = lax.fori_loop(0, tb_len, step,
                           (st_scr[:, 0:SSM_CS], st_scr[:, SSM_CS:2 * SSM_CS]), unroll=8)
    st_scr[:, 0:SSM_CS] = re
    st_scr[:, SSM_CS:2 * SSM_CS] = im

    y = _dot(x_scr[...].astype(BF16), cm_ref[0]) + d_ref[...] * u
    y_ref[...] = _gelu_tanh(y)

    @pl.when(tb == pl.num_programs(1) - 1)
    def _():
        sre_ref[...] = re
        sim_ref[...] = im


def _s5(u, bm, cm, d, lre, lim, h0re, h0im, *, bsz, seqlen, tb_len):
    nc = SSM_WIDTH // SSM_CH
    nt = seqlen // tb_len
    u_specs = [pl.BlockSpec((tb_len, SSM_CH), lambda c, t, b=b: (b * nt + t, c)) for b in range(bsz)]
    return pl.pallas_call(
        functools.partial(_s5_body, tb_len=tb_len, bsz=bsz),
        grid=(nc, nt),
        in_specs=u_specs + [
            pl.BlockSpec((1, SSM_CH, 2 * SSM_CS), lambda c, t: (c, 0, 0)),
            pl.BlockSpec((1, 2 * SSM_CS, SSM_CH), lambda c, t: (c, 0, 0)),
            pl.BlockSpec((1, SSM_CH), lambda c, t: (0, c)),
            pl.BlockSpec((1, SSM_CS), lambda c, t: (0, c)),
            pl.BlockSpec((1, SSM_CS), lambda c, t: (0, c)),
            pl.BlockSpec((bsz, SSM_CS), lambda c, t: (0, c)),
            pl.BlockSpec((bsz, SSM_CS), lambda c, t: (0, c)),
        ],
        out_specs=[
            pl.BlockSpec((None, tb_len * bsz, SSM_CH), lambda c, t: (c, t, 0)),
            pl.BlockSpec((bsz, SSM_CS), lambda c, t: (0, c)),
            pl.BlockSpec((bsz, SSM_CS), lambda c, t: (0, c)),
        ],
        out_shape=[
            jax.ShapeDtypeStruct((nc, seqlen * bsz, SSM_CH), F32),
            jax.ShapeDtypeStruct((bsz, SSM_GROUPS * SSM_STATE), F32),
            jax.ShapeDtypeStruct((bsz, SSM_GROUPS * SSM_STATE), F32),
        ],
        scratch_shapes=[
            pltpu.VMEM((tb_len * bsz, SSM_CH), F32),
            pltpu.VMEM((tb_len * bsz, 2 * SSM_CS), F32),
            pltpu.VMEM((bsz, 2 * SSM_CS), F32),
        ],
        compiler_params=_cp(("parallel", "arbitrary"), 40),
        name="s5",
    )(*([u] * bsz), bm, cm, d, lre, lim, h0re, h0im)


def _mergeout_body(x_ref, ya_ref, wglu_ref, bglu_ref, ymla_ref, ymem_ref, g_ref,
                   wbs_ref, wbm_ref, wbe_ref, wo_ref, o_ref, ya_scr, *, bsz, tt):
    rows = bsz * tt
    for c in range(SSM_WIDTH // SSM_CH):
        for b in range(bsz):
            ya_scr[b * tt:(b + 1) * tt, c * SSM_CH:(c + 1) * SSM_CH] = ya_ref[c, pl.ds(b, tt, stride=bsz), :]
    y = ya_scr[...]
    gate = _sigmoid(_dot(y.astype(BF16), wglu_ref[...]) + bglu_ref[...])
    yg = (y * gate).astype(BF16)
    g = g_ref[...].reshape(rows, 3 * D_MODEL)
    ymla = ymla_ref[...].reshape(rows, MLA_HEADS * V_DIM)
    ymem = ymem_ref[...].reshape(rows, MEM_WIDTH)
    m = (g[:, 0:D_MODEL].astype(F32) * _dot(yg, wbs_ref[...])
         + g[:, D_MODEL:2 * D_MODEL].astype(F32) * _dot(ymla, wbm_ref[...])
         + g[:, 2 * D_MODEL:3 * D_MODEL].astype(F32) * _dot(ymem, wbe_ref[...]))
    out = x_ref[...].reshape(rows, D_MODEL) + _dot(m.astype(BF16), wo_ref[...])
    o_ref[...] = out.reshape(bsz, tt, D_MODEL)


def _mergeout(x, ya_slabs, wglu, bglu, ymla, ymem, gates, wbs, wbm, wbe, wo, *, bsz, seqlen, tt):
    rows = bsz * seqlen
    nc = SSM_WIDTH // SSM_CH
    tile = lambda width: pl.BlockSpec((bsz, tt, width), lambda i: (0, i, 0))
    resident = lambda shape: pl.BlockSpec(shape, lambda i: (0, 0), pipeline_mode=pl.Buffered(1))
    v3 = lambda t: t.reshape(bsz, seqlen, t.shape[-1])
    out = pl.pallas_call(
        functools.partial(_mergeout_body, bsz=bsz, tt=tt),
        grid=(seqlen // tt,),
        in_specs=[
            tile(D_MODEL),
            pl.BlockSpec((nc, tt * bsz, SSM_CH), lambda i: (0, i, 0)),
            resident((SSM_WIDTH, SSM_WIDTH)),
            resident((1, SSM_WIDTH)),
            tile(MLA_HEADS * V_DIM),
            tile(MEM_WIDTH),
            tile(3 * D_MODEL),
            resident((SSM_WIDTH, D_MODEL)),
            resident((MLA_HEADS * V_DIM, D_MODEL)),
            resident((MEM_WIDTH, D_MODEL)),
            resident((D_MODEL, D_MODEL)),
        ],
        out_specs=tile(D_MODEL),
        out_shape=jax.ShapeDtypeStruct((bsz, seqlen, D_MODEL), F32),
        scratch_shapes=[pltpu.VMEM((bsz * tt, SSM_WIDTH), F32)],
        compiler_params=_cp(("parallel",), 56),
        name="mergeout",
    )(v3(x), ya_slabs, wglu, bglu, v3(ymla), v3(ymem), v3(gates), wbs, wbm, wbe, wo)
    return out.reshape(rows, D_MODEL)


def _rope_tables(pos):
    half = QK_ROPE // 2
    inv_freq = ROPE_THETA ** (-jnp.arange(half, dtype=F32) / half)
    ang = pos.astype(F32)[:, None] * inv_freq[None, :]
    cos, sin = jnp.cos(ang), jnp.sin(ang)
    zero = jnp.zeros((pos.shape[0], LANES - QK_ROPE), F32)
    k_tabs = (jnp.concatenate([cos, cos, zero], axis=1), jnp.concatenate([-sin, sin, zero], axis=1))
    q_tabs = (jnp.concatenate([cos, cos, cos, cos], axis=1), jnp.concatenate([-sin, sin, -sin, sin], axis=1))
    return k_tabs, q_tabs


def _s5_params(a_re, a_im, log_dt, b_re, b_im, c_re, c_im, d):
    dt = jnp.exp(log_dt)[:, None]
    mag = jnp.exp(a_re * dt)
    phase = a_im * dt
    lb_re, lb_im = mag * jnp.cos(phase), mag * jnp.sin(phase)
    den = a_re * a_re + a_im * a_im
    nr, ni = lb_re - 1.0, lb_im
    z_re = (nr * a_re + ni * a_im) / den
    z_im = (ni * a_re - nr * a_im) / den
    bb_re = z_re[..., None] * b_re - z_im[..., None] * b_im
    bb_im = z_re[..., None] * b_im + z_im[..., None] * b_re
    nc = SSM_GROUPS // SSM_CG
    eye = jnp.eye(SSM_CG, dtype=F32)

    def blk_b(t):
        t = t.reshape(nc, SSM_CG, SSM_STATE, SSM_GROUP)
        return jnp.einsum("cgph,gk->cghkp", t, eye).reshape(nc, SSM_CH, SSM_CS)

    def blk_c(t):
        t = t.reshape(nc, SSM_CG, SSM_GROUP, SSM_STATE)
        return jnp.einsum("cghp,gk->cgpkh", t, eye).reshape(nc, SSM_CS, SSM_CH)

    bm = jnp.concatenate([blk_b(bb_re), blk_b(bb_im)], axis=2).astype(BF16)
    cm = jnp.concatenate([blk_c(c_re), -blk_c(c_im)], axis=1).astype(BF16)
    return (bm, cm, d.reshape(1, SSM_WIDTH),
            lb_re.reshape(1, SSM_GROUPS * SSM_STATE), lb_im.reshape(1, SSM_GROUPS * SSM_STATE))


def _layer(x, bsz, seqlen, pos0, mem_k, mem_v, lat_past, krp_past, h0re, h0im, p, final_g):
    rows = bsz * seqlen
    tm = min(512, seqlen)
    (cos, sin), (qcos, qsin) = _rope_tables(pos0 + jnp.arange(seqlen))

    x1 = _ffn(x, p["ffn1_norm"], p["ffn1_wg"], p["ffn1_wu"], p["ffn1_wd"], tm=min(512, rows))
    u_tb, cq, lat, kr, krp, qm = _inproj(x1, p["mix_norm"], p["w_small"], p["q_norm"], p["kv_norm"],
                                         cos, sin, bsz=bsz, seqlen=seqlen, tm=tm)
    gates = _nmm(x1, p["mix_norm"], p["w_gates"], act="sigmoid", out_dtype=BF16,
                 tm=min(1024, rows), tn=1024, name="gates")

    ya_tb, sre, sim = _s5(u_tb, p["s5_bm"], p["s5_cm"], p["s5_d"], p["s5_lre"], p["s5_lim"],
                          h0re, h0im, bsz=bsz, seqlen=seqlen, tb_len=min(256, seqlen))

    q = _qproj(cq, p["w_uq"], qcos, qsin, seqlen=seqlen, tm=min(256, seqlen))
    if lat_past is None:
        kt, v = _kvproj_t(lat, krp, p["w_ukt"], p["w_uv"], bsz=bsz, seqlen=seqlen, tm=tm)
        ymla = _attn(q, kt, v, bsz=bsz, seqlen=seqlen, tq=256, hg=2)
    else:
        past = lat_past.shape[1]
        n_k = past + seqlen
        lat_all = jnp.concatenate([lat_past, lat.reshape(bsz, seqlen, KV_LORA)], axis=1)
        krp_all = jnp.concatenate([krp_past, krp.reshape(bsz, seqlen, LANES)], axis=1)
        k, v = _kvproj(lat_all.reshape(bsz * n_k, KV_LORA), krp_all.reshape(bsz * n_k, LANES),
                       p["w_uk"], p["w_uv"], tm=n_k)
        ymla = _attn_cached(q, k, v, bsz=bsz, n_q=seqlen, n_k=n_k, past=past)

    ymem = _memattn(qm, mem_k, mem_v, bsz=bsz, seqlen=seqlen, tq=tm)

    x2 = _mergeout(x1, ya_tb, p["w_glu"], p["b_glu"], ymla, ymem, gates,
                   p["w_br_ssm"], p["w_br_mla"], p["w_br_mem"], p["w_out"],
                   bsz=bsz, seqlen=seqlen, tt=32)
    y = _ffn(x2, p["ffn2_norm"], p["ffn2_wg"], p["ffn2_wu"], p["ffn2_wd"], final_g, tm=min(512, rows))
    return y, lat, kr, sre, sim


def kernel(x_prompt, x_sample, cache_kv_latent, cache_k_rope, cache_mem_k, cache_mem_v, state_ssm_re, state_ssm_im, mem_prompt, ffn1_norm, ffn1_w_gate, ffn1_w_up, ffn1_w_down, mix_norm, w_in, q_norm, w_uq, kv_norm, w_uk, w_uv, ssm_a_re, ssm_a_im, ssm_log_dt, ssm_b_re, ssm_b_im, ssm_c_re, ssm_c_im, ssm_d, ssm_w_glu, ssm_b_glu, mem_norm, w_mem_k, w_mem_v, w_br_ssm, w_br_mla, w_br_mem, w_out, ffn2_norm, ffn2_w_gate, ffn2_w_up, ffn2_w_down, final_norm):
    bp, lp, _ = x_prompt.shape
    bs, ls, _ = x_sample.shape
    past = cache_kv_latent.shape[2]
    l = 0
    bf = lambda t: t.astype(BF16)

    wi = bf(w_in[l])
    c0 = SSM_WIDTH + Q_LORA + KV_LORA
    w_kr = wi[:, c0:c0 + QK_ROPE]
    half = QK_ROPE // 2
    zpad = jnp.zeros((D_MODEL, LANES - QK_ROPE), BF16)
    w_small = jnp.concatenate([
        wi[:, :c0], w_kr, zpad, w_kr[:, half:], w_kr[:, :half], zpad,
        wi[:, c0 + QK_ROPE:c0 + QK_ROPE + MEM_WIDTH]], axis=1)
    w_gates = wi[:, c0 + QK_ROPE + MEM_WIDTH:]

    wq = w_uq[l].reshape(Q_LORA, MLA_HEADS, QK_NOPE + QK_ROPE)
    wq_r = wq[:, :, QK_NOPE:]
    hw = MLA_HEADS * LANES
    w_uq3 = jnp.concatenate([
        wq[:, :, :QK_NOPE].reshape(Q_LORA, hw),
        wq_r.reshape(Q_LORA, hw // 2),
        jnp.concatenate([wq_r[..., half:], wq_r[..., :half]], axis=-1).reshape(Q_LORA, hw // 2)], axis=1)

    bm, cm, d, lre, lim = _s5_params(ssm_a_re[l], ssm_a_im[l], ssm_log_dt[l], ssm_b_re[l], ssm_b_im[l],
                                     ssm_c_re[l], ssm_c_im[l], ssm_d[l])
    p = {
        "ffn1_norm": ffn1_norm[l][None], "ffn1_wg": bf(ffn1_w_gate[l]), "ffn1_wu": bf(ffn1_w_up[l]),
        "ffn1_wd": bf(ffn1_w_down[l]),
        "mix_norm": mix_norm[l][None], "w_small": bf(w_small), "w_gates": bf(w_gates),
        "q_norm": q_norm[l][None], "kv_norm": kv_norm[l][None],
        "w_uq": bf(w_uq3), "w_uk": bf(w_uk[l]), "w_ukt": bf(w_uk[l].T), "w_uv": bf(w_uv[l]),
        "s5_bm": bm, "s5_cm": cm, "s5_d": d, "s5_lre": lre, "s5_lim": lim,
        "w_glu": bf(ssm_w_glu[l]), "b_glu": ssm_b_glu[l][None],
        "w_br_ssm": bf(w_br_ssm[l]), "w_br_mla": bf(w_br_mla[l]), "w_br_mem": bf(w_br_mem[l]),
        "w_out": bf(w_out[l]),
        "ffn2_norm": ffn2_norm[l][None], "ffn2_wg": bf(ffn2_w_gate[l]), "ffn2_wu": bf(ffn2_w_up[l]),
        "ffn2_wd": bf(ffn2_w_down[l]),
    }
    fg = final_norm[None]

    w_mem = bf(jnp.concatenate([w_mem_k[l], w_mem_v[l]], axis=1))
    mkv = _nmm(mem_prompt.reshape(bp * MEM_TOKENS, D_MODEL), mem_norm[l][None], w_mem,
               act=None, out_dtype=F32, tm=512, tn=2 * MEM_WIDTH, name="memkv")
    mk_p = mkv[:, :MEM_WIDTH].reshape(bp, MEM_TOKENS, MEM_WIDTH)
    mv_p = mkv[:, MEM_WIDTH:].reshape(bp, MEM_TOKENS, MEM_WIDTH)

    n_state = SSM_GROUPS * SSM_STATE
    zero_state = jnp.zeros((bp, n_state), F32)
    yp, lat_p, kr_p, sre_p, sim_p = _layer(
        x_prompt.reshape(bp * lp, D_MODEL), bp, lp, 0, mk_p, mv_p, None, None,
        zero_state, zero_state, p, fg)

    krp_past = jnp.pad(cache_k_rope[l], ((0, 0), (0, 0), (0, LANES - QK_ROPE)))
    ys, lat_s, kr_s, sre_s, sim_s = _layer(
        x_sample.reshape(bs * ls, D_MODEL), bs, ls, past,
        cache_mem_k[l].reshape(bs, MEM_TOKENS, MEM_WIDTH), cache_mem_v[l].reshape(bs, MEM_TOKENS, MEM_WIDTH),
        cache_kv_latent[l], krp_past,
        state_ssm_re[l].reshape(bs, n_state), state_ssm_im[l].reshape(bs, n_state), p, fg)

    st = lambda t, b: t.reshape(1, b, SSM_GROUPS, SSM_STATE)
    return (yp.reshape(bp, lp, D_MODEL), ys.reshape(bs, ls, D_MODEL),
            lat_p.reshape(1, bp, lp, KV_LORA), kr_p.reshape(1, bp, lp, QK_ROPE),
            mk_p.reshape(1, bp, MEM_TOKENS, MEM_HEADS, MEM_HEAD_DIM),
            mv_p.reshape(1, bp, MEM_TOKENS, MEM_HEADS, MEM_HEAD_DIM),
            st(sre_p, bp), st(sim_p, bp),
            lat_s.reshape(1, bs, ls, KV_LORA), kr_s.reshape(1, bs, ls, QK_ROPE),
            st(sre_s, bs), st(sim_s, bs))
```

```python
import functools
import math

import jax
import jax.numpy as jnp
from jax import lax
from jax.experimental import pallas as pl
from jax.experimental.pallas import tpu as pltpu

F32 = jnp.float32
BF16 = jnp.bfloat16

D_MODEL = 2048
D_FF = 5632
CHUNK = 64
SSM_WIDTH = D_MODEL // 2
SSM_GROUP = 16
SSM_GROUPS = SSM_WIDTH // SSM_GROUP
SSM_STATE = 64
MLA_HEADS = 16
QK_NOPE = 128
QK_ROPE = 64
V_DIM = 128
Q_LORA = 768
KV_LORA = 512
ROPE_THETA = 10000.0
MEM_TOKENS = 256
MEM_HEADS = 4
MEM_HEAD_DIM = 128
MEM_WIDTH = MEM_HEADS * MEM_HEAD_DIM
RMS_EPS = 1e-6
NEG_INF = -1e30
MLA_SCALE = (QK_NOPE + QK_ROPE) ** -0.5
MEM_SCALE = MEM_HEAD_DIM ** -0.5

LANES = 128
HEAD_PAD = 2 * LANES
SSM_CH = 128
SSM_CG = SSM_CH // SSM_GROUP
SSM_CS = SSM_CG * SSM_STATE
MIB = 1024 * 1024


def _cp(sem, vmem_mib):
    return pltpu.CompilerParams(dimension_semantics=sem, vmem_limit_bytes=int(vmem_mib * MIB))


def _dot(a, b):
    return jnp.dot(a, b, preferred_element_type=F32)


def _rms(x, g):
    return x * lax.rsqrt(jnp.mean(x * x, axis=-1, keepdims=True) + RMS_EPS) * g


def _sigmoid(x):
    return 1.0 / (1.0 + jnp.exp(-x))


def _gelu_tanh(x):
    cdf = 0.5 * (1.0 + jnp.tanh(math.sqrt(2.0 / math.pi) * (x + 0.044715 * (x * x * x))))
    return x * cdf


def _ffn_body(x_ref, g_ref, wg_ref, wu_ref, wd_ref, *rest, final):
    if final:
        fg_ref, o_ref, h_scr, acc_scr = rest
    else:
        o_ref, h_scr, acc_scr = rest
    j = pl.program_id(1)

    @pl.when(j == 0)
    def _():
        h_scr[...] = _rms(x_ref[...], g_ref[...]).astype(BF16)
        acc_scr[...] = jnp.zeros_like(acc_scr)

    h = h_scr[...]
    a = _dot(h, wg_ref[...])
    b = _dot(h, wu_ref[...])
    act = ((a * _sigmoid(a)) * b).astype(BF16)
    acc_scr[...] += _dot(act, wd_ref[...])

    @pl.when(j == pl.num_programs(1) - 1)
    def _():
        y = x_ref[...] + 0.5 * acc_scr[...]
        if final:
            y = _rms(y, fg_ref[...])
        o_ref[...] = y


def _ffn(x, g, wg, wu, wd, final_g=None, *, tm=512, tf=512):
    rows = x.shape[0]
    final = final_g is not None
    in_specs = [
        pl.BlockSpec((tm, D_MODEL), lambda i, j: (i, 0)),
        pl.BlockSpec((1, D_MODEL), lambda i, j: (0, 0)),
        pl.BlockSpec((D_MODEL, tf), lambda i, j: (0, j)),
        pl.BlockSpec((D_MODEL, tf), lambda i, j: (0, j)),
        pl.BlockSpec((tf, D_MODEL), lambda i, j: (j, 0)),
    ]
    args = [x, g, wg, wu, wd]
    if final:
        in_specs.append(pl.BlockSpec((1, D_MODEL), lambda i, j: (0, 0)))
        args.append(final_g)
    return pl.pallas_call(
        functools.partial(_ffn_body, final=final),
        grid=(rows // tm, D_FF // tf),
        in_specs=in_specs,
        out_specs=pl.BlockSpec((tm, D_MODEL), lambda i, j: (i, 0)),
        out_shape=jax.ShapeDtypeStruct((rows, D_MODEL), F32),
        scratch_shapes=[pltpu.VMEM((tm, D_MODEL), BF16), pltpu.VMEM((tm, D_MODEL), F32)],
        compiler_params=_cp(("parallel", "arbitrary"), 48),
        name="ffn",
    )(*args)


def _nmm_body(x_ref, g_ref, w_ref, o_ref, h_scr, *, act):
    @pl.when(pl.program_id(1) == 0)
    def _():
        h_scr[...] = _rms(x_ref[...], g_ref[...]).astype(BF16)

    z = _dot(h_scr[...], w_ref[...])
    if act == "sigmoid":
        z = _sigmoid(z)
    o_ref[...] = z.astype(o_ref.dtype)


def _nmm(x, g, w, *, act, out_dtype, tm, tn, name):
    rows, k = x.shape
    n = w.shape[1]
    return pl.pallas_call(
        functools.partial(_nmm_body, act=act),
        grid=(rows // tm, n // tn),
        in_specs=[
            pl.BlockSpec((tm, k), lambda i, j: (i, 0)),
            pl.BlockSpec((1, k), lambda i, j: (0, 0)),
            pl.BlockSpec((k, tn), lambda i, j: (0, j)),
        ],
        out_specs=pl.BlockSpec((tm, tn), lambda i, j: (i, j)),
        out_shape=jax.ShapeDtypeStruct((rows, n), out_dtype),
        scratch_shapes=[pltpu.VMEM((tm, k), BF16)],
        compiler_params=_cp(("parallel", "arbitrary"), 40),
        name=name,
    )(x, g, w)


_C_U = SSM_WIDTH
_C_Q = _C_U + Q_LORA
_C_KV = _C_Q + KV_LORA
_C_KX = _C_KV + LANES
_C_KS = _C_KX + LANES
_C_QM = _C_KS + MEM_WIDTH


def _inproj_body(x_ref, g_ref, w_ref, qg_ref, kvg_ref, cos_ref, sin_ref,
                 u_ref, cq_ref, lat_ref, kr_ref, krp_ref, qm_ref):
    h = _rms(x_ref[...], g_ref[...]).astype(BF16)
    u_ref[...] = _dot(h, w_ref[:, 0:_C_U])
    cq_ref[...] = _rms(_dot(h, w_ref[:, _C_U:_C_Q]), qg_ref[...]).astype(BF16)
    lat_ref[...] = _rms(_dot(h, w_ref[:, _C_Q:_C_KV]), kvg_ref[...])
    kx = _dot(h, w_ref[:, _C_KV:_C_KX])
    ks = _dot(h, w_ref[:, _C_KX:_C_KS])
    r = kx * cos_ref[...] + ks * sin_ref[...]
    krp_ref[...] = r
    kr_ref[...] = r[:, :QK_ROPE]
    qm_ref[...] = _dot(h, w_ref[:, _C_KS:_C_QM]).astype(BF16)


def _inproj(x, g, w, qg, kvg, cos, sin, *, bsz, seqlen, tm):
    rows = bsz * seqlen
    nt = seqlen // tm
    full = lambda i: (0, 0)
    row = lambda i: (i, 0)
    return pl.pallas_call(
        _inproj_body,
        grid=(rows // tm,),
        in_specs=[
            pl.BlockSpec((tm, D_MODEL), row),
            pl.BlockSpec((1, D_MODEL), full),
            pl.BlockSpec((D_MODEL, _C_QM), full),
            pl.BlockSpec((1, Q_LORA), full),
            pl.BlockSpec((1, KV_LORA), full),
            pl.BlockSpec((tm, LANES), lambda i: (i % nt, 0)),
            pl.BlockSpec((tm, LANES), lambda i: (i % nt, 0)),
        ],
        out_specs=[
            pl.BlockSpec((tm, SSM_WIDTH), row),
            pl.BlockSpec((tm, Q_LORA), row),
            pl.BlockSpec((tm, KV_LORA), row),
            pl.BlockSpec((tm, QK_ROPE), row),
            pl.BlockSpec((tm, LANES), row),
            pl.BlockSpec((tm, MEM_WIDTH), row),
        ],
        out_shape=[
            jax.ShapeDtypeStruct((rows, SSM_WIDTH), F32),
            jax.ShapeDtypeStruct((rows, Q_LORA), BF16),
            jax.ShapeDtypeStruct((rows, KV_LORA), F32),
            jax.ShapeDtypeStruct((rows, QK_ROPE), F32),
            jax.ShapeDtypeStruct((rows, LANES), F32),
            jax.ShapeDtypeStruct((rows, MEM_WIDTH), BF16),
        ],
        compiler_params=_cp(("parallel",), 56),
        name="inproj",
    )(x, g, w, qg, kvg, cos, sin)


def _qproj_body(cq_ref, w_ref, cos_ref, sin_ref, q_ref):
    cq = cq_ref[...]
    hw = MLA_HEADS * LANES
    nope = _dot(cq, w_ref[:, 0:hw])
    for h in range(MLA_HEADS):
        q_ref[:, h * HEAD_PAD:h * HEAD_PAD + LANES] = nope[:, h * LANES:(h + 1) * LANES].astype(BF16)
    pw = hw // 2
    rx = _dot(cq, w_ref[:, hw:hw + pw])
    rs = _dot(cq, w_ref[:, hw + pw:hw + 2 * pw])
    c = cos_ref[...]
    s = sin_ref[...]
    for j in range(MLA_HEADS // 2):
        sl = slice(j * LANES, (j + 1) * LANES)
        r = (rx[:, sl] * c + rs[:, sl] * s).astype(BF16)
        for h in (2 * j, 2 * j + 1):
            q_ref[:, h * HEAD_PAD + LANES:(h + 1) * HEAD_PAD] = r


def _qproj(cq, w, cos, sin, *, seqlen, tm):
    rows = cq.shape[0]
    nt = seqlen // tm
    return pl.pallas_call(
        _qproj_body,
        grid=(rows // tm,),
        in_specs=[
            pl.BlockSpec((tm, Q_LORA), lambda i: (i, 0)),
            pl.BlockSpec((Q_LORA, 2 * MLA_HEADS * LANES), lambda i: (0, 0)),
            pl.BlockSpec((tm, LANES), lambda i: (i % nt, 0)),
            pl.BlockSpec((tm, LANES), lambda i: (i % nt, 0)),
        ],
        out_specs=pl.BlockSpec((tm, MLA_HEADS * HEAD_PAD), lambda i: (i, 0)),
        out_shape=jax.ShapeDtypeStruct((rows, MLA_HEADS * HEAD_PAD), BF16),
        compiler_params=_cp(("parallel",), 48),
        name="qproj",
    )(cq, w, cos, sin)


def _kvproj_t_body(lat_ref, krp_ref, wukt_ref, wuv_ref, kt_ref, v_ref):
    lat = lat_ref[...]
    v_ref[...] = _dot(lat.astype(BF16), wuv_ref[...]).astype(BF16)
    kt = _dot(wukt_ref[...], lat.T.astype(BF16))
    krt = krp_ref[...].T.astype(BF16)
    krt_odd = jnp.concatenate([krt[QK_ROPE:], krt[:QK_ROPE]], axis=0)
    for h in range(MLA_HEADS):
        kt_ref[h * HEAD_PAD:h * HEAD_PAD + LANES, :] = kt[h * LANES:(h + 1) * LANES, :].astype(BF16)
        kt_ref[h * HEAD_PAD + LANES:(h + 1) * HEAD_PAD, :] = krt_odd if h % 2 else krt


def _kvproj_t(lat, krp, wukt, wuv, *, bsz, seqlen, tm):
    rows = bsz * seqlen
    nt = seqlen // tm
    return pl.pallas_call(
        _kvproj_t_body,
        grid=(rows // tm,),
        in_specs=[
            pl.BlockSpec((tm, KV_LORA), lambda i: (i, 0)),
            pl.BlockSpec((tm, LANES), lambda i: (i, 0)),
            pl.BlockSpec((MLA_HEADS * QK_NOPE, KV_LORA), lambda i: (0, 0)),
            pl.BlockSpec((KV_LORA, MLA_HEADS * V_DIM), lambda i: (0, 0)),
        ],
        out_specs=[
            pl.BlockSpec((None, MLA_HEADS * HEAD_PAD, tm), lambda i: (i // nt, 0, i % nt)),
            pl.BlockSpec((tm, MLA_HEADS * V_DIM), lambda i: (i, 0)),
        ],
        out_shape=[
            jax.ShapeDtypeStruct((bsz, MLA_HEADS * HEAD_PAD, seqlen), BF16),
            jax.ShapeDtypeStruct((rows, MLA_HEADS * V_DIM), BF16),
        ],
        compiler_params=_cp(("parallel",), 48),
        name="kvproj_t",
    )(lat, krp, wukt, wuv)


def _kvproj_body(lat_ref, krp_ref, wuk_ref, wuv_ref, k_ref, v_ref):
    lat = lat_ref[...].astype(BF16)
    v_ref[...] = _dot(lat, wuv_ref[...]).astype(BF16)
    kn = _dot(lat, wuk_ref[...])
    kr = krp_ref[...]
    kr_odd = pltpu.roll(kr, QK_ROPE, axis=1).astype(BF16)
    kr = kr.astype(BF16)
    for h in range(MLA_HEADS):
        k_ref[:, h * HEAD_PAD:h * HEAD_PAD + LANES] = kn[:, h * LANES:(h + 1) * LANES].astype(BF16)
        k_ref[:, h * HEAD_PAD + LANES:(h + 1) * HEAD_PAD] = kr_odd if h % 2 else kr


def _kvproj(lat, krp, wuk, wuv, *, tm):
    rows = lat.shape[0]
    return pl.pallas_call(
        _kvproj_body,
        grid=(rows // tm,),
        in_specs=[
            pl.BlockSpec((tm, KV_LORA), lambda i: (i, 0)),
            pl.BlockSpec((tm, LANES), lambda i: (i, 0)),
            pl.BlockSpec((KV_LORA, MLA_HEADS * QK_NOPE), lambda i: (0, 0)),
            pl.BlockSpec((KV_LORA, MLA_HEADS * V_DIM), lambda i: (0, 0)),
        ],
        out_specs=[
            pl.BlockSpec((tm, MLA_HEADS * HEAD_PAD), lambda i: (i, 0)),
            pl.BlockSpec((tm, MLA_HEADS * V_DIM), lambda i: (i, 0)),
        ],
        out_shape=[
            jax.ShapeDtypeStruct((rows, MLA_HEADS * HEAD_PAD), BF16),
            jax.ShapeDtypeStruct((rows, MLA_HEADS * V_DIM), BF16),
        ],
        compiler_params=_cp(("parallel",), 48),
        name="kvproj",
    )(lat, krp, wuk, wuv)


def _chunk_mask(tq, tk, q0, k0):
    r = lax.broadcasted_iota(jnp.int32, (tq, tk), 0) + q0
    c = lax.broadcasted_iota(jnp.int32, (tq, tk), 1) + k0
    return (c // CHUNK) <= (r // CHUNK)


_MLA_EXP2_SCALE = MLA_SCALE * math.log2(math.e)


def _attn_body(q_ref, kt_ref, v_ref, o_ref, *, tq, hg, nq):
    visible = _chunk_mask(tq, tq, 0, 0)
    for vq in range(nq):
        n_past = vq * tq
        rows = slice(n_past, n_past + tq)
        for h in range(hg):
            q = q_ref[rows, h * HEAD_PAD:(h + 1) * HEAD_PAD]
            hk = slice(h * HEAD_PAD, (h + 1) * HEAD_PAD)
            hv = slice(h * V_DIM, (h + 1) * V_DIM)
            strips = []
            m_acc = None
            for k0 in range(0, n_past + tq, tq):
                s = _dot(q, kt_ref[hk, k0:k0 + tq]) * _MLA_EXP2_SCALE
                if k0 == n_past:
                    s = jnp.where(visible, s, NEG_INF)
                strips.append(s)
                for c0 in range(0, tq, LANES):
                    part = s[:, c0:c0 + LANES]
                    m_acc = part if m_acc is None else jnp.maximum(m_acc, part)
            m = jnp.max(m_acc, axis=-1, keepdims=True)
            l_acc = None
            acc = None
            for k0, s in zip(range(0, n_past + tq, tq), strips):
                p = jnp.exp2(s - m)
                for c0 in range(0, tq, LANES):
                    part = p[:, c0:c0 + LANES]
                    l_acc = part if l_acc is None else l_acc + part
                pv = _dot(p.astype(BF16), v_ref[k0:k0 + tq, hv])
                acc = pv if acc is None else acc + pv
            l = jnp.sum(l_acc, axis=-1, keepdims=True)
            o_ref[rows, hv] = (acc / l).astype(BF16)


def _attn(q, kt, v, *, bsz, seqlen, tq, hg):
    nq = seqlen // tq
    return pl.pallas_call(
        functools.partial(_attn_body, tq=tq, hg=hg, nq=nq),
        grid=(bsz, MLA_HEADS // hg),
        in_specs=[
            pl.BlockSpec((seqlen, hg * HEAD_PAD), lambda b, g: (b, g)),
            pl.BlockSpec((None, hg * HEAD_PAD, seqlen), lambda b, g: (b, g, 0)),
            pl.BlockSpec((seqlen, hg * V_DIM), lambda b, g: (b, g)),
        ],
        out_specs=pl.BlockSpec((seqlen, hg * V_DIM), lambda b, g: (b, g)),
        out_shape=jax.ShapeDtypeStruct((bsz * seqlen, MLA_HEADS * V_DIM), BF16),
        compiler_params=_cp(("parallel", "parallel"), 48),
        name="attn",
    )(q, kt, v)


def _attn_cached_body(q_ref, k_ref, v_ref, o_ref, *, n_q, n_k, past):
    visible = _chunk_mask(n_q, n_k, past, 0)
    for h in range(MLA_HEADS):
        q = q_ref[:, h * HEAD_PAD:(h + 1) * HEAD_PAD]
        k = k_ref[:, h * HEAD_PAD:(h + 1) * HEAD_PAD]
        s = lax.dot_general(q, k, (((1,), (1,)), ((), ())), preferred_element_type=F32) * _MLA_EXP2_SCALE
        s = jnp.where(visible, s, NEG_INF)
        m = jnp.max(s, axis=-1, keepdims=True)
        p = jnp.exp2(s - m)
        l = jnp.sum(p, axis=-1, keepdims=True)
        acc = _dot(p.astype(BF16), v_ref[:, h * V_DIM:(h + 1) * V_DIM])
        o_ref[:, h * V_DIM:(h + 1) * V_DIM] = (acc / l).astype(BF16)


def _attn_cached(q, k, v, *, bsz, n_q, n_k, past):
    return pl.pallas_call(
        functools.partial(_attn_cached_body, n_q=n_q, n_k=n_k, past=past),
        grid=(bsz,),
        in_specs=[
            pl.BlockSpec((n_q, MLA_HEADS * HEAD_PAD), lambda b: (b, 0)),
            pl.BlockSpec((n_k, MLA_HEADS * HEAD_PAD), lambda b: (b, 0)),
            pl.BlockSpec((n_k, MLA_HEADS * V_DIM), lambda b: (b, 0)),
        ],
        out_specs=pl.BlockSpec((n_q, MLA_HEADS * V_DIM), lambda b: (b, 0)),
        out_shape=jax.ShapeDtypeStruct((bsz * n_q, MLA_HEADS * V_DIM), BF16),
        compiler_params=_cp(("parallel",), 48),
        name="attn_cached",
    )(q, k, v)


def _memattn_body(q_ref, k_ref, v_ref, o_ref):
    kt = k_ref[...].T.astype(BF16)
    vb = v_ref[...].astype(BF16)
    for h in range(MEM_HEADS):
        sl = slice(h * MEM_HEAD_DIM, (h + 1) * MEM_HEAD_DIM)
        s = _dot(q_ref[:, sl], kt[sl, :]) * MEM_SCALE
        m = jnp.max(s, axis=-1, keepdims=True)
        p = jnp.exp(s - m)
        l = jnp.sum(p, axis=-1, keepdims=True)
        o_ref[:, sl] = (_dot(p.astype(BF16), vb[:, sl]) / l).astype(BF16)


def _memattn(q, k, v, *, bsz, seqlen, tq):
    nt = seqlen // tq
    return pl.pallas_call(
        _memattn_body,
        grid=(bsz, nt),
        in_specs=[
            pl.BlockSpec((tq, MEM_WIDTH), lambda b, i: (b * nt + i, 0)),
            pl.BlockSpec((None, MEM_TOKENS, MEM_WIDTH), lambda b, i: (b, 0, 0)),
            pl.BlockSpec((None, MEM_TOKENS, MEM_WIDTH), lambda b, i: (b, 0, 0)),
        ],
        out_specs=pl.BlockSpec((tq, MEM_WIDTH), lambda b, i: (b * nt + i, 0)),
        out_shape=jax.ShapeDtypeStruct((bsz * seqlen, MEM_WIDTH), BF16),
        compiler_params=_cp(("parallel", "parallel"), 24),
        name="memattn",
    )(q, k, v)


S5_PAIR = 2


def _s5_body(*refs, tb_len, bsz):
    u_refs = refs[:bsz]
    (bm_ref, cm_ref, d_ref, lre_ref, lim_ref, h0re_ref, h0im_ref,
     y_ref, sre_ref, sim_ref, u_scr, bu_scr, xb_scr, st_scr) = refs[bsz:]
    tb = pl.program_id(1)

    @pl.when(tb == 0)
    def _():
        st_scr[0] = h0re_ref[...]
        st_scr[1] = h0im_ref[...]

    for k in range(S5_PAIR):
        for b in range(bsz):
            u_scr[k, pl.ds(b, tb_len, stride=bsz), :] = u_refs[b][:, k * SSM_CH:(k + 1) * SSM_CH]
    for k in range(S5_PAIR):
        bu_scr[k] = _dot(u_scr[k].astype(BF16), bm_ref[k])

    for k in range(S5_PAIR):
        cs = slice(k * SSM_CS, (k + 1) * SSM_CS)
        lre = jnp.broadcast_to(lre_ref[:, cs], (bsz, SSM_CS))
        lim = jnp.broadcast_to(lim_ref[:, cs], (bsz, SSM_CS))
        re = st_scr[0, :, cs]
        im = st_scr[1, :, cs]
        for t in range(0, tb_len, 2):
            pair = []
            for r0 in (t * bsz, (t + 1) * bsz):
                nre = (lre * re - lim * im) + bu_scr[k, r0:r0 + bsz, 0:SSM_CS]
                nim = (lre * im + lim * re) + bu_scr[k, r0:r0 + bsz, SSM_CS:2 * SSM_CS]
                re, im = nre, nim
                pair.append((nre, nim))
            rows = slice(t * bsz, (t + 2) * bsz)
            xb_scr[k, rows, 0:SSM_CS] = jnp.concatenate([pair[0][0], pair[1][0]], axis=0).astype(BF16)
            xb_scr[k, rows, SSM_CS:2 * SSM_CS] = jnp.concatenate([pair[0][1], pair[1][1]], axis=0).astype(BF16)
        st_scr[0, :, cs] = re
        st_scr[1, :, cs] = im
        y = _dot(xb_scr[k], cm_ref[k]) + d_ref[:, k * SSM_CH:(k + 1) * SSM_CH] * u_scr[k]
        y_ref[k] = _gelu_tanh(y)

    @pl.when(tb == pl.num_programs(1) - 1)
    def _():
        sre_ref[...] = st_scr[0]
        sim_ref[...] = st_scr[1]


def _s5(u, bm, cm, d, lre, lim, h0re, h0im, *, bsz, seqlen, tb_len):
    nc = SSM_WIDTH // SSM_CH
    nt = seqlen // tb_len
    pch, pcs = S5_PAIR * SSM_CH, S5_PAIR * SSM_CS
    u_specs = [pl.BlockSpec((tb_len, pch), lambda g, t, b=b: (b * nt + t, g)) for b in range(bsz)]
    return pl.pallas_call(
        functools.partial(_s5_body, tb_len=tb_len, bsz=bsz),
        grid=(nc // S5_PAIR, nt),
        in_specs=u_specs + [
            pl.BlockSpec((S5_PAIR, SSM_CH, 2 * SSM_CS), lambda g, t: (g, 0, 0)),
            pl.BlockSpec((S5_PAIR, 2 * SSM_CS, SSM_CH), lambda g, t: (g, 0, 0)),
            pl.BlockSpec((1, pch), lambda g, t: (0, g)),
            pl.BlockSpec((1, pcs), lambda g, t: (0, g)),
            pl.BlockSpec((1, pcs), lambda g, t: (0, g)),
            pl.BlockSpec((bsz, pcs), lambda g, t: (0, g)),
            pl.BlockSpec((bsz, pcs), lambda g, t: (0, g)),
        ],
        out_specs=[
            pl.BlockSpec((S5_PAIR, tb_len * bsz, SSM_CH), lambda g, t: (g, t, 0)),
            pl.BlockSpec((bsz, pcs), lambda g, t: (0, g)),
            pl.BlockSpec((bsz, pcs), lambda g, t: (0, g)),
        ],
        out_shape=[
            jax.ShapeDtypeStruct((nc, seqlen * bsz, SSM_CH), F32),
            jax.ShapeDtypeStruct((bsz, SSM_GROUPS * SSM_STATE), F32),
            jax.ShapeDtypeStruct((bsz, SSM_GROUPS * SSM_STATE), F32),
        ],
        scratch_shapes=[
            pltpu.VMEM((S5_PAIR, tb_len * bsz, SSM_CH), F32),
            pltpu.VMEM((S5_PAIR, tb_len * bsz, 2 * SSM_CS), F32),
            pltpu.VMEM((S5_PAIR, tb_len * bsz, 2 * SSM_CS), BF16),
            pltpu.VMEM((2, bsz, pcs), F32),
        ],
        compiler_params=_cp(("parallel", "arbitrary"), 48),
        name="s5",
    )(*([u] * bsz), bm, cm, d, lre, lim, h0re, h0im)


def _mergeout_body(x_ref, ya_ref, wglu_ref, bglu_ref, ymla_ref, ymem_ref, g_ref,
                   wbs_ref, wbm_ref, wbe_ref, wo_ref, o_ref, ya_scr, *, bsz, tt):
    rows = bsz * tt
    for c in range(SSM_WIDTH // SSM_CH):
        for b in range(bsz):
            ya_scr[b * tt:(b + 1) * tt, c * SSM_CH:(c + 1) * SSM_CH] = ya_ref[c, pl.ds(b, tt, stride=bsz), :]
    y = ya_scr[...]
    gate = _sigmoid(_dot(y.astype(BF16), wglu_ref[...]) + bglu_ref[...])
    yg = (y * gate).astype(BF16)
    g = g_ref[...].reshape(rows, 3 * D_MODEL)
    ymla = ymla_ref[...].reshape(rows, MLA_HEADS * V_DIM)
    ymem = ymem_ref[...].reshape(rows, MEM_WIDTH)
    m = (g[:, 0:D_MODEL].astype(F32) * _dot(yg, wbs_ref[...])
         + g[:, D_MODEL:2 * D_MODEL].astype(F32) * _dot(ymla, wbm_ref[...])
         + g[:, 2 * D_MODEL:3 * D_MODEL].astype(F32) * _dot(ymem, wbe_ref[...]))
    out = x_ref[...].reshape(rows, D_MODEL) + _dot(m.astype(BF16), wo_ref[...])
    o_ref[...] = out.reshape(bsz, tt, D_MODEL)


def _mergeout(x, ya_slabs, wglu, bglu, ymla, ymem, gates, wbs, wbm, wbe, wo, *, bsz, seqlen, tt):
    rows = bsz * seqlen
    nc = SSM_WIDTH // SSM_CH
    tile = lambda width: pl.BlockSpec((bsz, tt, width), lambda i: (0, i, 0))
    resident = lambda shape: pl.BlockSpec(shape, lambda i: (0, 0), pipeline_mode=pl.Buffered(1))
    v3 = lambda t: t.reshape(bsz, seqlen, t.shape[-1])
    out = pl.pallas_call(
        functools.partial(_mergeout_body, bsz=bsz, tt=tt),
        grid=(seqlen // tt,),
        in_specs=[
            tile(D_MODEL),
            pl.BlockSpec((nc, tt * bsz, SSM_CH), lambda i: (0, i, 0)),
            resident((SSM_WIDTH, SSM_WIDTH)),
            resident((1, SSM_WIDTH)),
            tile(MLA_HEADS * V_DIM),
            tile(MEM_WIDTH),
            tile(3 * D_MODEL),
            resident((SSM_WIDTH, D_MODEL)),
            resident((MLA_HEADS * V_DIM, D_MODEL)),
            resident((MEM_WIDTH, D_MODEL)),
            resident((D_MODEL, D_MODEL)),
        ],
        out_specs=tile(D_MODEL),
        out_shape=jax.ShapeDtypeStruct((bsz, seqlen, D_MODEL), F32),
        scratch_shapes=[pltpu.VMEM((bsz * tt, SSM_WIDTH), F32)],
        compiler_params=_cp(("parallel",), 56),
        name="mergeout",
    )(v3(x), ya_slabs, wglu, bglu, v3(ymla), v3(ymem), v3(gates), wbs, wbm, wbe, wo)
    return out.reshape(rows, D_MODEL)


def _rope_tables(pos):
    half = QK_ROPE // 2
    inv_freq = ROPE_THETA ** (-jnp.arange(half, dtype=F32) / half)
    ang = pos.astype(F32)[:, None] * inv_freq[None, :]
    cos, sin = jnp.cos(ang), jnp.sin(ang)
    zero = jnp.zeros((pos.shape[0], LANES - QK_ROPE), F32)
    k_tabs = (jnp.concatenate([cos, cos, zero], axis=1), jnp.concatenate([-sin, sin, zero], axis=1))
    q_tabs = (jnp.concatenate([cos, cos, cos, cos], axis=1), jnp.concatenate([-sin, sin, -sin, sin], axis=1))
    return k_tabs, q_tabs


def _s5_params(a_re, a_im, log_dt, b_re, b_im, c_re, c_im, d):
    dt = jnp.exp(log_dt)[:, None]
    mag = jnp.exp(a_re * dt)
    phase = a_im * dt
    lb_re, lb_im = mag * jnp.cos(phase), mag * jnp.sin(phase)
    den = a_re * a_re + a_im * a_im
    nr, ni = lb_re - 1.0, lb_im
    z_re = (nr * a_re + ni * a_im) / den
    z_im = (ni * a_re - nr * a_im) / den
    bb_re = z_re[..., None] * b_re - z_im[..., None] * b_im
    bb_im = z_re[..., None] * b_im + z_im[..., None] * b_re
    nc = SSM_GROUPS // SSM_CG
    eye = jnp.eye(SSM_CG, dtype=F32)

    def blk_b(t):
        t = t.reshape(nc, SSM_CG, SSM_STATE, SSM_GROUP)
        return jnp.einsum("cgph,gk->cghkp", t, eye).reshape(nc, SSM_CH, SSM_CS)

    def blk_c(t):
        t = t.reshape(nc, SSM_CG, SSM_GROUP, SSM_STATE)
        return jnp.einsum("cghp,gk->cgpkh", t, eye).reshape(nc, SSM_CS, SSM_CH)

    bm = jnp.concatenate([blk_b(bb_re), blk_b(bb_im)], axis=2).astype(BF16)
    cm = jnp.concatenate([blk_c(c_re), -blk_c(c_im)], axis=1).astype(BF16)
    return (bm, cm, d.reshape(1, SSM_WIDTH),
            lb_re.reshape(1, SSM_GROUPS * SSM_STATE), lb_im.reshape(1, SSM_GROUPS * SSM_STATE))


def _layer(x, bsz, seqlen, pos0, mem_k, mem_v, lat_past, krp_past, h0re, h0im, p, final_g):
    rows = bsz * seqlen
    tm = min(512, seqlen)
    rep = max(1, min(512, rows) // seqlen)
    (cos, sin), (qcos, qsin) = jax.tree.map(lambda t: jnp.tile(t, (rep, 1)),
                                            _rope_tables(pos0 + jnp.arange(seqlen)))

    x1 = _ffn(x, p["ffn1_norm"], p["ffn1_wg"], p["ffn1_wu"], p["ffn1_wd"], tm=min(512, rows))
    u_tb, cq, lat, kr, krp, qm = _inproj(x1, p["mix_norm"], p["w_small"], p["q_norm"], p["kv_norm"],
                                         cos, sin, bsz=bsz // rep, seqlen=seqlen * rep, tm=tm * rep)
    gates = _nmm(x1, p["mix_norm"], p["w_gates"], act="sigmoid", out_dtype=BF16,
                 tm=min(1024, rows), tn=1024, name="gates")

    ya_tb, sre, sim = _s5(u_tb, p["s5_bm"], p["s5_cm"], p["s5_d"], p["s5_lre"], p["s5_lim"],
                          h0re, h0im, bsz=bsz, seqlen=seqlen, tb_len=min(256, seqlen))

    q = _qproj(cq, p["w_uq"], qcos, qsin, seqlen=seqlen * rep, tm=min(256, seqlen * rep))
    if lat_past is None:
        kt, v = _kvproj_t(lat, krp, p["w_ukt"], p["w_uv"], bsz=bsz, seqlen=seqlen, tm=tm)
        ymla = _attn(q, kt, v, bsz=bsz, seqlen=seqlen, tq=256, hg=2)
    else:
        past = lat_past.shape[1]
        n_k = past + seqlen
        lat_all = jnp.concatenate([lat_past, lat.reshape(bsz, seqlen, KV_LORA)], axis=1)
        krp_all = jnp.concatenate([krp_past, krp.reshape(bsz, seqlen, LANES)], axis=1)
        k, v = _kvproj(lat_all.reshape(bsz * n_k, KV_LORA), krp_all.reshape(bsz * n_k, LANES),
                       p["w_uk"], p["w_uv"], tm=n_k)
        ymla = _attn_cached(q, k, v, bsz=bsz, n_q=seqlen, n_k=n_k, past=past)

    ymem = _memattn(qm, mem_k, mem_v, bsz=bsz, seqlen=seqlen, tq=tm)

    x2 = _mergeout(x1, ya_tb, p["w_glu"], p["b_glu"], ymla, ymem, gates,
                   p["w_br_ssm"], p["w_br_mla"], p["w_br_mem"], p["w_out"],
                   bsz=bsz, seqlen=seqlen, tt=32)
    y = _ffn(x2, p["ffn2_norm"], p["ffn2_wg"], p["ffn2_wu"], p["ffn2_wd"], final_g, tm=min(512, rows))
    return y, lat, kr, sre, sim


def kernel(x_prompt, x_sample, cache_kv_latent, cache_k_rope, cache_mem_k, cache_mem_v, state_ssm_re, state_ssm_im, mem_prompt, ffn1_norm, ffn1_w_gate, ffn1_w_up, ffn1_w_down, mix_norm, w_in, q_norm, w_uq, kv_norm, w_uk, w_uv, ssm_a_re, ssm_a_im, ssm_log_dt, ssm_b_re, ssm_b_im, ssm_c_re, ssm_c_im, ssm_d, ssm_w_glu, ssm_b_glu, mem_norm, w_mem_k, w_mem_v, w_br_ssm, w_br_mla, w_br_mem, w_out, ffn2_norm, ffn2_w_gate, ffn2_w_up, ffn2_w_down, final_norm):
    bp, lp, _ = x_prompt.shape
    bs, ls, _ = x_sample.shape
    past = cache_kv_latent.shape[2]
    l = 0
    bf = lambda t: t.astype(BF16)

    wi = bf(w_in[l])
    c0 = SSM_WIDTH + Q_LORA + KV_LORA
    w_kr = wi[:, c0:c0 + QK_ROPE]
    half = QK_ROPE // 2
    zpad = jnp.zeros((D_MODEL, LANES - QK_ROPE), BF16)
    w_small = jnp.concatenate([
        wi[:, :c0], w_kr, zpad, w_kr[:, half:], w_kr[:, :half], zpad,
        wi[:, c0 + QK_ROPE:c0 + QK_ROPE + MEM_WIDTH]], axis=1)
    w_gates = wi[:, c0 + QK_ROPE + MEM_WIDTH:]

    wq = w_uq[l].reshape(Q_LORA, MLA_HEADS, QK_NOPE + QK_ROPE)
    wq_r = wq[:, :, QK_NOPE:]
    hw = MLA_HEADS * LANES
    w_uq3 = jnp.concatenate([
        wq[:, :, :QK_NOPE].reshape(Q_LORA, hw),
        wq_r.reshape(Q_LORA, hw // 2),
        jnp.concatenate([wq_r[..., half:], wq_r[..., :half]], axis=-1).reshape(Q_LORA, hw // 2)], axis=1)

    bm, cm, d, lre, lim = _s5_params(ssm_a_re[l], ssm_a_im[l], ssm_log_dt[l], ssm_b_re[l], ssm_b_im[l],
                                     ssm_c_re[l], ssm_c_im[l], ssm_d[l])
    p = {
        "ffn1_norm": ffn1_norm[l][None], "ffn1_wg": bf(ffn1_w_gate[l]), "ffn1_wu": bf(ffn1_w_up[l]),
        "ffn1_wd": bf(ffn1_w_down[l]),
        "mix_norm": mix_norm[l][None], "w_small": bf(w_small), "w_gates": bf(w_gates),
        "q_norm": q_norm[l][None], "kv_norm": kv_norm[l][None],
        "w_uq": bf(w_uq3), "w_uk": bf(w_uk[l]), "w_ukt": bf(w_uk[l].T), "w_uv": bf(w_uv[l]),
        "s5_bm": bm, "s5_cm": cm, "s5_d": d, "s5_lre": lre, "s5_lim": lim,
        "w_glu": bf(ssm_w_glu[l]), "b_glu": ssm_b_glu[l][None],
        "w_br_ssm": bf(w_br_ssm[l]), "w_br_mla": bf(w_br_mla[l]), "w_br_mem": bf(w_br_mem[l]),
        "w_out": bf(w_out[l]),
        "ffn2_norm": ffn2_norm[l][None], "ffn2_wg": bf(ffn2_w_gate[l]), "ffn2_wu": bf(ffn2_w_up[l]),
        "ffn2_wd": bf(ffn2_w_down[l]),
    }
    fg = final_norm[None]

    w_mem = bf(jnp.concatenate([w_mem_k[l], w_mem_v[l]], axis=1))
    mkv = _nmm(mem_prompt.reshape(bp * MEM_TOKENS, D_MODEL), mem_norm[l][None], w_mem,
               act=None, out_dtype=F32, tm=512, tn=2 * MEM_WIDTH, name="memkv")
    mk_p = mkv[:, :MEM_WIDTH].reshape(bp, MEM_TOKENS, MEM_WIDTH)
    mv_p = mkv[:, MEM_WIDTH:].reshape(bp, MEM_TOKENS, MEM_WIDTH)

    n_state = SSM_GROUPS * SSM_STATE
    zero_state = jnp.zeros((bp, n_state), F32)
    yp, lat_p, kr_p, sre_p, sim_p = _layer(
        x_prompt.reshape(bp * lp, D_MODEL), bp, lp, 0, mk_p, mv_p, None, None,
        zero_state, zero_state, p, fg)

    krp_past = jnp.pad(cache_k_rope[l], ((0, 0), (0, 0), (0, LANES - QK_ROPE)))
    ys, lat_s, kr_s, sre_s, sim_s = _layer(
        x_sample.reshape(bs * ls, D_MODEL), bs, ls, past,
        cache_mem_k[l].reshape(bs, MEM_TOKENS, MEM_WIDTH), cache_mem_v[l].reshape(bs, MEM_TOKENS, MEM_WIDTH),
        cache_kv_latent[l], krp_past,
        state_ssm_re[l].reshape(bs, n_state), state_ssm_im[l].reshape(bs, n_state), p, fg)

    st = lambda t, b: t.reshape(1, b, SSM_GROUPS, SSM_STATE)
    return (yp.reshape(bp, lp, D_MODEL), ys.reshape(bs, ls, D_MODEL),
            lat_p.reshape(1, bp, lp, KV_LORA), kr_p.reshape(1, bp, lp, QK_ROPE),
            mk_p.reshape(1, bp, MEM_TOKENS, MEM_HEADS, MEM_HEAD_DIM),
            mv_p.reshape(1, bp, MEM_TOKENS, MEM_HEADS, MEM_HEAD_DIM),
            st(sre_p, bp), st(sim_p, bp),
            lat_s.reshape(1, bs, ls, KV_LORA), kr_s.reshape(1, bs, ls, QK_ROPE),
            st(sre_s, bs), st(sim_s, bs))
```

```python
import functools
import math

import jax
import jax.numpy as jnp
from jax import lax
from jax.experimental import pallas as pl
from jax.experimental.pallas import tpu as pltpu

F32 = jnp.float32
BF16 = jnp.bfloat16

D_MODEL = 2048
D_FF = 5632
CHUNK = 64
SSM_WIDTH = D_MODEL // 2
SSM_GROUP = 16
SSM_GROUPS = SSM_WIDTH // SSM_GROUP
SSM_STATE = 64
MLA_HEADS = 16
QK_NOPE = 128
QK_ROPE = 64
V_DIM = 128
Q_LORA = 768
KV_LORA = 512
ROPE_THETA = 10000.0
MEM_TOKENS = 256
MEM_HEADS = 4
MEM_HEAD_DIM = 128
MEM_WIDTH = MEM_HEADS * MEM_HEAD_DIM
RMS_EPS = 1e-6
NEG_INF = -1e30
MLA_SCALE = (QK_NOPE + QK_ROPE) ** -0.5
MEM_SCALE = MEM_HEAD_DIM ** -0.5

LANES = 128
HEAD_PAD = 2 * LANES
SSM_CH = 128
SSM_CG = SSM_CH // SSM_GROUP
SSM_CS = SSM_CG * SSM_STATE
MIB = 1024 * 1024


def _cp(sem, vmem_mib):
    return pltpu.CompilerParams(dimension_semantics=sem, vmem_limit_bytes=int(vmem_mib * MIB))


def _dot(a, b):
    return jnp.dot(a, b, preferred_element_type=F32)


def _rms(x, g):
    return x * lax.rsqrt(jnp.mean(x * x, axis=-1, keepdims=True) + RMS_EPS) * g


def _sigmoid(x):
    return 1.0 / (1.0 + jnp.exp(-x))


def _gelu_tanh(x):
    cdf = 0.5 * (1.0 + jnp.tanh(math.sqrt(2.0 / math.pi) * (x + 0.044715 * (x * x * x))))
    return x * cdf


def _ffn_body(x_ref, g_ref, wg_ref, wu_ref, wd_ref, *rest, final):
    if final:
        fg_ref, o_ref, h_scr, acc_scr = rest
    else:
        o_ref, h_scr, acc_scr = rest
    j = pl.program_id(1)
    last = pl.num_programs(1) - 1

    def contribution(h):
        a = _dot(h, wg_ref[...])
        b = _dot(h, wu_ref[...])
        act = ((a * _sigmoid(a)) * b).astype(BF16)
        return _dot(act, wd_ref[...])

    @pl.when(j == 0)
    def _():
        h = _rms(x_ref[...], g_ref[...]).astype(BF16)
        h_scr[...] = h
        acc_scr[...] = contribution(h)

    @pl.when(jnp.logical_and(j > 0, j < last))
    def _():
        acc_scr[...] += contribution(h_scr[...])

    @pl.when(j == last)
    def _():
        y = x_ref[...] + 0.5 * (acc_scr[...] + contribution(h_scr[...]))
        if final:
            y = _rms(y, fg_ref[...])
        o_ref[...] = y


def _ffn(x, g, wg, wu, wd, final_g=None, *, tm=512, tf=512):
    rows = x.shape[0]
    final = final_g is not None
    in_specs = [
        pl.BlockSpec((tm, D_MODEL), lambda i, j: (i, 0)),
        pl.BlockSpec((1, D_MODEL), lambda i, j: (0, 0)),
        pl.BlockSpec((D_MODEL, tf), lambda i, j: (0, j)),
        pl.BlockSpec((D_MODEL, tf), lambda i, j: (0, j)),
        pl.BlockSpec((tf, D_MODEL), lambda i, j: (j, 0)),
    ]
    args = [x, g, wg, wu, wd]
    if final:
        in_specs.append(pl.BlockSpec((1, D_MODEL), lambda i, j: (0, 0)))
        args.append(final_g)
    return pl.pallas_call(
        functools.partial(_ffn_body, final=final),
        grid=(rows // tm, D_FF // tf),
        in_specs=in_specs,
        out_specs=pl.BlockSpec((tm, D_MODEL), lambda i, j: (i, 0)),
        out_shape=jax.ShapeDtypeStruct((rows, D_MODEL), F32),
        scratch_shapes=[pltpu.VMEM((tm, D_MODEL), BF16), pltpu.VMEM((tm, D_MODEL), F32)],
        compiler_params=_cp(("parallel", "arbitrary"), 48),
        name="ffn",
    )(*args)


def _nmm_body(x_ref, g_ref, w_ref, o_ref, h_scr, *, act):
    def tile(h):
        z = _dot(h, w_ref[...])
        if act == "sigmoid":
            z = _sigmoid(z)
        o_ref[...] = z.astype(o_ref.dtype)

    @pl.when(pl.program_id(1) == 0)
    def _():
        h = _rms(x_ref[...], g_ref[...]).astype(BF16)
        h_scr[...] = h
        tile(h)

    @pl.when(pl.program_id(1) > 0)
    def _():
        tile(h_scr[...])


def _nmm(x, g, w, *, act, out_dtype, tm, tn, name):
    rows, k = x.shape
    n = w.shape[1]
    return pl.pallas_call(
        functools.partial(_nmm_body, act=act),
        grid=(rows // tm, n // tn),
        in_specs=[
            pl.BlockSpec((tm, k), lambda i, j: (i, 0)),
            pl.BlockSpec((1, k), lambda i, j: (0, 0)),
            pl.BlockSpec((k, tn), lambda i, j: (0, j)),
        ],
        out_specs=pl.BlockSpec((tm, tn), lambda i, j: (i, j)),
        out_shape=jax.ShapeDtypeStruct((rows, n), out_dtype),
        scratch_shapes=[pltpu.VMEM((tm, k), BF16)],
        compiler_params=_cp(("parallel", "arbitrary"), 40),
        name=name,
    )(x, g, w)


_C_U = SSM_WIDTH
_C_Q = _C_U + Q_LORA
_C_KV = _C_Q + KV_LORA
_C_KX = _C_KV + LANES
_C_KS = _C_KX + LANES
_C_QM = _C_KS + MEM_WIDTH


def _inproj_body(x_ref, g_ref, w_ref, qg_ref, kvg_ref, cos_ref, sin_ref,
                 u_ref, cq_ref, lat_ref, kr_ref, krp_ref, qm_ref):
    h = _rms(x_ref[...], g_ref[...]).astype(BF16)
    u_ref[...] = _dot(h, w_ref[:, 0:_C_U])
    cq_ref[...] = _rms(_dot(h, w_ref[:, _C_U:_C_Q]), qg_ref[...]).astype(BF16)
    lat_ref[...] = _rms(_dot(h, w_ref[:, _C_Q:_C_KV]), kvg_ref[...])
    kx = _dot(h, w_ref[:, _C_KV:_C_KX])
    ks = _dot(h, w_ref[:, _C_KX:_C_KS])
    r = kx * cos_ref[...] + ks * sin_ref[...]
    krp_ref[...] = r
    kr_ref[...] = r[:, :QK_ROPE]
    qm_ref[...] = _dot(h, w_ref[:, _C_KS:_C_QM]).astype(BF16)


def _inproj(x, g, w, qg, kvg, cos, sin, *, bsz, seqlen, tm):
    rows = bsz * seqlen
    nt = seqlen // tm
    full = lambda i: (0, 0)
    row = lambda i: (i, 0)
    return pl.pallas_call(
        _inproj_body,
        grid=(rows // tm,),
        in_specs=[
            pl.BlockSpec((tm, D_MODEL), row),
            pl.BlockSpec((1, D_MODEL), full),
            pl.BlockSpec((D_MODEL, _C_QM), full),
            pl.BlockSpec((1, Q_LORA), full),
            pl.BlockSpec((1, KV_LORA), full),
            pl.BlockSpec((tm, LANES), lambda i: (i % nt, 0)),
            pl.BlockSpec((tm, LANES), lambda i: (i % nt, 0)),
        ],
        out_specs=[
            pl.BlockSpec((tm, SSM_WIDTH), row),
            pl.BlockSpec((tm, Q_LORA), row),
            pl.BlockSpec((tm, KV_LORA), row),
            pl.BlockSpec((tm, QK_ROPE), row),
            pl.BlockSpec((tm, LANES), row),
            pl.BlockSpec((tm, MEM_WIDTH), row),
        ],
        out_shape=[
            jax.ShapeDtypeStruct((rows, SSM_WIDTH), F32),
            jax.ShapeDtypeStruct((rows, Q_LORA), BF16),
            jax.ShapeDtypeStruct((rows, KV_LORA), F32),
            jax.ShapeDtypeStruct((rows, QK_ROPE), F32),
            jax.ShapeDtypeStruct((rows, LANES), F32),
            jax.ShapeDtypeStruct((rows, MEM_WIDTH), BF16),
        ],
        compiler_params=_cp(("parallel",), 56),
        name="inproj",
    )(x, g, w, qg, kvg, cos, sin)


def _qproj_body(cq_ref, w_ref, cos_ref, sin_ref, q_ref):
    cq = cq_ref[...]
    hw = MLA_HEADS * LANES
    nope = _dot(cq, w_ref[:, 0:hw])
    for h in range(MLA_HEADS):
        q_ref[:, h * HEAD_PAD:h * HEAD_PAD + LANES] = nope[:, h * LANES:(h + 1) * LANES].astype(BF16)
    pw = hw // 2
    rx = _dot(cq, w_ref[:, hw:hw + pw])
    rs = _dot(cq, w_ref[:, hw + pw:hw + 2 * pw])
    c = cos_ref[...]
    s = sin_ref[...]
    for j in range(MLA_HEADS // 2):
        sl = slice(j * LANES, (j + 1) * LANES)
        r = (rx[:, sl] * c + rs[:, sl] * s).astype(BF16)
        for h in (2 * j, 2 * j + 1):
            q_ref[:, h * HEAD_PAD + LANES:(h + 1) * HEAD_PAD] = r


def _qproj(cq, w, cos, sin, *, seqlen, tm):
    rows = cq.shape[0]
    nt = seqlen // tm
    return pl.pallas_call(
        _qproj_body,
        grid=(rows // tm,),
        in_specs=[
            pl.BlockSpec((tm, Q_LORA), lambda i: (i, 0)),
            pl.BlockSpec((Q_LORA, 2 * MLA_HEADS * LANES), lambda i: (0, 0)),
            pl.BlockSpec((tm, LANES), lambda i: (i % nt, 0)),
            pl.BlockSpec((tm, LANES), lambda i: (i % nt, 0)),
        ],
        out_specs=pl.BlockSpec((tm, MLA_HEADS * HEAD_PAD), lambda i: (i, 0)),
        out_shape=jax.ShapeDtypeStruct((rows, MLA_HEADS * HEAD_PAD), BF16),
        compiler_params=_cp(("parallel",), 48),
        name="qproj",
    )(cq, w, cos, sin)


def _kvproj_t_body(lat_ref, krp_ref, wukt_ref, wuv_ref, kt_ref, v_ref):
    lat = lat_ref[...]
    v_ref[...] = _dot(lat.astype(BF16), wuv_ref[...]).astype(BF16)
    kt = _dot(wukt_ref[...], lat.T.astype(BF16))
    krt = krp_ref[...].T.astype(BF16)
    krt_odd = jnp.concatenate([krt[QK_ROPE:], krt[:QK_ROPE]], axis=0)
    for h in range(MLA_HEADS):
        kt_ref[h * HEAD_PAD:h * HEAD_PAD + LANES, :] = kt[h * LANES:(h + 1) * LANES, :].astype(BF16)
        kt_ref[h * HEAD_PAD + LANES:(h + 1) * HEAD_PAD, :] = krt_odd if h % 2 else krt


def _kvproj_t(lat, krp, wukt, wuv, *, bsz, seqlen, tm):
    rows = bsz * seqlen
    nt = seqlen // tm
    return pl.pallas_call(
        _kvproj_t_body,
        grid=(rows // tm,),
        in_specs=[
            pl.BlockSpec((tm, KV_LORA), lambda i: (i, 0)),
            pl.BlockSpec((tm, LANES), lambda i: (i, 0)),
            pl.BlockSpec((MLA_HEADS * QK_NOPE, KV_LORA), lambda i: (0, 0)),
            pl.BlockSpec((KV_LORA, MLA_HEADS * V_DIM), lambda i: (0, 0)),
        ],
        out_specs=[
            pl.BlockSpec((None, MLA_HEADS * HEAD_PAD, tm), lambda i: (i // nt, 0, i % nt)),
            pl.BlockSpec((tm, MLA_HEADS * V_DIM), lambda i: (i, 0)),
        ],
        out_shape=[
            jax.ShapeDtypeStruct((bsz, MLA_HEADS * HEAD_PAD, seqlen), BF16),
            jax.ShapeDtypeStruct((rows, MLA_HEADS * V_DIM), BF16),
        ],
        compiler_params=_cp(("parallel",), 48),
        name="kvproj_t",
    )(lat, krp, wukt, wuv)


LAT_PAD = KV_LORA + LANES


def _qabsorb_body(q_ref, wukt_ref, o_ref):
    ql = _dot(q_ref[:, 0:LANES], wukt_ref[...])
    o_ref[:, 0:KV_LORA] = ql.astype(BF16)
    rope = q_ref[:, LANES:HEAD_PAD]
    lane = lax.broadcasted_iota(jnp.int32, rope.shape, 1)
    own = (lane < QK_ROPE) == (pl.program_id(0) % 2 == 0)
    o_ref[:, KV_LORA:LAT_PAD] = jnp.where(own, rope, jnp.zeros_like(rope))


def _qabsorb(q, wukt):
    rows = q.shape[0]
    return pl.pallas_call(
        _qabsorb_body,
        grid=(MLA_HEADS,),
        in_specs=[
            pl.BlockSpec((rows, HEAD_PAD), lambda h: (0, h)),
            pl.BlockSpec((QK_NOPE, KV_LORA), lambda h: (h, 0)),
        ],
        out_specs=pl.BlockSpec((None, rows, LAT_PAD), lambda h: (h, 0, 0)),
        out_shape=jax.ShapeDtypeStruct((MLA_HEADS, rows, LAT_PAD), BF16),
        compiler_params=_cp(("parallel",), 24),
        name="qabsorb",
    )(q, wukt)


def _vabsorb_body(o_ref, wuv_ref, y_ref):
    y_ref[...] = _dot(o_ref[...], wuv_ref[...]).astype(BF16)


def _vabsorb(o_lat, wuv):
    rows = o_lat.shape[1]
    return pl.pallas_call(
        _vabsorb_body,
        grid=(MLA_HEADS,),
        in_specs=[
            pl.BlockSpec((None, rows, KV_LORA), lambda h: (h, 0, 0)),
            pl.BlockSpec((KV_LORA, V_DIM), lambda h: (0, h)),
        ],
        out_specs=pl.BlockSpec((rows, V_DIM), lambda h: (0, h)),
        out_shape=jax.ShapeDtypeStruct((rows, MLA_HEADS * V_DIM), BF16),
        compiler_params=_cp(("parallel",), 24),
        name="vabsorb",
    )(o_lat, wuv)


def _chunk_mask(tq, tk, q0, k0):
    r = lax.broadcasted_iota(jnp.int32, (tq, tk), 0) + q0
    c = lax.broadcasted_iota(jnp.int32, (tq, tk), 1) + k0
    return (c // CHUNK) <= (r // CHUNK)


_MLA_EXP2_SCALE = MLA_SCALE * math.log2(math.e)


def _attn_body(q_ref, kt_ref, v_ref, o_ref, *, tq, hg, nq):
    visible = _chunk_mask(tq, tq, 0, 0)
    for vq in range(nq):
        n_past = vq * tq
        rows = slice(n_past, n_past + tq)
        for h in range(hg):
            q = q_ref[rows, h * HEAD_PAD:(h + 1) * HEAD_PAD]
            hk = slice(h * HEAD_PAD, (h + 1) * HEAD_PAD)
            hv = slice(h * V_DIM, (h + 1) * V_DIM)
            strips = []
            m_acc = None
            for k0 in range(0, n_past + tq, tq):
                s = _dot(q, kt_ref[hk, k0:k0 + tq]) * _MLA_EXP2_SCALE
                if k0 == n_past:
                    s = jnp.where(visible, s, NEG_INF)
                strips.append(s)
                for c0 in range(0, tq, LANES):
                    part = s[:, c0:c0 + LANES]
                    m_acc = part if m_acc is None else jnp.maximum(m_acc, part)
            m = jnp.max(m_acc, axis=-1, keepdims=True)
            l_acc = None
            acc = None
            for k0, s in zip(range(0, n_past + tq, tq), strips):
                p = jnp.exp2(s - m)
                for c0 in range(0, tq, LANES):
                    part = p[:, c0:c0 + LANES]
                    l_acc = part if l_acc is None else l_acc + part
                pv = _dot(p.astype(BF16), v_ref[k0:k0 + tq, hv])
                acc = pv if acc is None else acc + pv
            l = jnp.sum(l_acc, axis=-1, keepdims=True)
            o_ref[rows, hv] = (acc / l).astype(BF16)


def _attn(q, kt, v, *, bsz, seqlen, tq, hg):
    nq = seqlen // tq
    return pl.pallas_call(
        functools.partial(_attn_body, tq=tq, hg=hg, nq=nq),
        grid=(bsz, MLA_HEADS // hg),
        in_specs=[
            pl.BlockSpec((seqlen, hg * HEAD_PAD), lambda b, g: (b, g)),
            pl.BlockSpec((None, hg * HEAD_PAD, seqlen), lambda b, g: (b, g, 0)),
            pl.BlockSpec((seqlen, hg * V_DIM), lambda b, g: (b, g)),
        ],
        out_specs=pl.BlockSpec((seqlen, hg * V_DIM), lambda b, g: (b, g)),
        out_shape=jax.ShapeDtypeStruct((bsz * seqlen, MLA_HEADS * V_DIM), BF16),
        compiler_params=_cp(("parallel", "parallel"), 48),
        name="attn",
    )(q, kt, v)


def _attn_cached_body(q_ref, k_ref, o_ref, *, n_q, n_k, past):
    q = q_ref[...].reshape(MLA_HEADS * n_q, LAT_PAD)
    k = k_ref[...]
    s = lax.dot_general(q, k, (((1,), (1,)), ((), ())), preferred_element_type=F32) * _MLA_EXP2_SCALE
    s = s.reshape(MLA_HEADS, n_q, n_k)
    s = jnp.where(_chunk_mask(n_q, n_k, past, 0)[None], s, NEG_INF)
    m = jnp.max(s, axis=-1, keepdims=True)
    p = jnp.exp2(s - m)
    l = jnp.sum(p, axis=-1, keepdims=True)
    acc = _dot(p.reshape(MLA_HEADS * n_q, n_k).astype(BF16), k[:, 0:KV_LORA])
    o_ref[...] = (acc.reshape(MLA_HEADS, n_q, KV_LORA) / l).astype(BF16)


def _attn_cached(q_lat, k_lat, *, bsz, n_q, n_k, past):
    return pl.pallas_call(
        functools.partial(_attn_cached_body, n_q=n_q, n_k=n_k, past=past),
        grid=(bsz,),
        in_specs=[
            pl.BlockSpec((MLA_HEADS, n_q, LAT_PAD), lambda b: (0, b, 0)),
            pl.BlockSpec((n_k, LAT_PAD), lambda b: (b, 0)),
        ],
        out_specs=pl.BlockSpec((MLA_HEADS, n_q, KV_LORA), lambda b: (0, b, 0)),
        out_shape=jax.ShapeDtypeStruct((MLA_HEADS, bsz * n_q, KV_LORA), BF16),
        compiler_params=_cp(("parallel",), 48),
        name="attn_cached",
    )(q_lat, k_lat)


def _memattn_body(q_ref, k_ref, v_ref, o_ref):
    kt = k_ref[...].T.astype(BF16)
    vb = v_ref[...].astype(BF16)
    for h in range(MEM_HEADS):
        sl = slice(h * MEM_HEAD_DIM, (h + 1) * MEM_HEAD_DIM)
        s = _dot(q_ref[:, sl], kt[sl, :]) * MEM_SCALE
        m = jnp.max(s, axis=-1, keepdims=True)
        p = jnp.exp(s - m)
        l = jnp.sum(p, axis=-1, keepdims=True)
        o_ref[:, sl] = (_dot(p.astype(BF16), vb[:, sl]) / l).astype(BF16)


def _memattn(q, k, v, *, bsz, seqlen, tq):
    nt = seqlen // tq
    return pl.pallas_call(
        _memattn_body,
        grid=(bsz, nt),
        in_specs=[
            pl.BlockSpec((tq, MEM_WIDTH), lambda b, i: (b * nt + i, 0)),
            pl.BlockSpec((None, MEM_TOKENS, MEM_WIDTH), lambda b, i: (b, 0, 0)),
            pl.BlockSpec((None, MEM_TOKENS, MEM_WIDTH), lambda b, i: (b, 0, 0)),
        ],
        out_specs=pl.BlockSpec((tq, MEM_WIDTH), lambda b, i: (b * nt + i, 0)),
        out_shape=jax.ShapeDtypeStruct((bsz * seqlen, MEM_WIDTH), BF16),
        compiler_params=_cp(("parallel", "parallel"), 24),
        name="memattn",
    )(q, k, v)


S5_PAIR = 2


def _s5_body(*refs, tb_len, bsz):
    u_refs = refs[:bsz]
    (bm_ref, cm_ref, d_ref, lre_ref, lim_ref, h0re_ref, h0im_ref,
     y_ref, sre_ref, sim_ref, u_scr, bu_scr, xb_scr, st_scr) = refs[bsz:]
    tb = pl.program_id(1)

    @pl.when(tb == 0)
    def _():
        st_scr[0] = h0re_ref[...]
        st_scr[1] = h0im_ref[...]

    for k in range(S5_PAIR):
        for b in range(bsz):
            u_scr[k, pl.ds(b, tb_len, stride=bsz), :] = u_refs[b][:, k * SSM_CH:(k + 1) * SSM_CH]
    for k in range(S5_PAIR):
        bu_scr[k] = _dot(u_scr[k].astype(BF16), bm_ref[k])

    for k in range(S5_PAIR):
        cs = slice(k * SSM_CS, (k + 1) * SSM_CS)
        lre = jnp.broadcast_to(lre_ref[:, cs], (bsz, SSM_CS))
        lim = jnp.broadcast_to(lim_ref[:, cs], (bsz, SSM_CS))
        re = st_scr[0, :, cs]
        im = st_scr[1, :, cs]
        for t in range(0, tb_len, 2):
            pair = []
            for r0 in (t * bsz, (t + 1) * bsz):
                nre = (lre * re - lim * im) + bu_scr[k, r0:r0 + bsz, 0:SSM_CS]
                nim = (lre * im + lim * re) + bu_scr[k, r0:r0 + bsz, SSM_CS:2 * SSM_CS]
                re, im = nre, nim
                pair.append((nre, nim))
            rows = slice(t * bsz, (t + 2) * bsz)
            xb_scr[k, rows, 0:SSM_CS] = jnp.concatenate([pair[0][0], pair[1][0]], axis=0).astype(BF16)
            xb_scr[k, rows, SSM_CS:2 * SSM_CS] = jnp.concatenate([pair[0][1], pair[1][1]], axis=0).astype(BF16)
        st_scr[0, :, cs] = re
        st_scr[1, :, cs] = im
        y = _dot(xb_scr[k], cm_ref[k]) + d_ref[:, k * SSM_CH:(k + 1) * SSM_CH] * u_scr[k]
        y_ref[k] = _gelu_tanh(y)

    @pl.when(tb == pl.num_programs(1) - 1)
    def _():
        sre_ref[...] = st_scr[0]
        sim_ref[...] = st_scr[1]


def _s5(u, bm, cm, d, lre, lim, h0re, h0im, *, bsz, seqlen, tb_len):
    nc = SSM_WIDTH // SSM_CH
    nt = seqlen // tb_len
    pch, pcs = S5_PAIR * SSM_CH, S5_PAIR * SSM_CS
    u_specs = [pl.BlockSpec((tb_len, pch), lambda g, t, b=b: (b * nt + t, g)) for b in range(bsz)]
    return pl.pallas_call(
        functools.partial(_s5_body, tb_len=tb_len, bsz=bsz),
        grid=(nc // S5_PAIR, nt),
        in_specs=u_specs + [
            pl.BlockSpec((S5_PAIR, SSM_CH, 2 * SSM_CS), lambda g, t: (g, 0, 0)),
            pl.BlockSpec((S5_PAIR, 2 * SSM_CS, SSM_CH), lambda g, t: (g, 0, 0)),
            pl.BlockSpec((1, pch), lambda g, t: (0, g)),
            pl.BlockSpec((1, pcs), lambda g, t: (0, g)),
            pl.BlockSpec((1, pcs), lambda g, t: (0, g)),
            pl.BlockSpec((bsz, pcs), lambda g, t: (0, g)),
            pl.BlockSpec((bsz, pcs), lambda g, t: (0, g)),
        ],
        out_specs=[
            pl.BlockSpec((S5_PAIR, tb_len * bsz, SSM_CH), lambda g, t: (g, t, 0)),
            pl.BlockSpec((bsz, pcs), lambda g, t: (0, g)),
            pl.BlockSpec((bsz, pcs), lambda g, t: (0, g)),
        ],
        out_shape=[
            jax.ShapeDtypeStruct((nc, seqlen * bsz, SSM_CH), F32),
            jax.ShapeDtypeStruct((bsz, SSM_GROUPS * SSM_STATE), F32),
            jax.ShapeDtypeStruct((bsz, SSM_GROUPS * SSM_STATE), F32),
        ],
        scratch_shapes=[
            pltpu.VMEM((S5_PAIR, tb_len * bsz, SSM_CH), F32),
            pltpu.VMEM((S5_PAIR, tb_len * bsz, 2 * SSM_CS), F32),
            pltpu.VMEM((S5_PAIR, tb_len * bsz, 2 * SSM_CS), BF16),
            pltpu.VMEM((2, bsz, pcs), F32),
        ],
        compiler_params=_cp(("parallel", "arbitrary"), 48),
        name="s5",
    )(*([u] * bsz), bm, cm, d, lre, lim, h0re, h0im)


def _mergeout_body(x_ref, ya_ref, wglu_ref, bglu_ref, ymla_ref, ymem_ref, g_ref,
                   wbs_ref, wbm_ref, wbe_ref, wo_ref, o_ref, ya_scr, *, bsz, tt):
    rows = bsz * tt
    for c in range(SSM_WIDTH // SSM_CH):
        for b in range(bsz):
            ya_scr[b * tt:(b + 1) * tt, c * SSM_CH:(c + 1) * SSM_CH] = ya_ref[c, pl.ds(b, tt, stride=bsz), :]
    y = ya_scr[...]
    gate = _sigmoid(_dot(y.astype(BF16), wglu_ref[...]) + bglu_ref[...])
    yg = (y * gate).astype(BF16)
    g = g_ref[...].reshape(rows, 3 * D_MODEL)
    ymla = ymla_ref[...].reshape(rows, MLA_HEADS * V_DIM)
    ymem = ymem_ref[...].reshape(rows, MEM_WIDTH)
    m = (g[:, 0:D_MODEL].astype(F32) * _dot(yg, wbs_ref[...])
         + g[:, D_MODEL:2 * D_MODEL].astype(F32) * _dot(ymla, wbm_ref[...])
         + g[:, 2 * D_MODEL:3 * D_MODEL].astype(F32) * _dot(ymem, wbe_ref[...]))
    out = x_ref[...].reshape(rows, D_MODEL) + _dot(m.astype(BF16), wo_ref[...])
    o_ref[...] = out.reshape(bsz, tt, D_MODEL)


def _mergeout(x, ya_slabs, wglu, bglu, ymla, ymem, gates, wbs, wbm, wbe, wo, *, bsz, seqlen, tt):
    rows = bsz * seqlen
    nc = SSM_WIDTH // SSM_CH
    tile = lambda width: pl.BlockSpec((bsz, tt, width), lambda i: (0, i, 0))
    resident = lambda shape: pl.BlockSpec(shape, lambda i: (0, 0), pipeline_mode=pl.Buffered(1))
    v3 = lambda t: t.reshape(bsz, seqlen, t.shape[-1])
    out = pl.pallas_call(
        functools.partial(_mergeout_body, bsz=bsz, tt=tt),
        grid=(seqlen // tt,),
        in_specs=[
            tile(D_MODEL),
            pl.BlockSpec((nc, tt * bsz, SSM_CH), lambda i: (0, i, 0)),
            resident((SSM_WIDTH, SSM_WIDTH)),
            resident((1, SSM_WIDTH)),
            tile(MLA_HEADS * V_DIM),
            tile(MEM_WIDTH),
            tile(3 * D_MODEL),
            resident((SSM_WIDTH, D_MODEL)),
            resident((MLA_HEADS * V_DIM, D_MODEL)),
            resident((MEM_WIDTH, D_MODEL)),
            resident((D_MODEL, D_MODEL)),
        ],
        out_specs=tile(D_MODEL),
        out_shape=jax.ShapeDtypeStruct((bsz, seqlen, D_MODEL), F32),
        scratch_shapes=[pltpu.VMEM((bsz * tt, SSM_WIDTH), F32)],
        compiler_params=_cp(("parallel",), 56),
        name="mergeout",
    )(v3(x), ya_slabs, wglu, bglu, v3(ymla), v3(ymem), v3(gates), wbs, wbm, wbe, wo)
    return out.reshape(rows, D_MODEL)


def _rope_tables(pos):
    half = QK_ROPE // 2
    inv_freq = ROPE_THETA ** (-jnp.arange(half, dtype=F32) / half)
    ang = pos.astype(F32)[:, None] * inv_freq[None, :]
    cos, sin = jnp.cos(ang), jnp.sin(ang)
    zero = jnp.zeros((pos.shape[0], LANES - QK_ROPE), F32)
    k_tabs = (jnp.concatenate([cos, cos, zero], axis=1), jnp.concatenate([-sin, sin, zero], axis=1))
    q_tabs = (jnp.concatenate([cos, cos, cos, cos], axis=1), jnp.concatenate([-sin, sin, -sin, sin], axis=1))
    return k_tabs, q_tabs


def _s5_params(a_re, a_im, log_dt, b_re, b_im, c_re, c_im, d):
    dt = jnp.exp(log_dt)[:, None]
    mag = jnp.exp(a_re * dt)
    phase = a_im * dt
    lb_re, lb_im = mag * jnp.cos(phase), mag * jnp.sin(phase)
    den = a_re * a_re + a_im * a_im
    nr, ni = lb_re - 1.0, lb_im
    z_re = (nr * a_re + ni * a_im) / den
    z_im = (ni * a_re - nr * a_im) / den
    bb_re = z_re[..., None] * b_re - z_im[..., None] * b_im
    bb_im = z_re[..., None] * b_im + z_im[..., None] * b_re
    nc = SSM_GROUPS // SSM_CG
    eye = jnp.eye(SSM_CG, dtype=F32)

    def blk_b(t):
        t = t.reshape(nc, SSM_CG, SSM_STATE, SSM_GROUP)
        return jnp.einsum("cgph,gk->cghkp", t, eye).reshape(nc, SSM_CH, SSM_CS)

    def blk_c(t):
        t = t.reshape(nc, SSM_CG, SSM_GROUP, SSM_STATE)
        return jnp.einsum("cghp,gk->cgpkh", t, eye).reshape(nc, SSM_CS, SSM_CH)

    bm = jnp.concatenate([blk_b(bb_re), blk_b(bb_im)], axis=2).astype(BF16)
    cm = jnp.concatenate([blk_c(c_re), -blk_c(c_im)], axis=1).astype(BF16)
    return (bm, cm, d.reshape(1, SSM_WIDTH),
            lb_re.reshape(1, SSM_GROUPS * SSM_STATE), lb_im.reshape(1, SSM_GROUPS * SSM_STATE))


def _layer(x, bsz, seqlen, pos0, mem_k, mem_v, lat_past, kr_past, h0re, h0im, p, final_g):
    rows = bsz * seqlen
    tm = min(512, seqlen)
    rep = max(1, min(512, rows) // seqlen)
    (cos, sin), (qcos, qsin) = jax.tree.map(lambda t: jnp.tile(t, (rep, 1)),
                                            _rope_tables(pos0 + jnp.arange(seqlen)))

    x1 = _ffn(x, p["ffn1_norm"], p["ffn1_wg"], p["ffn1_wu"], p["ffn1_wd"], tm=min(512, rows))
    u_tb, cq, lat, kr, krp, qm = _inproj(x1, p["mix_norm"], p["w_small"], p["q_norm"], p["kv_norm"],
                                         cos, sin, bsz=bsz // rep, seqlen=seqlen * rep, tm=tm * rep)
    gates = _nmm(x1, p["mix_norm"], p["w_gates"], act="sigmoid", out_dtype=BF16,
                 tm=min(1024, rows), tn=1024, name="gates")

    ya_tb, sre, sim = _s5(u_tb, p["s5_bm"], p["s5_cm"], p["s5_d"], p["s5_lre"], p["s5_lim"],
                          h0re, h0im, bsz=bsz, seqlen=seqlen, tb_len=min(256, seqlen))

    q = _qproj(cq, p["w_uq"], qcos, qsin, seqlen=seqlen * rep, tm=min(256, seqlen * rep))
    if lat_past is None:
        kt, v = _kvproj_t(lat, krp, p["w_ukt"], p["w_uv"], bsz=bsz, seqlen=seqlen, tm=tm)
        ymla = _attn(q, kt, v, bsz=bsz, seqlen=seqlen, tq=256, hg=2)
    else:
        past = lat_past.shape[1]
        n_k = past + seqlen
        lat_all = jnp.concatenate([lat_past, lat.reshape(bsz, seqlen, KV_LORA)], axis=1)
        kr_all = jnp.concatenate([kr_past, kr.reshape(bsz, seqlen, QK_ROPE)], axis=1)
        k_lat = jnp.concatenate([lat_all, kr_all, kr_all], axis=-1).astype(BF16).reshape(bsz * n_k, LAT_PAD)
        o_lat = _attn_cached(_qabsorb(q, p["w_ukt"]), k_lat, bsz=bsz, n_q=seqlen, n_k=n_k, past=past)
        ymla = _vabsorb(o_lat, p["w_uv"])

    ymem = _memattn(qm, mem_k, mem_v, bsz=bsz, seqlen=seqlen, tq=tm)

    x2 = _mergeout(x1, ya_tb, p["w_glu"], p["b_glu"], ymla, ymem, gates,
                   p["w_br_ssm"], p["w_br_mla"], p["w_br_mem"], p["w_out"],
                   bsz=bsz, seqlen=seqlen, tt=32)
    y = _ffn(x2, p["ffn2_norm"], p["ffn2_wg"], p["ffn2_wu"], p["ffn2_wd"], final_g, tm=min(512, rows))
    return y, lat, kr, sre, sim


def kernel(x_prompt, x_sample, cache_kv_latent, cache_k_rope, cache_mem_k, cache_mem_v, state_ssm_re, state_ssm_im, mem_prompt, ffn1_norm, ffn1_w_gate, ffn1_w_up, ffn1_w_down, mix_norm, w_in, q_norm, w_uq, kv_norm, w_uk, w_uv, ssm_a_re, ssm_a_im, ssm_log_dt, ssm_b_re, ssm_b_im, ssm_c_re, ssm_c_im, ssm_d, ssm_w_glu, ssm_b_glu, mem_norm, w_mem_k, w_mem_v, w_br_ssm, w_br_mla, w_br_mem, w_out, ffn2_norm, ffn2_w_gate, ffn2_w_up, ffn2_w_down, final_norm):
    bp, lp, _ = x_prompt.shape
    bs, ls, _ = x_sample.shape
    past = cache_kv_latent.shape[2]
    l = 0
    bf = lambda t: t.astype(BF16)

    wi = bf(w_in[l])
    c0 = SSM_WIDTH + Q_LORA + KV_LORA
    w_kr = wi[:, c0:c0 + QK_ROPE]
    half = QK_ROPE // 2
    zpad = jnp.zeros((D_MODEL, LANES - QK_ROPE), BF16)
    w_small = jnp.concatenate([
        wi[:, :c0], w_kr, zpad, w_kr[:, half:], w_kr[:, :half], zpad,
        wi[:, c0 + QK_ROPE:c0 + QK_ROPE + MEM_WIDTH]], axis=1)
    w_gates = wi[:, c0 + QK_ROPE + MEM_WIDTH:]

    wq = w_uq[l].reshape(Q_LORA, MLA_HEADS, QK_NOPE + QK_ROPE)
    wq_r = wq[:, :, QK_NOPE:]
    hw = MLA_HEADS * LANES
    w_uq3 = jnp.concatenate([
        wq[:, :, :QK_NOPE].reshape(Q_LORA, hw),
        wq_r.reshape(Q_LORA, hw // 2),
        jnp.concatenate([wq_r[..., half:], wq_r[..., :half]], axis=-1).reshape(Q_LORA, hw // 2)], axis=1)

    bm, cm, d, lre, lim = _s5_params(ssm_a_re[l], ssm_a_im[l], ssm_log_dt[l], ssm_b_re[l], ssm_b_im[l],
                                     ssm_c_re[l], ssm_c_im[l], ssm_d[l])
    p = {
        "ffn1_norm": ffn1_norm[l][None], "ffn1_wg": bf(ffn1_w_gate[l]), "ffn1_wu": bf(ffn1_w_up[l]),
        "ffn1_wd": bf(ffn1_w_down[l]),
        "mix_norm": mix_norm[l][None], "w_small": bf(w_small), "w_gates": bf(w_gates),
        "q_norm": q_norm[l][None], "kv_norm": kv_norm[l][None],
        "w_uq": bf(w_uq3), "w_ukt": bf(w_uk[l].T), "w_uv": bf(w_uv[l]),
        "s5_bm": bm, "s5_cm": cm, "s5_d": d, "s5_lre": lre, "s5_lim": lim,
        "w_glu": bf(ssm_w_glu[l]), "b_glu": ssm_b_glu[l][None],
        "w_br_ssm": bf(w_br_ssm[l]), "w_br_mla": bf(w_br_mla[l]), "w_br_mem": bf(w_br_mem[l]),
        "w_out": bf(w_out[l]),
        "ffn2_norm": ffn2_norm[l][None], "ffn2_wg": bf(ffn2_w_gate[l]), "ffn2_wu": bf(ffn2_w_up[l]),
        "ffn2_wd": bf(ffn2_w_down[l]),
    }
    fg = final_norm[None]

    w_mem = bf(jnp.concatenate([w_mem_k[l], w_mem_v[l]], axis=1))
    mkv = _nmm(mem_prompt.reshape(bp * MEM_TOKENS, D_MODEL), mem_norm[l][None], w_mem,
               act=None, out_dtype=F32, tm=512, tn=2 * MEM_WIDTH, name="memkv")
    mk_p = mkv[:, :MEM_WIDTH].reshape(bp, MEM_TOKENS, MEM_WIDTH)
    mv_p = mkv[:, MEM_WIDTH:].reshape(bp, MEM_TOKENS, MEM_WIDTH)

    n_state = SSM_GROUPS * SSM_STATE
    zero_state = jnp.zeros((bp, n_state), F32)
    yp, lat_p, kr_p, sre_p, sim_p = _layer(
        x_prompt.reshape(bp * lp, D_MODEL), bp, lp, 0, mk_p, mv_p, None, None,
        zero_state, zero_state, p, fg)

    ys, lat_s, kr_s, sre_s, sim_s = _layer(
        x_sample.reshape(bs * ls, D_MODEL), bs, ls, past,
        cache_mem_k[l].reshape(bs, MEM_TOKENS, MEM_WIDTH), cache_mem_v[l].reshape(bs, MEM_TOKENS, MEM_WIDTH),
        cache_kv_latent[l], cache_k_rope[l],
        state_ssm_re[l].reshape(bs, n_state), state_ssm_im[l].reshape(bs, n_state), p, fg)

    st = lambda t, b: t.reshape(1, b, SSM_GROUPS, SSM_STATE)
    return (yp.reshape(bp, lp, D_MODEL), ys.reshape(bs, ls, D_MODEL),
            lat_p.reshape(1, bp, lp, KV_LORA), kr_p.reshape(1, bp, lp, QK_ROPE),
            mk_p.reshape(1, bp, MEM_TOKENS, MEM_HEADS, MEM_HEAD_DIM),
            mv_p.reshape(1, bp, MEM_TOKENS, MEM_HEADS, MEM_HEAD_DIM),
            st(sre_p, bp), st(sim_p, bp),
            lat_s.reshape(1, bs, ls, KV_LORA), kr_s.reshape(1, bs, ls, QK_ROPE),
            st(sre_s, bs), st(sim_s, bs))
```

```python
import functools
import math

import jax
import jax.numpy as jnp
from jax import lax
from jax.experimental import pallas as pl
from jax.experimental.pallas import tpu as pltpu

F32 = jnp.float32
BF16 = jnp.bfloat16

D_MODEL = 2048
D_FF = 5632
CHUNK = 64
SSM_WIDTH = D_MODEL // 2
SSM_GROUP = 16
SSM_GROUPS = SSM_WIDTH // SSM_GROUP
SSM_STATE = 64
MLA_HEADS = 16
QK_NOPE = 128
QK_ROPE = 64
V_DIM = 128
Q_LORA = 768
KV_LORA = 512
ROPE_THETA = 10000.0
MEM_TOKENS = 256
MEM_HEADS = 4
MEM_HEAD_DIM = 128
MEM_WIDTH = MEM_HEADS * MEM_HEAD_DIM
RMS_EPS = 1e-6
NEG_INF = -1e30
MLA_SCALE = (QK_NOPE + QK_ROPE) ** -0.5
MEM_SCALE = MEM_HEAD_DIM ** -0.5

LANES = 128
HEAD_PAD = 2 * LANES
SSM_CH = 128
SSM_CG = SSM_CH // SSM_GROUP
SSM_CS = SSM_CG * SSM_STATE
MIB = 1024 * 1024


def _cp(sem, vmem_mib):
    return pltpu.CompilerParams(dimension_semantics=sem, vmem_limit_bytes=int(vmem_mib * MIB))


def _dot(a, b):
    return jnp.dot(a, b, preferred_element_type=F32)


def _rms(x, g):
    return x * lax.rsqrt(jnp.mean(x * x, axis=-1, keepdims=True) + RMS_EPS) * g


def _sigmoid(x):
    return 1.0 / (1.0 + jnp.exp(-x))


def _gelu_tanh(x):
    cdf = 0.5 * (1.0 + jnp.tanh(math.sqrt(2.0 / math.pi) * (x + 0.044715 * (x * x * x))))
    return x * cdf


def _ffn_body(x_ref, g_ref, wg_ref, wu_ref, wd_ref, *rest, final):
    if final:
        fg_ref, o_ref, h_scr, acc_scr = rest
    else:
        o_ref, h_scr, acc_scr = rest
    j = pl.program_id(1)
    last = pl.num_programs(1) - 1

    def contribution(h):
        a = _dot(h, wg_ref[...])
        b = _dot(h, wu_ref[...])
        act = ((a * _sigmoid(a)) * b).astype(BF16)
        return _dot(act, wd_ref[...])

    @pl.when(j == 0)
    def _():
        h = _rms(x_ref[...], g_ref[...]).astype(BF16)
        h_scr[...] = h
        acc_scr[...] = contribution(h)

    @pl.when(jnp.logical_and(j > 0, j < last))
    def _():
        acc_scr[...] += contribution(h_scr[...])

    @pl.when(j == last)
    def _():
        y = x_ref[...] + 0.5 * (acc_scr[...] + contribution(h_scr[...]))
        if final:
            y = _rms(y, fg_ref[...])
        o_ref[...] = y


def _ffn(x, g, wg, wu, wd, final_g=None, *, tm=512, tf=512):
    rows = x.shape[0]
    final = final_g is not None
    in_specs = [
        pl.BlockSpec((tm, D_MODEL), lambda i, j: (i, 0)),
        pl.BlockSpec((1, D_MODEL), lambda i, j: (0, 0)),
        pl.BlockSpec((D_MODEL, tf), lambda i, j: (0, j)),
        pl.BlockSpec((D_MODEL, tf), lambda i, j: (0, j)),
        pl.BlockSpec((tf, D_MODEL), lambda i, j: (j, 0)),
    ]
    args = [x, g, wg, wu, wd]
    if final:
        in_specs.append(pl.BlockSpec((1, D_MODEL), lambda i, j: (0, 0)))
        args.append(final_g)
    return pl.pallas_call(
        functools.partial(_ffn_body, final=final),
        grid=(rows // tm, D_FF // tf),
        in_specs=in_specs,
        out_specs=pl.BlockSpec((tm, D_MODEL), lambda i, j: (i, 0)),
        out_shape=jax.ShapeDtypeStruct((rows, D_MODEL), F32),
        scratch_shapes=[pltpu.VMEM((tm, D_MODEL), BF16), pltpu.VMEM((tm, D_MODEL), F32)],
        compiler_params=_cp(("parallel", "arbitrary"), 48),
        name="ffn",
    )(*args)


def _nmm_body(x_ref, g_ref, w_ref, o_ref, h_scr, *, act):
    def tile(h):
        z = _dot(h, w_ref[...])
        if act == "sigmoid":
            z = _sigmoid(z)
        o_ref[...] = z.astype(o_ref.dtype)

    @pl.when(pl.program_id(1) == 0)
    def _():
        h = _rms(x_ref[...], g_ref[...]).astype(BF16)
        h_scr[...] = h
        tile(h)

    @pl.when(pl.program_id(1) > 0)
    def _():
        tile(h_scr[...])


def _nmm(x, g, w, *, act, out_dtype, tm, tn, name):
    rows, k = x.shape
    n = w.shape[1]
    return pl.pallas_call(
        functools.partial(_nmm_body, act=act),
        grid=(rows // tm, n // tn),
        in_specs=[
            pl.BlockSpec((tm, k), lambda i, j: (i, 0)),
            pl.BlockSpec((1, k), lambda i, j: (0, 0)),
            pl.BlockSpec((k, tn), lambda i, j: (0, j)),
        ],
        out_specs=pl.BlockSpec((tm, tn), lambda i, j: (i, j)),
        out_shape=jax.ShapeDtypeStruct((rows, n), out_dtype),
        scratch_shapes=[pltpu.VMEM((tm, k), BF16)],
        compiler_params=_cp(("parallel", "arbitrary"), 40),
        name=name,
    )(x, g, w)


_C_U = SSM_WIDTH
_C_Q = _C_U + Q_LORA
_C_KV = _C_Q + KV_LORA
_C_KX = _C_KV + LANES
_C_KS = _C_KX + LANES
_C_QM = _C_KS + MEM_WIDTH


def _inproj_body(x_ref, g_ref, w_ref, qg_ref, kvg_ref, cos_ref, sin_ref,
                 u_ref, cq_ref, lat_ref, kr_ref, krp_ref, qm_ref):
    h = _rms(x_ref[...], g_ref[...]).astype(BF16)
    u_ref[...] = _dot(h, w_ref[:, 0:_C_U])
    cq_ref[...] = _rms(_dot(h, w_ref[:, _C_U:_C_Q]), qg_ref[...]).astype(BF16)
    lat_ref[...] = _rms(_dot(h, w_ref[:, _C_Q:_C_KV]), kvg_ref[...])
    kx = _dot(h, w_ref[:, _C_KV:_C_KX])
    ks = _dot(h, w_ref[:, _C_KX:_C_KS])
    r = kx * cos_ref[...] + ks * sin_ref[...]
    krp_ref[...] = r
    kr_ref[...] = r[:, :QK_ROPE]
    qm_ref[...] = _dot(h, w_ref[:, _C_KS:_C_QM]).astype(BF16)


def _inproj(x, g, w, qg, kvg, cos, sin, *, bsz, seqlen, tm):
    rows = bsz * seqlen
    nt = seqlen // tm
    full = lambda i: (0, 0)
    row = lambda i: (i, 0)
    return pl.pallas_call(
        _inproj_body,
        grid=(rows // tm,),
        in_specs=[
            pl.BlockSpec((tm, D_MODEL), row),
            pl.BlockSpec((1, D_MODEL), full),
            pl.BlockSpec((D_MODEL, _C_QM), full),
            pl.BlockSpec((1, Q_LORA), full),
            pl.BlockSpec((1, KV_LORA), full),
            pl.BlockSpec((tm, LANES), lambda i: (i % nt, 0)),
            pl.BlockSpec((tm, LANES), lambda i: (i % nt, 0)),
        ],
        out_specs=[
            pl.BlockSpec((tm, SSM_WIDTH), row),
            pl.BlockSpec((tm, Q_LORA), row),
            pl.BlockSpec((tm, KV_LORA), row),
            pl.BlockSpec((tm, QK_ROPE), row),
            pl.BlockSpec((tm, LANES), row),
            pl.BlockSpec((tm, MEM_WIDTH), row),
        ],
        out_shape=[
            jax.ShapeDtypeStruct((rows, SSM_WIDTH), F32),
            jax.ShapeDtypeStruct((rows, Q_LORA), BF16),
            jax.ShapeDtypeStruct((rows, KV_LORA), F32),
            jax.ShapeDtypeStruct((rows, QK_ROPE), F32),
            jax.ShapeDtypeStruct((rows, LANES), F32),
            jax.ShapeDtypeStruct((rows, MEM_WIDTH), BF16),
        ],
        compiler_params=_cp(("parallel",), 56),
        name="inproj",
    )(x, g, w, qg, kvg, cos, sin)


def _qproj_body(cq_ref, w_ref, cos_ref, sin_ref, q_ref):
    cq = cq_ref[...]
    hw = MLA_HEADS * LANES
    nope = _dot(cq, w_ref[:, 0:hw])
    for h in range(MLA_HEADS):
        q_ref[:, h * HEAD_PAD:h * HEAD_PAD + LANES] = nope[:, h * LANES:(h + 1) * LANES].astype(BF16)
    pw = hw // 2
    rx = _dot(cq, w_ref[:, hw:hw + pw])
    rs = _dot(cq, w_ref[:, hw + pw:hw + 2 * pw])
    c = cos_ref[...]
    s = sin_ref[...]
    for j in range(MLA_HEADS // 2):
        sl = slice(j * LANES, (j + 1) * LANES)
        r = (rx[:, sl] * c + rs[:, sl] * s).astype(BF16)
        for h in (2 * j, 2 * j + 1):
            q_ref[:, h * HEAD_PAD + LANES:(h + 1) * HEAD_PAD] = r


def _qproj(cq, w, cos, sin, *, seqlen, tm):
    rows = cq.shape[0]
    nt = seqlen // tm
    return pl.pallas_call(
        _qproj_body,
        grid=(rows // tm,),
        in_specs=[
            pl.BlockSpec((tm, Q_LORA), lambda i: (i, 0)),
            pl.BlockSpec((Q_LORA, 2 * MLA_HEADS * LANES), lambda i: (0, 0)),
            pl.BlockSpec((tm, LANES), lambda i: (i % nt, 0)),
            pl.BlockSpec((tm, LANES), lambda i: (i % nt, 0)),
        ],
        out_specs=pl.BlockSpec((tm, MLA_HEADS * HEAD_PAD), lambda i: (i, 0)),
        out_shape=jax.ShapeDtypeStruct((rows, MLA_HEADS * HEAD_PAD), BF16),
        compiler_params=_cp(("parallel",), 48),
        name="qproj",
    )(cq, w, cos, sin)


def _kvproj_t_body(lat_ref, krp_ref, wukt_ref, wuv_ref, kt_ref, v_ref):
    lat = lat_ref[...]
    v_ref[...] = _dot(lat.astype(BF16), wuv_ref[...]).astype(BF16)
    kt = _dot(wukt_ref[...], lat.T.astype(BF16))
    krt = krp_ref[...].T.astype(BF16)
    krt_odd = jnp.concatenate([krt[QK_ROPE:], krt[:QK_ROPE]], axis=0)
    for h in range(MLA_HEADS):
        kt_ref[h * HEAD_PAD:h * HEAD_PAD + LANES, :] = kt[h * LANES:(h + 1) * LANES, :].astype(BF16)
        kt_ref[h * HEAD_PAD + LANES:(h + 1) * HEAD_PAD, :] = krt_odd if h % 2 else krt


def _kvproj_t(lat, krp, wukt, wuv, *, bsz, seqlen, tm):
    rows = bsz * seqlen
    nt = seqlen // tm
    return pl.pallas_call(
        _kvproj_t_body,
        grid=(rows // tm,),
        in_specs=[
            pl.BlockSpec((tm, KV_LORA), lambda i: (i, 0)),
            pl.BlockSpec((tm, LANES), lambda i: (i, 0)),
            pl.BlockSpec((MLA_HEADS * QK_NOPE, KV_LORA), lambda i: (0, 0)),
            pl.BlockSpec((KV_LORA, MLA_HEADS * V_DIM), lambda i: (0, 0)),
        ],
        out_specs=[
            pl.BlockSpec((None, MLA_HEADS * HEAD_PAD, tm), lambda i: (i // nt, 0, i % nt)),
            pl.BlockSpec((tm, MLA_HEADS * V_DIM), lambda i: (i, 0)),
        ],
        out_shape=[
            jax.ShapeDtypeStruct((bsz, MLA_HEADS * HEAD_PAD, seqlen), BF16),
            jax.ShapeDtypeStruct((rows, MLA_HEADS * V_DIM), BF16),
        ],
        compiler_params=_cp(("parallel",), 48),
        name="kvproj_t",
    )(lat, krp, wukt, wuv)


LAT_PAD = KV_LORA + LANES


def _qabsorb_body(q_ref, wukt_ref, o_ref):
    ql = _dot(q_ref[:, 0:LANES], wukt_ref[...])
    o_ref[:, 0:KV_LORA] = ql.astype(BF16)
    rope = q_ref[:, LANES:HEAD_PAD]
    lane = lax.broadcasted_iota(jnp.int32, rope.shape, 1)
    own = (lane < QK_ROPE) == (pl.program_id(0) % 2 == 0)
    o_ref[:, KV_LORA:LAT_PAD] = jnp.where(own, rope, jnp.zeros_like(rope))


def _qabsorb(q, wukt):
    rows = q.shape[0]
    return pl.pallas_call(
        _qabsorb_body,
        grid=(MLA_HEADS,),
        in_specs=[
            pl.BlockSpec((rows, HEAD_PAD), lambda h: (0, h)),
            pl.BlockSpec((QK_NOPE, KV_LORA), lambda h: (h, 0)),
        ],
        out_specs=pl.BlockSpec((None, rows, LAT_PAD), lambda h: (h, 0, 0)),
        out_shape=jax.ShapeDtypeStruct((MLA_HEADS, rows, LAT_PAD), BF16),
        compiler_params=_cp(("parallel",), 24),
        name="qabsorb",
    )(q, wukt)


def _vabsorb_body(o_ref, wuv_ref, y_ref):
    y_ref[...] = _dot(o_ref[...], wuv_ref[...]).astype(BF16)


def _vabsorb(o_lat, wuv):
    rows = o_lat.shape[1]
    return pl.pallas_call(
        _vabsorb_body,
        grid=(MLA_HEADS,),
        in_specs=[
            pl.BlockSpec((None, rows, KV_LORA), lambda h: (h, 0, 0)),
            pl.BlockSpec((KV_LORA, V_DIM), lambda h: (0, h)),
        ],
        out_specs=pl.BlockSpec((rows, V_DIM), lambda h: (0, h)),
        out_shape=jax.ShapeDtypeStruct((rows, MLA_HEADS * V_DIM), BF16),
        compiler_params=_cp(("parallel",), 24),
        name="vabsorb",
    )(o_lat, wuv)


def _chunk_mask(tq, tk, q0, k0):
    r = lax.broadcasted_iota(jnp.int32, (tq, tk), 0) + q0
    c = lax.broadcasted_iota(jnp.int32, (tq, tk), 1) + k0
    return (c // CHUNK) <= (r // CHUNK)


_MLA_EXP2_SCALE = MLA_SCALE * math.log2(math.e)


def _attn_body(q_ref, kt_ref, v_ref, o_ref, *, tq, hg, nq):
    visible = _chunk_mask(tq, tq, 0, 0)
    for vq in range(nq):
        n_past = vq * tq
        rows = slice(n_past, n_past + tq)
        for h in range(hg):
            q = q_ref[rows, h * HEAD_PAD:(h + 1) * HEAD_PAD]
            hk = slice(h * HEAD_PAD, (h + 1) * HEAD_PAD)
            hv = slice(h * V_DIM, (h + 1) * V_DIM)
            m = l = acc = None
            for k0 in [n_past] + list(range(0, n_past, tq)):
                s = _dot(q, kt_ref[hk, k0:k0 + tq]) * _MLA_EXP2_SCALE
                if k0 == n_past:
                    s = jnp.where(visible, s, NEG_INF)
                m_strip = jnp.max(s, axis=-1, keepdims=True)
                m_new = m_strip if m is None else jnp.maximum(m, m_strip)
                p = jnp.exp2(s - m_new)
                l_strip = p[:, 0:LANES]
                for c0 in range(LANES, tq, LANES):
                    l_strip = l_strip + p[:, c0:c0 + LANES]
                pv = _dot(p.astype(BF16), v_ref[k0:k0 + tq, hv])
                if m is None:
                    l, acc = l_strip, pv
                else:
                    alpha = jnp.exp2(m - m_new)
                    l = alpha * l + l_strip
                    acc = alpha * acc + pv
                m = m_new
            o_ref[rows, hv] = (acc / jnp.sum(l, axis=-1, keepdims=True)).astype(BF16)


def _attn(q, kt, v, *, bsz, seqlen, tq, hg):
    nq = seqlen // tq
    return pl.pallas_call(
        functools.partial(_attn_body, tq=tq, hg=hg, nq=nq),
        grid=(bsz, MLA_HEADS // hg),
        in_specs=[
            pl.BlockSpec((seqlen, hg * HEAD_PAD), lambda b, g: (b, g)),
            pl.BlockSpec((None, hg * HEAD_PAD, seqlen), lambda b, g: (b, g, 0)),
            pl.BlockSpec((seqlen, hg * V_DIM), lambda b, g: (b, g)),
        ],
        out_specs=pl.BlockSpec((seqlen, hg * V_DIM), lambda b, g: (b, g)),
        out_shape=jax.ShapeDtypeStruct((bsz * seqlen, MLA_HEADS * V_DIM), BF16),
        compiler_params=_cp(("parallel", "parallel"), 48),
        name="attn",
    )(q, kt, v)


def _attn_cached_body(q_ref, k_ref, o_ref, *, n_q, n_k, past):
    q = q_ref[...].reshape(MLA_HEADS * n_q, LAT_PAD)
    k = k_ref[...]
    s = lax.dot_general(q, k, (((1,), (1,)), ((), ())), preferred_element_type=F32) * _MLA_EXP2_SCALE
    s = s.reshape(MLA_HEADS, n_q, n_k)
    s = jnp.where(_chunk_mask(n_q, n_k, past, 0)[None], s, NEG_INF)
    m = jnp.max(s, axis=-1, keepdims=True)
    p = jnp.exp2(s - m)
    l = jnp.sum(p, axis=-1, keepdims=True)
    acc = _dot(p.reshape(MLA_HEADS * n_q, n_k).astype(BF16), k[:, 0:KV_LORA])
    o_ref[...] = (acc.reshape(MLA_HEADS, n_q, KV_LORA) / l).astype(BF16)


def _attn_cached(q_lat, k_lat, *, bsz, n_q, n_k, past):
    return pl.pallas_call(
        functools.partial(_attn_cached_body, n_q=n_q, n_k=n_k, past=past),
        grid=(bsz,),
        in_specs=[
            pl.BlockSpec((MLA_HEADS, n_q, LAT_PAD), lambda b: (0, b, 0)),
            pl.BlockSpec((n_k, LAT_PAD), lambda b: (b, 0)),
        ],
        out_specs=pl.BlockSpec((MLA_HEADS, n_q, KV_LORA), lambda b: (0, b, 0)),
        out_shape=jax.ShapeDtypeStruct((MLA_HEADS, bsz * n_q, KV_LORA), BF16),
        compiler_params=_cp(("parallel",), 48),
        name="attn_cached",
    )(q_lat, k_lat)


def _memattn_body(q_ref, k_ref, v_ref, o_ref):
    kt = k_ref[...].T.astype(BF16)
    vb = v_ref[...].astype(BF16)
    for h in range(MEM_HEADS):
        sl = slice(h * MEM_HEAD_DIM, (h + 1) * MEM_HEAD_DIM)
        s = _dot(q_ref[:, sl], kt[sl, :]) * MEM_SCALE
        m = jnp.max(s, axis=-1, keepdims=True)
        p = jnp.exp(s - m)
        l = jnp.sum(p, axis=-1, keepdims=True)
        o_ref[:, sl] = (_dot(p.astype(BF16), vb[:, sl]) / l).astype(BF16)


def _memattn(q, k, v, *, bsz, seqlen, tq):
    nt = seqlen // tq
    return pl.pallas_call(
        _memattn_body,
        grid=(bsz, nt),
        in_specs=[
            pl.BlockSpec((tq, MEM_WIDTH), lambda b, i: (b * nt + i, 0)),
            pl.BlockSpec((None, MEM_TOKENS, MEM_WIDTH), lambda b, i: (b, 0, 0)),
            pl.BlockSpec((None, MEM_TOKENS, MEM_WIDTH), lambda b, i: (b, 0, 0)),
        ],
        out_specs=pl.BlockSpec((tq, MEM_WIDTH), lambda b, i: (b * nt + i, 0)),
        out_shape=jax.ShapeDtypeStruct((bsz * seqlen, MEM_WIDTH), BF16),
        compiler_params=_cp(("parallel", "parallel"), 24),
        name="memattn",
    )(q, k, v)


S5_PAIR = 2


def _s5_body(*refs, tb_len, bsz):
    u_refs = refs[:bsz]
    (bm_ref, cm_ref, d_ref, lre_ref, lim_ref, h0re_ref, h0im_ref,
     y_ref, sre_ref, sim_ref, u_scr, bu_scr, xb_scr, st_scr) = refs[bsz:]
    tb = pl.program_id(1)

    @pl.when(tb == 0)
    def _():
        st_scr[0] = h0re_ref[...]
        st_scr[1] = h0im_ref[...]

    for k in range(S5_PAIR):
        for b in range(bsz):
            u_scr[k, pl.ds(b, tb_len, stride=bsz), :] = u_refs[b][:, k * SSM_CH:(k + 1) * SSM_CH]
    for k in range(S5_PAIR):
        bu_scr[k] = _dot(u_scr[k].astype(BF16), bm_ref[k])

    for k in range(S5_PAIR):
        cs = slice(k * SSM_CS, (k + 1) * SSM_CS)
        lre = jnp.broadcast_to(lre_ref[:, cs], (bsz, SSM_CS))
        lim = jnp.broadcast_to(lim_ref[:, cs], (bsz, SSM_CS))
        re = st_scr[0, :, cs]
        im = st_scr[1, :, cs]
        for t in range(0, tb_len, 2):
            pair = []
            for r0 in (t * bsz, (t + 1) * bsz):
                nre = (lre * re - lim * im) + bu_scr[k, r0:r0 + bsz, 0:SSM_CS]
                nim = (lre * im + lim * re) + bu_scr[k, r0:r0 + bsz, SSM_CS:2 * SSM_CS]
                re, im = nre, nim
                pair.append((nre, nim))
            rows = slice(t * bsz, (t + 2) * bsz)
            xb_scr[k, rows, 0:SSM_CS] = jnp.concatenate([pair[0][0], pair[1][0]], axis=0).astype(BF16)
            xb_scr[k, rows, SSM_CS:2 * SSM_CS] = jnp.concatenate([pair[0][1], pair[1][1]], axis=0).astype(BF16)
        st_scr[0, :, cs] = re
        st_scr[1, :, cs] = im
        y = _dot(xb_scr[k], cm_ref[k]) + d_ref[:, k * SSM_CH:(k + 1) * SSM_CH] * u_scr[k]
        y_ref[k] = _gelu_tanh(y)

    @pl.when(tb == pl.num_programs(1) - 1)
    def _():
        sre_ref[...] = st_scr[0]
        sim_ref[...] = st_scr[1]


def _s5(u, bm, cm, d, lre, lim, h0re, h0im, *, bsz, seqlen, tb_len):
    nc = SSM_WIDTH // SSM_CH
    nt = seqlen // tb_len
    pch, pcs = S5_PAIR * SSM_CH, S5_PAIR * SSM_CS
    u_specs = [pl.BlockSpec((tb_len, pch), lambda g, t, b=b: (b * nt + t, g)) for b in range(bsz)]
    return pl.pallas_call(
        functools.partial(_s5_body, tb_len=tb_len, bsz=bsz),
        grid=(nc // S5_PAIR, nt),
        in_specs=u_specs + [
            pl.BlockSpec((S5_PAIR, SSM_CH, 2 * SSM_CS), lambda g, t: (g, 0, 0)),
            pl.BlockSpec((S5_PAIR, 2 * SSM_CS, SSM_CH), lambda g, t: (g, 0, 0)),
            pl.BlockSpec((1, pch), lambda g, t: (0, g)),
            pl.BlockSpec((1, pcs), lambda g, t: (0, g)),
            pl.BlockSpec((1, pcs), lambda g, t: (0, g)),
            pl.BlockSpec((bsz, pcs), lambda g, t: (0, g)),
            pl.BlockSpec((bsz, pcs), lambda g, t: (0, g)),
        ],
        out_specs=[
            pl.BlockSpec((S5_PAIR, tb_len * bsz, SSM_CH), lambda g, t: (g, t, 0)),
            pl.BlockSpec((bsz, pcs), lambda g, t: (0, g)),
            pl.BlockSpec((bsz, pcs), lambda g, t: (0, g)),
        ],
        out_shape=[
            jax.ShapeDtypeStruct((nc, seqlen * bsz, SSM_CH), F32),
            jax.ShapeDtypeStruct((bsz, SSM_GROUPS * SSM_STATE), F32),
            jax.ShapeDtypeStruct((bsz, SSM_GROUPS * SSM_STATE), F32),
        ],
        scratch_shapes=[
            pltpu.VMEM((S5_PAIR, tb_len * bsz, SSM_CH), F32),
            pltpu.VMEM((S5_PAIR, tb_len * bsz, 2 * SSM_CS), F32),
            pltpu.VMEM((S5_PAIR, tb_len * bsz, 2 * SSM_CS), BF16),
            pltpu.VMEM((2, bsz, pcs), F32),
        ],
        compiler_params=_cp(("parallel", "arbitrary"), 48),
        name="s5",
    )(*([u] * bsz), bm, cm, d, lre, lim, h0re, h0im)


def _mergeout_body(x_ref, ya_ref, wglu_ref, bglu_ref, ymla_ref, ymem_ref, g_ref,
                   wbs_ref, wbm_ref, wbe_ref, wo_ref, o_ref, ya_scr, *, bsz, tt):
    rows = bsz * tt
    for c in range(SSM_WIDTH // SSM_CH):
        for b in range(bsz):
            ya_scr[b * tt:(b + 1) * tt, c * SSM_CH:(c + 1) * SSM_CH] = ya_ref[c, pl.ds(b, tt, stride=bsz), :]
    y = ya_scr[...]
    gate = _sigmoid(_dot(y.astype(BF16), wglu_ref[...]) + bglu_ref[...])
    yg = (y * gate).astype(BF16)
    g = g_ref[...].reshape(rows, 3 * D_MODEL)
    ymla = ymla_ref[...].reshape(rows, MLA_HEADS * V_DIM)
    ymem = ymem_ref[...].reshape(rows, MEM_WIDTH)
    m = (g[:, 0:D_MODEL].astype(F32) * _dot(yg, wbs_ref[...])
         + g[:, D_MODEL:2 * D_MODEL].astype(F32) * _dot(ymla, wbm_ref[...])
         + g[:, 2 * D_MODEL:3 * D_MODEL].astype(F32) * _dot(ymem, wbe_ref[...]))
    out = x_ref[...].reshape(rows, D_MODEL) + _dot(m.astype(BF16), wo_ref[...])
    o_ref[...] = out.reshape(bsz, tt, D_MODEL)


def _mergeout(x, ya_slabs, wglu, bglu, ymla, ymem, gates, wbs, wbm, wbe, wo, *, bsz, seqlen, tt):
    rows = bsz * seqlen
    nc = SSM_WIDTH // SSM_CH
    tile = lambda width: pl.BlockSpec((bsz, tt, width), lambda i: (0, i, 0))
    resident = lambda shape: pl.BlockSpec(shape, lambda i: (0, 0), pipeline_mode=pl.Buffered(1))
    v3 = lambda t: t.reshape(bsz, seqlen, t.shape[-1])
    out = pl.pallas_call(
        functools.partial(_mergeout_body, bsz=bsz, tt=tt),
        grid=(seqlen // tt,),
        in_specs=[
            tile(D_MODEL),
            pl.BlockSpec((nc, tt * bsz, SSM_CH), lambda i: (0, i, 0)),
            resident((SSM_WIDTH, SSM_WIDTH)),
            resident((1, SSM_WIDTH)),
            tile(MLA_HEADS * V_DIM),
            tile(MEM_WIDTH),
            tile(3 * D_MODEL),
            resident((SSM_WIDTH, D_MODEL)),
            resident((MLA_HEADS * V_DIM, D_MODEL)),
            resident((MEM_WIDTH, D_MODEL)),
            resident((D_MODEL, D_MODEL)),
        ],
        out_specs=tile(D_MODEL),
        out_shape=jax.ShapeDtypeStruct((bsz, seqlen, D_MODEL), F32),
        scratch_shapes=[pltpu.VMEM((bsz * tt, SSM_WIDTH), F32)],
        compiler_params=_cp(("parallel",), 56),
        name="mergeout",
    )(v3(x), ya_slabs, wglu, bglu, v3(ymla), v3(ymem), v3(gates), wbs, wbm, wbe, wo)
    return out.reshape(rows, D_MODEL)


def _rope_tables(pos):
    half = QK_ROPE // 2
    inv_freq = ROPE_THETA ** (-jnp.arange(half, dtype=F32) / half)
    ang = pos.astype(F32)[:, None] * inv_freq[None, :]
    cos, sin = jnp.cos(ang), jnp.sin(ang)
    zero = jnp.zeros((pos.shape[0], LANES - QK_ROPE), F32)
    k_tabs = (jnp.concatenate([cos, cos, zero], axis=1), jnp.concatenate([-sin, sin, zero], axis=1))
    q_tabs = (jnp.concatenate([cos, cos, cos, cos], axis=1), jnp.concatenate([-sin, sin, -sin, sin], axis=1))
    return k_tabs, q_tabs


def _s5_params(a_re, a_im, log_dt, b_re, b_im, c_re, c_im, d):
    dt = jnp.exp(log_dt)[:, None]
    mag = jnp.exp(a_re * dt)
    phase = a_im * dt
    lb_re, lb_im = mag * jnp.cos(phase), mag * jnp.sin(phase)
    den = a_re * a_re + a_im * a_im
    nr, ni = lb_re - 1.0, lb_im
    z_re = (nr * a_re + ni * a_im) / den
    z_im = (ni * a_re - nr * a_im) / den
    bb_re = z_re[..., None] * b_re - z_im[..., None] * b_im
    bb_im = z_re[..., None] * b_im + z_im[..., None] * b_re
    nc = SSM_GROUPS // SSM_CG
    eye = jnp.eye(SSM_CG, dtype=F32)

    def blk_b(t):
        t = t.reshape(nc, SSM_CG, SSM_STATE, SSM_GROUP)
        return jnp.einsum("cgph,gk->cghkp", t, eye).reshape(nc, SSM_CH, SSM_CS)

    def blk_c(t):
        t = t.reshape(nc, SSM_CG, SSM_GROUP, SSM_STATE)
        return jnp.einsum("cghp,gk->cgpkh", t, eye).reshape(nc, SSM_CS, SSM_CH)

    bm = jnp.concatenate([blk_b(bb_re), blk_b(bb_im)], axis=2).astype(BF16)
    cm = jnp.concatenate([blk_c(c_re), -blk_c(c_im)], axis=1).astype(BF16)
    return (bm, cm, d.reshape(1, SSM_WIDTH),
            lb_re.reshape(1, SSM_GROUPS * SSM_STATE), lb_im.reshape(1, SSM_GROUPS * SSM_STATE))


def _layer(x, bsz, seqlen, pos0, mem_k, mem_v, lat_past, kr_past, h0re, h0im, p, final_g):
    rows = bsz * seqlen
    tm = min(512, seqlen)
    rep = max(1, min(512, rows) // seqlen)
    (cos, sin), (qcos, qsin) = jax.tree.map(lambda t: jnp.tile(t, (rep, 1)),
                                            _rope_tables(pos0 + jnp.arange(seqlen)))

    x1 = _ffn(x, p["ffn1_norm"], p["ffn1_wg"], p["ffn1_wu"], p["ffn1_wd"], tm=min(512, rows))
    u_tb, cq, lat, kr, krp, qm = _inproj(x1, p["mix_norm"], p["w_small"], p["q_norm"], p["kv_norm"],
                                         cos, sin, bsz=bsz // rep, seqlen=seqlen * rep, tm=tm * rep)
    gates = _nmm(x1, p["mix_norm"], p["w_gates"], act="sigmoid", out_dtype=BF16,
                 tm=min(1024, rows), tn=1024, name="gates")

    ya_tb, sre, sim = _s5(u_tb, p["s5_bm"], p["s5_cm"], p["s5_d"], p["s5_lre"], p["s5_lim"],
                          h0re, h0im, bsz=bsz, seqlen=seqlen, tb_len=min(256, seqlen))

    q = _qproj(cq, p["w_uq"], qcos, qsin, seqlen=seqlen * rep, tm=min(256, seqlen * rep))
    if lat_past is None:
        kt, v = _kvproj_t(lat, krp, p["w_ukt"], p["w_uv"], bsz=bsz, seqlen=seqlen, tm=tm)
        ymla = _attn(q, kt, v, bsz=bsz, seqlen=seqlen, tq=256, hg=2)
    else:
        past = lat_past.shape[1]
        n_k = past + seqlen
        lat_all = jnp.concatenate([lat_past, lat.reshape(bsz, seqlen, KV_LORA)], axis=1)
        kr_all = jnp.concatenate([kr_past, kr.reshape(bsz, seqlen, QK_ROPE)], axis=1)
        k_lat = jnp.concatenate([lat_all, kr_all, kr_all], axis=-1).astype(BF16).reshape(bsz * n_k, LAT_PAD)
        o_lat = _attn_cached(_qabsorb(q, p["w_ukt"]), k_lat, bsz=bsz, n_q=seqlen, n_k=n_k, past=past)
        ymla = _vabsorb(o_lat, p["w_uv"])

    ymem = _memattn(qm, mem_k, mem_v, bsz=bsz, seqlen=seqlen, tq=tm)

    x2 = _mergeout(x1, ya_tb, p["w_glu"], p["b_glu"], ymla, ymem, gates,
                   p["w_br_ssm"], p["w_br_mla"], p["w_br_mem"], p["w_out"],
                   bsz=bsz, seqlen=seqlen, tt=32)
    y = _ffn(x2, p["ffn2_norm"], p["ffn2_wg"], p["ffn2_wu"], p["ffn2_wd"], final_g, tm=min(512, rows))
    return y, lat, kr, sre, sim


def kernel(x_prompt, x_sample, cache_kv_latent, cache_k_rope, cache_mem_k, cache_mem_v, state_ssm_re, state_ssm_im, mem_prompt, ffn1_norm, ffn1_w_gate, ffn1_w_up, ffn1_w_down, mix_norm, w_in, q_norm, w_uq, kv_norm, w_uk, w_uv, ssm_a_re, ssm_a_im, ssm_log_dt, ssm_b_re, ssm_b_im, ssm_c_re, ssm_c_im, ssm_d, ssm_w_glu, ssm_b_glu, mem_norm, w_mem_k, w_mem_v, w_br_ssm, w_br_mla, w_br_mem, w_out, ffn2_norm, ffn2_w_gate, ffn2_w_up, ffn2_w_down, final_norm):
    bp, lp, _ = x_prompt.shape
    bs, ls, _ = x_sample.shape
    past = cache_kv_latent.shape[2]
    l = 0
    bf = lambda t: t.astype(BF16)

    wi = bf(w_in[l])
    c0 = SSM_WIDTH + Q_LORA + KV_LORA
    w_kr = wi[:, c0:c0 + QK_ROPE]
    half = QK_ROPE // 2
    zpad = jnp.zeros((D_MODEL, LANES - QK_ROPE), BF16)
    w_small = jnp.concatenate([
        wi[:, :c0], w_kr, zpad, w_kr[:, half:], w_kr[:, :half], zpad,
        wi[:, c0 + QK_ROPE:c0 + QK_ROPE + MEM_WIDTH]], axis=1)
    w_gates = wi[:, c0 + QK_ROPE + MEM_WIDTH:]

    wq = w_uq[l].reshape(Q_LORA, MLA_HEADS, QK_NOPE + QK_ROPE)
    wq_r = wq[:, :, QK_NOPE:]
    hw = MLA_HEADS * LANES
    w_uq3 = jnp.concatenate([
        wq[:, :, :QK_NOPE].reshape(Q_LORA, hw),
        wq_r.reshape(Q_LORA, hw // 2),
        jnp.concatenate([wq_r[..., half:], wq_r[..., :half]], axis=-1).reshape(Q_LORA, hw // 2)], axis=1)

    bm, cm, d, lre, lim = _s5_params(ssm_a_re[l], ssm_a_im[l], ssm_log_dt[l], ssm_b_re[l], ssm_b_im[l],
                                     ssm_c_re[l], ssm_c_im[l], ssm_d[l])
    p = {
        "ffn1_norm": ffn1_norm[l][None], "ffn1_wg": bf(ffn1_w_gate[l]), "ffn1_wu": bf(ffn1_w_up[l]),
        "ffn1_wd": bf(ffn1_w_down[l]),
        "mix_norm": mix_norm[l][None], "w_small": bf(w_small), "w_gates": bf(w_gates),
        "q_norm": q_norm[l][None], "kv_norm": kv_norm[l][None],
        "w_uq": bf(w_uq3), "w_ukt": bf(w_uk[l].T), "w_uv": bf(w_uv[l]),
        "s5_bm": bm, "s5_cm": cm, "s5_d": d, "s5_lre": lre, "s5_lim": lim,
        "w_glu": bf(ssm_w_glu[l]), "b_glu": ssm_b_glu[l][None],
        "w_br_ssm": bf(w_br_ssm[l]), "w_br_mla": bf(w_br_mla[l]), "w_br_mem": bf(w_br_mem[l]),
        "w_out": bf(w_out[l]),
        "ffn2_norm": ffn2_norm[l][None], "ffn2_wg": bf(ffn2_w_gate[l]), "ffn2_wu": bf(ffn2_w_up[l]),
        "ffn2_wd": bf(ffn2_w_down[l]),
    }
    fg = final_norm[None]

    w_mem = bf(jnp.concatenate([w_mem_k[l], w_mem_v[l]], axis=1))
    mkv = _nmm(mem_prompt.reshape(bp * MEM_TOKENS, D_MODEL), mem_norm[l][None], w_mem,
               act=None, out_dtype=F32, tm=512, tn=2 * MEM_WIDTH, name="memkv")
    mk_p = mkv[:, :MEM_WIDTH].reshape(bp, MEM_TOKENS, MEM_WIDTH)
    mv_p = mkv[:, MEM_WIDTH:].reshape(bp, MEM_TOKENS, MEM_WIDTH)

    n_state = SSM_GROUPS * SSM_STATE
    zero_state = jnp.zeros((bp, n_state), F32)
    yp, lat_p, kr_p, sre_p, sim_p = _layer(
        x_prompt.reshape(bp * lp, D_MODEL), bp, lp, 0, mk_p, mv_p, None, None,
        zero_state, zero_state, p, fg)

    ys, lat_s, kr_s, sre_s, sim_s = _layer(
        x_sample.reshape(bs * ls, D_MODEL), bs, ls, past,
        cache_mem_k[l].reshape(bs, MEM_TOKENS, MEM_WIDTH), cache_mem_v[l].reshape(bs, MEM_TOKENS, MEM_WIDTH),
        cache_kv_latent[l], cache_k_rope[l],
        state_ssm_re[l].reshape(bs, n_state), state_ssm_im[l].reshape(bs, n_state), p, fg)

    st = lambda t, b: t.reshape(1, b, SSM_GROUPS, SSM_STATE)
    return (yp.reshape(bp, lp, D_MODEL), ys.reshape(bs, ls, D_MODEL),
            lat_p.reshape(1, bp, lp, KV_LORA), kr_p.reshape(1, bp, lp, QK_ROPE),
            mk_p.reshape(1, bp, MEM_TOKENS, MEM_HEADS, MEM_HEAD_DIM),
            mv_p.reshape(1, bp, MEM_TOKENS, MEM_HEADS, MEM_HEAD_DIM),
            st(sre_p, bp), st(sim_p, bp),
            lat_s.reshape(1, bs, ls, KV_LORA), kr_s.reshape(1, bs, ls, QK_ROPE),
            st(sre_s, bs), st(sim_s, bs))
```

```python
import functools
import math

import jax
import jax.numpy as jnp
from jax import lax
from jax.experimental import pallas as pl
from jax.experimental.pallas import tpu as pltpu

F32 = jnp.float32
BF16 = jnp.bfloat16

D_MODEL = 2048
D_FF = 5632
CHUNK = 64
SSM_WIDTH = D_MODEL // 2
SSM_GROUP = 16
SSM_GROUPS = SSM_WIDTH // SSM_GROUP
SSM_STATE = 64
MLA_HEADS = 16
QK_NOPE = 128
QK_ROPE = 64
V_DIM = 128
Q_LORA = 768
KV_LORA = 512
ROPE_THETA = 10000.0
MEM_TOKENS = 256
MEM_HEADS = 4
MEM_HEAD_DIM = 128
MEM_WIDTH = MEM_HEADS * MEM_HEAD_DIM
RMS_EPS = 1e-6
NEG_INF = -1e30
MLA_SCALE = (QK_NOPE + QK_ROPE) ** -0.5
MEM_SCALE = MEM_HEAD_DIM ** -0.5

LANES = 128
HEAD_PAD = 2 * LANES
SSM_CH = 128
SSM_CG = SSM_CH // SSM_GROUP
SSM_CS = SSM_CG * SSM_STATE
MIB = 1024 * 1024


def _cp(sem, vmem_mib):
    return pltpu.CompilerParams(dimension_semantics=sem, vmem_limit_bytes=int(vmem_mib * MIB))


def _dot(a, b):
    return jnp.dot(a, b, preferred_element_type=F32)


def _rms(x, g):
    return x * lax.rsqrt(jnp.mean(x * x, axis=-1, keepdims=True) + RMS_EPS) * g


def _sigmoid(x):
    return 1.0 / (1.0 + jnp.exp(-x))


def _gelu_tanh(x):
    cdf = 0.5 * (1.0 + jnp.tanh(math.sqrt(2.0 / math.pi) * (x + 0.044715 * (x * x * x))))
    return x * cdf


def _ffn_body(x_ref, g_ref, wg_ref, wu_ref, wd_ref, *rest, final, tf, tf_last):
    if final:
        fg_ref, o_ref, h_scr, acc_scr = rest
    else:
        o_ref, h_scr, acc_scr = rest
    j = pl.program_id(1)
    last = pl.num_programs(1) - 1

    def contribution(h, width):
        a = _dot(h, wg_ref[:, 0:width])
        b = _dot(h, wu_ref[:, 0:width])
        act = ((a * _sigmoid(a)) * b).astype(BF16)
        return _dot(act, wd_ref[0:width, :])

    @pl.when(j == 0)
    def _():
        h = _rms(x_ref[...], g_ref[...]).astype(BF16)
        h_scr[...] = h
        acc_scr[...] = contribution(h, tf)

    @pl.when(jnp.logical_and(j > 0, j < last))
    def _():
        acc_scr[...] += contribution(h_scr[...], tf)

    @pl.when(j == last)
    def _():
        y = x_ref[...] + 0.5 * (acc_scr[...] + contribution(h_scr[...], tf_last))
        if final:
            y = _rms(y, fg_ref[...])
        o_ref[...] = y


def _ffn(x, g, wg, wu, wd, final_g=None, *, tm=512, tf=1024):
    nf = pl.cdiv(D_FF, tf)
    tf_last = D_FF - (nf - 1) * tf
    rows = x.shape[0]
    final = final_g is not None
    in_specs = [
        pl.BlockSpec((tm, D_MODEL), lambda i, j: (i, 0)),
        pl.BlockSpec((1, D_MODEL), lambda i, j: (0, 0)),
        pl.BlockSpec((D_MODEL, tf), lambda i, j: (0, j)),
        pl.BlockSpec((D_MODEL, tf), lambda i, j: (0, j)),
        pl.BlockSpec((tf, D_MODEL), lambda i, j: (j, 0)),
    ]
    args = [x, g, wg, wu, wd]
    if final:
        in_specs.append(pl.BlockSpec((1, D_MODEL), lambda i, j: (0, 0)))
        args.append(final_g)
    return pl.pallas_call(
        functools.partial(_ffn_body, final=final, tf=tf, tf_last=tf_last),
        grid=(rows // tm, nf),
        in_specs=in_specs,
        out_specs=pl.BlockSpec((tm, D_MODEL), lambda i, j: (i, 0)),
        out_shape=jax.ShapeDtypeStruct((rows, D_MODEL), F32),
        scratch_shapes=[pltpu.VMEM((tm, D_MODEL), BF16), pltpu.VMEM((tm, D_MODEL), F32)],
        compiler_params=_cp(("parallel", "arbitrary"), 56),
        name="ffn",
    )(*args)


def _nmm_body(x_ref, g_ref, w_ref, o_ref, h_scr, *, act):
    def tile(h):
        z = _dot(h, w_ref[...])
        if act == "sigmoid":
            z = _sigmoid(z)
        o_ref[...] = z.astype(o_ref.dtype)

    @pl.when(pl.program_id(1) == 0)
    def _():
        h = _rms(x_ref[...], g_ref[...]).astype(BF16)
        h_scr[...] = h
        tile(h)

    @pl.when(pl.program_id(1) > 0)
    def _():
        tile(h_scr[...])


def _nmm(x, g, w, *, act, out_dtype, tm, tn, name):
    rows, k = x.shape
    n = w.shape[1]
    return pl.pallas_call(
        functools.partial(_nmm_body, act=act),
        grid=(rows // tm, n // tn),
        in_specs=[
            pl.BlockSpec((tm, k), lambda i, j: (i, 0)),
            pl.BlockSpec((1, k), lambda i, j: (0, 0)),
            pl.BlockSpec((k, tn), lambda i, j: (0, j)),
        ],
        out_specs=pl.BlockSpec((tm, tn), lambda i, j: (i, j)),
        out_shape=jax.ShapeDtypeStruct((rows, n), out_dtype),
        scratch_shapes=[pltpu.VMEM((tm, k), BF16)],
        compiler_params=_cp(("parallel", "arbitrary"), 40),
        name=name,
    )(x, g, w)


_C_U = SSM_WIDTH
_C_Q = _C_U + Q_LORA
_C_KV = _C_Q + KV_LORA
_C_KX = _C_KV + LANES
_C_KS = _C_KX + LANES
_C_QM = _C_KS + MEM_WIDTH


def _inproj_body(x_ref, g_ref, w_ref, qg_ref, kvg_ref, cos_ref, sin_ref,
                 u_ref, cq_ref, lat_ref, kr_ref, krp_ref, qm_ref):
    h = _rms(x_ref[...], g_ref[...]).astype(BF16)
    u_ref[...] = _dot(h, w_ref[:, 0:_C_U])
    cq_ref[...] = _rms(_dot(h, w_ref[:, _C_U:_C_Q]), qg_ref[...]).astype(BF16)
    lat_ref[...] = _rms(_dot(h, w_ref[:, _C_Q:_C_KV]), kvg_ref[...])
    kx = _dot(h, w_ref[:, _C_KV:_C_KX])
    ks = _dot(h, w_ref[:, _C_KX:_C_KS])
    r = kx * cos_ref[...] + ks * sin_ref[...]
    krp_ref[...] = r
    kr_ref[...] = r[:, :QK_ROPE]
    qm_ref[...] = _dot(h, w_ref[:, _C_KS:_C_QM]).astype(BF16)


def _inproj(x, g, w, qg, kvg, cos, sin, *, bsz, seqlen, tm):
    rows = bsz * seqlen
    nt = seqlen // tm
    full = lambda i: (0, 0)
    row = lambda i: (i, 0)
    return pl.pallas_call(
        _inproj_body,
        grid=(rows // tm,),
        in_specs=[
            pl.BlockSpec((tm, D_MODEL), row),
            pl.BlockSpec((1, D_MODEL), full),
            pl.BlockSpec((D_MODEL, _C_QM), full),
            pl.BlockSpec((1, Q_LORA), full),
            pl.BlockSpec((1, KV_LORA), full),
            pl.BlockSpec((tm, LANES), lambda i: (i % nt, 0)),
            pl.BlockSpec((tm, LANES), lambda i: (i % nt, 0)),
        ],
        out_specs=[
            pl.BlockSpec((tm, SSM_WIDTH), row),
            pl.BlockSpec((tm, Q_LORA), row),
            pl.BlockSpec((tm, KV_LORA), row),
            pl.BlockSpec((tm, QK_ROPE), row),
            pl.BlockSpec((tm, LANES), row),
            pl.BlockSpec((tm, MEM_WIDTH), row),
        ],
        out_shape=[
            jax.ShapeDtypeStruct((rows, SSM_WIDTH), F32),
            jax.ShapeDtypeStruct((rows, Q_LORA), BF16),
            jax.ShapeDtypeStruct((rows, KV_LORA), F32),
            jax.ShapeDtypeStruct((rows, QK_ROPE), F32),
            jax.ShapeDtypeStruct((rows, LANES), F32),
            jax.ShapeDtypeStruct((rows, MEM_WIDTH), BF16),
        ],
        compiler_params=_cp(("parallel",), 56),
        name="inproj",
    )(x, g, w, qg, kvg, cos, sin)


def _qproj_body(cq_ref, w_ref, cos_ref, sin_ref, q_ref):
    cq = cq_ref[...]
    hw = MLA_HEADS * LANES
    nope = _dot(cq, w_ref[:, 0:hw])
    for h in range(MLA_HEADS):
        q_ref[:, h * HEAD_PAD:h * HEAD_PAD + LANES] = nope[:, h * LANES:(h + 1) * LANES].astype(BF16)
    pw = hw // 2
    rx = _dot(cq, w_ref[:, hw:hw + pw])
    rs = _dot(cq, w_ref[:, hw + pw:hw + 2 * pw])
    c = cos_ref[...]
    s = sin_ref[...]
    for j in range(MLA_HEADS // 2):
        sl = slice(j * LANES, (j + 1) * LANES)
        r = (rx[:, sl] * c + rs[:, sl] * s).astype(BF16)
        for h in (2 * j, 2 * j + 1):
            q_ref[:, h * HEAD_PAD + LANES:(h + 1) * HEAD_PAD] = r


def _qproj(cq, w, cos, sin, *, seqlen, tm):
    rows = cq.shape[0]
    nt = seqlen // tm
    return pl.pallas_call(
        _qproj_body,
        grid=(rows // tm,),
        in_specs=[
            pl.BlockSpec((tm, Q_LORA), lambda i: (i, 0)),
            pl.BlockSpec((Q_LORA, 2 * MLA_HEADS * LANES), lambda i: (0, 0)),
            pl.BlockSpec((tm, LANES), lambda i: (i % nt, 0)),
            pl.BlockSpec((tm, LANES), lambda i: (i % nt, 0)),
        ],
        out_specs=pl.BlockSpec((tm, MLA_HEADS * HEAD_PAD), lambda i: (i, 0)),
        out_shape=jax.ShapeDtypeStruct((rows, MLA_HEADS * HEAD_PAD), BF16),
        compiler_params=_cp(("parallel",), 48),
        name="qproj",
    )(cq, w, cos, sin)


def _kvproj_t_body(lat_ref, krp_ref, wukt_ref, wuv_ref, kt_ref, v_ref):
    lat = lat_ref[...]
    v_ref[...] = _dot(lat.astype(BF16), wuv_ref[...]).astype(BF16)
    kt = _dot(wukt_ref[...], lat.T.astype(BF16))
    krt = krp_ref[...].T.astype(BF16)
    krt_odd = jnp.concatenate([krt[QK_ROPE:], krt[:QK_ROPE]], axis=0)
    for h in range(MLA_HEADS):
        kt_ref[h * HEAD_PAD:h * HEAD_PAD + LANES, :] = kt[h * LANES:(h + 1) * LANES, :].astype(BF16)
        kt_ref[h * HEAD_PAD + LANES:(h + 1) * HEAD_PAD, :] = krt_odd if h % 2 else krt


def _kvproj_t(lat, krp, wukt, wuv, *, bsz, seqlen, tm):
    rows = bsz * seqlen
    nt = seqlen // tm
    return pl.pallas_call(
        _kvproj_t_body,
        grid=(rows // tm,),
        in_specs=[
            pl.BlockSpec((tm, KV_LORA), lambda i: (i, 0)),
            pl.BlockSpec((tm, LANES), lambda i: (i, 0)),
            pl.BlockSpec((MLA_HEADS * QK_NOPE, KV_LORA), lambda i: (0, 0)),
            pl.BlockSpec((KV_LORA, MLA_HEADS * V_DIM), lambda i: (0, 0)),
        ],
        out_specs=[
            pl.BlockSpec((None, MLA_HEADS * HEAD_PAD, tm), lambda i: (i // nt, 0, i % nt)),
            pl.BlockSpec((tm, MLA_HEADS * V_DIM), lambda i: (i, 0)),
        ],
        out_shape=[
            jax.ShapeDtypeStruct((bsz, MLA_HEADS * HEAD_PAD, seqlen), BF16),
            jax.ShapeDtypeStruct((rows, MLA_HEADS * V_DIM), BF16),
        ],
        compiler_params=_cp(("parallel",), 48),
        name="kvproj_t",
    )(lat, krp, wukt, wuv)


LAT_PAD = KV_LORA + LANES


def _qabsorb_body(q_ref, wukt_ref, o_ref):
    ql = _dot(q_ref[:, 0:LANES], wukt_ref[...])
    o_ref[:, 0:KV_LORA] = ql.astype(BF16)
    rope = q_ref[:, LANES:HEAD_PAD]
    lane = lax.broadcasted_iota(jnp.int32, rope.shape, 1)
    own = (lane < QK_ROPE) == (pl.program_id(0) % 2 == 0)
    o_ref[:, KV_LORA:LAT_PAD] = jnp.where(own, rope, jnp.zeros_like(rope))


def _qabsorb(q, wukt):
    rows = q.shape[0]
    return pl.pallas_call(
        _qabsorb_body,
        grid=(MLA_HEADS,),
        in_specs=[
            pl.BlockSpec((rows, HEAD_PAD), lambda h: (0, h)),
            pl.BlockSpec((QK_NOPE, KV_LORA), lambda h: (h, 0)),
        ],
        out_specs=pl.BlockSpec((None, rows, LAT_PAD), lambda h: (h, 0, 0)),
        out_shape=jax.ShapeDtypeStruct((MLA_HEADS, rows, LAT_PAD), BF16),
        compiler_params=_cp(("parallel",), 24),
        name="qabsorb",
    )(q, wukt)


def _vabsorb_body(o_ref, wuv_ref, y_ref):
    y_ref[...] = _dot(o_ref[...], wuv_ref[...]).astype(BF16)


def _vabsorb(o_lat, wuv):
    rows = o_lat.shape[1]
    return pl.pallas_call(
        _vabsorb_body,
        grid=(MLA_HEADS,),
        in_specs=[
            pl.BlockSpec((None, rows, KV_LORA), lambda h: (h, 0, 0)),
            pl.BlockSpec((KV_LORA, V_DIM), lambda h: (0, h)),
        ],
        out_specs=pl.BlockSpec((rows, V_DIM), lambda h: (0, h)),
        out_shape=jax.ShapeDtypeStruct((rows, MLA_HEADS * V_DIM), BF16),
        compiler_params=_cp(("parallel",), 24),
        name="vabsorb",
    )(o_lat, wuv)


def _chunk_mask(tq, tk, q0, k0):
    r = lax.broadcasted_iota(jnp.int32, (tq, tk), 0) + q0
    c = lax.broadcasted_iota(jnp.int32, (tq, tk), 1) + k0
    return (c // CHUNK) <= (r // CHUNK)


_MLA_EXP2_SCALE = MLA_SCALE * math.log2(math.e)


def _attn_body(q_ref, kt_ref, v_ref, o_ref, *, tq, ts, hg, nq):
    visible = _chunk_mask(tq, tq, 0, 0)
    for vq in range(nq):
        n_past = vq * tq
        rows = slice(n_past, n_past + tq)
        for h in range(hg):
            q = q_ref[rows, h * HEAD_PAD:(h + 1) * HEAD_PAD]
            hk = slice(h * HEAD_PAD, (h + 1) * HEAD_PAD)
            hv = slice(h * V_DIM, (h + 1) * V_DIM)
            m = l = acc = None
            strips = [(n_past, tq)] + [(k0, min(ts, n_past - k0)) for k0 in range(0, n_past, ts)]
            for k0, kw in strips:
                s = _dot(q, kt_ref[hk, k0:k0 + kw]) * _MLA_EXP2_SCALE
                if k0 == n_past:
                    s = jnp.where(visible, s, NEG_INF)
                m_strip = jnp.max(s, axis=-1, keepdims=True)
                m_new = m_strip if m is None else jnp.maximum(m, m_strip)
                p = jnp.exp2(s - m_new)
                l_strip = p[:, 0:LANES]
                for c0 in range(LANES, kw, LANES):
                    l_strip = l_strip + p[:, c0:c0 + LANES]
                pv = _dot(p.astype(BF16), v_ref[k0:k0 + kw, hv])
                if m is None:
                    l, acc = l_strip, pv
                else:
                    alpha = jnp.exp2(m - m_new)
                    l = alpha * l + l_strip
                    acc = alpha * acc + pv
                m = m_new
            o_ref[rows, hv] = (acc / jnp.sum(l, axis=-1, keepdims=True)).astype(BF16)


def _attn(q, kt, v, *, bsz, seqlen, tq, hg):
    nq = seqlen // tq
    return pl.pallas_call(
        functools.partial(_attn_body, tq=tq, ts=tq, hg=hg, nq=nq),
        grid=(bsz, MLA_HEADS // hg),
        in_specs=[
            pl.BlockSpec((seqlen, hg * HEAD_PAD), lambda b, g: (b, g)),
            pl.BlockSpec((None, hg * HEAD_PAD, seqlen), lambda b, g: (b, g, 0)),
            pl.BlockSpec((seqlen, hg * V_DIM), lambda b, g: (b, g)),
        ],
        out_specs=pl.BlockSpec((seqlen, hg * V_DIM), lambda b, g: (b, g)),
        out_shape=jax.ShapeDtypeStruct((bsz * seqlen, MLA_HEADS * V_DIM), BF16),
        compiler_params=_cp(("parallel", "parallel"), 48),
        name="attn",
    )(q, kt, v)


def _attn_cached_body(q_ref, k_ref, o_ref, *, n_q, n_k, past):
    q = q_ref[...].reshape(MLA_HEADS * n_q, LAT_PAD)
    k = k_ref[...]
    s = lax.dot_general(q, k, (((1,), (1,)), ((), ())), preferred_element_type=F32) * _MLA_EXP2_SCALE
    s = s.reshape(MLA_HEADS, n_q, n_k)
    s = jnp.where(_chunk_mask(n_q, n_k, past, 0)[None], s, NEG_INF)
    m = jnp.max(s, axis=-1, keepdims=True)
    p = jnp.exp2(s - m)
    l = jnp.sum(p, axis=-1, keepdims=True)
    acc = _dot(p.reshape(MLA_HEADS * n_q, n_k).astype(BF16), k[:, 0:KV_LORA])
    o_ref[...] = (acc.reshape(MLA_HEADS, n_q, KV_LORA) / l).astype(BF16)


def _attn_cached(q_lat, k_lat, *, bsz, n_q, n_k, past):
    return pl.pallas_call(
        functools.partial(_attn_cached_body, n_q=n_q, n_k=n_k, past=past),
        grid=(bsz,),
        in_specs=[
            pl.BlockSpec((MLA_HEADS, n_q, LAT_PAD), lambda b: (0, b, 0)),
            pl.BlockSpec((n_k, LAT_PAD), lambda b: (b, 0)),
        ],
        out_specs=pl.BlockSpec((MLA_HEADS, n_q, KV_LORA), lambda b: (0, b, 0)),
        out_shape=jax.ShapeDtypeStruct((MLA_HEADS, bsz * n_q, KV_LORA), BF16),
        compiler_params=_cp(("parallel",), 48),
        name="attn_cached",
    )(q_lat, k_lat)


def _memattn_body(q_ref, k_ref, v_ref, o_ref):
    kt = k_ref[...].T.astype(BF16)
    vb = v_ref[...].astype(BF16)
    for h in range(MEM_HEADS):
        sl = slice(h * MEM_HEAD_DIM, (h + 1) * MEM_HEAD_DIM)
        s = _dot(q_ref[:, sl], kt[sl, :]) * MEM_SCALE
        m = jnp.max(s, axis=-1, keepdims=True)
        p = jnp.exp(s - m)
        l = jnp.sum(p, axis=-1, keepdims=True)
        o_ref[:, sl] = (_dot(p.astype(BF16), vb[:, sl]) / l).astype(BF16)


def _memattn(q, k, v, *, bsz, seqlen, tq):
    nt = seqlen // tq
    return pl.pallas_call(
        _memattn_body,
        grid=(bsz, nt),
        in_specs=[
            pl.BlockSpec((tq, MEM_WIDTH), lambda b, i: (b * nt + i, 0)),
            pl.BlockSpec((None, MEM_TOKENS, MEM_WIDTH), lambda b, i: (b, 0, 0)),
            pl.BlockSpec((None, MEM_TOKENS, MEM_WIDTH), lambda b, i: (b, 0, 0)),
        ],
        out_specs=pl.BlockSpec((tq, MEM_WIDTH), lambda b, i: (b * nt + i, 0)),
        out_shape=jax.ShapeDtypeStruct((bsz * seqlen, MEM_WIDTH), BF16),
        compiler_params=_cp(("parallel", "parallel"), 24),
        name="memattn",
    )(q, k, v)


S5_PAIR = 2


def _s5_body(*refs, tb_len, bsz):
    u_refs = refs[:bsz]
    (bm_ref, cm_ref, d_ref, lre_ref, lim_ref, h0re_ref, h0im_ref,
     y_ref, sre_ref, sim_ref, u_scr, bu_scr, xb_scr, st_scr) = refs[bsz:]
    tb = pl.program_id(1)

    @pl.when(tb == 0)
    def _():
        st_scr[0] = h0re_ref[...]
        st_scr[1] = h0im_ref[...]

    for k in range(S5_PAIR):
        for b in range(bsz):
            u_scr[k, pl.ds(b, tb_len, stride=bsz), :] = u_refs[b][:, k * SSM_CH:(k + 1) * SSM_CH]
    for k in range(S5_PAIR):
        bu_scr[k] = _dot(u_scr[k].astype(BF16), bm_ref[k])

    for k in range(S5_PAIR):
        cs = slice(k * SSM_CS, (k + 1) * SSM_CS)
        lre = jnp.broadcast_to(lre_ref[:, cs], (bsz, SSM_CS))
        lim = jnp.broadcast_to(lim_ref[:, cs], (bsz, SSM_CS))
        re = st_scr[0, :, cs]
        im = st_scr[1, :, cs]
        for t in range(0, tb_len, 2):
            pair = []
            for r0 in (t * bsz, (t + 1) * bsz):
                nre = (lre * re - lim * im) + bu_scr[k, r0:r0 + bsz, 0:SSM_CS]
                nim = (lre * im + lim * re) + bu_scr[k, r0:r0 + bsz, SSM_CS:2 * SSM_CS]
                re, im = nre, nim
                pair.append((nre, nim))
            rows = slice(t * bsz, (t + 2) * bsz)
            xb_scr[k, rows, 0:SSM_CS] = jnp.concatenate([pair[0][0], pair[1][0]], axis=0).astype(BF16)
            xb_scr[k, rows, SSM_CS:2 * SSM_CS] = jnp.concatenate([pair[0][1], pair[1][1]], axis=0).astype(BF16)
        st_scr[0, :, cs] = re
        st_scr[1, :, cs] = im
        y = _dot(xb_scr[k], cm_ref[k]) + d_ref[:, k * SSM_CH:(k + 1) * SSM_CH] * u_scr[k]
        y_ref[k] = _gelu_tanh(y)

    @pl.when(tb == pl.num_programs(1) - 1)
    def _():
        sre_ref[...] = st_scr[0]
        sim_ref[...] = st_scr[1]


def _s5(u, bm, cm, d, lre, lim, h0re, h0im, *, bsz, seqlen, tb_len):
    nc = SSM_WIDTH // SSM_CH
    nt = seqlen // tb_len
    pch, pcs = S5_PAIR * SSM_CH, S5_PAIR * SSM_CS
    u_specs = [pl.BlockSpec((tb_len, pch), lambda g, t, b=b: (b * nt + t, g)) for b in range(bsz)]
    return pl.pallas_call(
        functools.partial(_s5_body, tb_len=tb_len, bsz=bsz),
        grid=(nc // S5_PAIR, nt),
        in_specs=u_specs + [
            pl.BlockSpec((S5_PAIR, SSM_CH, 2 * SSM_CS), lambda g, t: (g, 0, 0)),
            pl.BlockSpec((S5_PAIR, 2 * SSM_CS, SSM_CH), lambda g, t: (g, 0, 0)),
            pl.BlockSpec((1, pch), lambda g, t: (0, g)),
            pl.BlockSpec((1, pcs), lambda g, t: (0, g)),
            pl.BlockSpec((1, pcs), lambda g, t: (0, g)),
            pl.BlockSpec((bsz, pcs), lambda g, t: (0, g)),
            pl.BlockSpec((bsz, pcs), lambda g, t: (0, g)),
        ],
        out_specs=[
            pl.BlockSpec((S5_PAIR, tb_len * bsz, SSM_CH), lambda g, t: (g, t, 0)),
            pl.BlockSpec((bsz, pcs), lambda g, t: (0, g)),
            pl.BlockSpec((bsz, pcs), lambda g, t: (0, g)),
        ],
        out_shape=[
            jax.ShapeDtypeStruct((nc, seqlen * bsz, SSM_CH), F32),
            jax.ShapeDtypeStruct((bsz, SSM_GROUPS * SSM_STATE), F32),
            jax.ShapeDtypeStruct((bsz, SSM_GROUPS * SSM_STATE), F32),
        ],
        scratch_shapes=[
            pltpu.VMEM((S5_PAIR, tb_len * bsz, SSM_CH), F32),
            pltpu.VMEM((S5_PAIR, tb_len * bsz, 2 * SSM_CS), F32),
            pltpu.VMEM((S5_PAIR, tb_len * bsz, 2 * SSM_CS), BF16),
            pltpu.VMEM((2, bsz, pcs), F32),
        ],
        compiler_params=_cp(("parallel", "arbitrary"), 48),
        name="s5",
    )(*([u] * bsz), bm, cm, d, lre, lim, h0re, h0im)


def _mergeout_body(x_ref, ya_ref, wglu_ref, bglu_ref, ymla_ref, ymem_ref, g_ref,
                   wbs_ref, wbm_ref, wbe_ref, wo_ref, o_ref, ya_scr, *, bsz, tt):
    rows = bsz * tt
    for c in range(SSM_WIDTH // SSM_CH):
        for b in range(bsz):
            ya_scr[b * tt:(b + 1) * tt, c * SSM_CH:(c + 1) * SSM_CH] = ya_ref[c, pl.ds(b, tt, stride=bsz), :]
    y = ya_scr[...]
    gate = _sigmoid(_dot(y.astype(BF16), wglu_ref[...]) + bglu_ref[...])
    yg = (y * gate).astype(BF16)
    g = g_ref[...].reshape(rows, 3 * D_MODEL)
    ymla = ymla_ref[...].reshape(rows, MLA_HEADS * V_DIM)
    ymem = ymem_ref[...].reshape(rows, MEM_WIDTH)
    m = (g[:, 0:D_MODEL].astype(F32) * _dot(yg, wbs_ref[...])
         + g[:, D_MODEL:2 * D_MODEL].astype(F32) * _dot(ymla, wbm_ref[...])
         + g[:, 2 * D_MODEL:3 * D_MODEL].astype(F32) * _dot(ymem, wbe_ref[...]))
    out = x_ref[...].reshape(rows, D_MODEL) + _dot(m.astype(BF16), wo_ref[...])
    o_ref[...] = out.reshape(bsz, tt, D_MODEL)


def _mergeout(x, ya_slabs, wglu, bglu, ymla, ymem, gates, wbs, wbm, wbe, wo, *, bsz, seqlen, tt):
    rows = bsz * seqlen
    nc = SSM_WIDTH // SSM_CH
    tile = lambda width: pl.BlockSpec((bsz, tt, width), lambda i: (0, i, 0))
    resident = lambda shape: pl.BlockSpec(shape, lambda i: (0, 0), pipeline_mode=pl.Buffered(1))
    v3 = lambda t: t.reshape(bsz, seqlen, t.shape[-1])
    out = pl.pallas_call(
        functools.partial(_mergeout_body, bsz=bsz, tt=tt),
        grid=(seqlen // tt,),
        in_specs=[
            tile(D_MODEL),
            pl.BlockSpec((nc, tt * bsz, SSM_CH), lambda i: (0, i, 0)),
            resident((SSM_WIDTH, SSM_WIDTH)),
            resident((1, SSM_WIDTH)),
            tile(MLA_HEADS * V_DIM),
            tile(MEM_WIDTH),
            tile(3 * D_MODEL),
            resident((SSM_WIDTH, D_MODEL)),
            resident((MLA_HEADS * V_DIM, D_MODEL)),
            resident((MEM_WIDTH, D_MODEL)),
            resident((D_MODEL, D_MODEL)),
        ],
        out_specs=tile(D_MODEL),
        out_shape=jax.ShapeDtypeStruct((bsz, seqlen, D_MODEL), F32),
        scratch_shapes=[pltpu.VMEM((bsz * tt, SSM_WIDTH), F32)],
        compiler_params=_cp(("parallel",), 56),
        name="mergeout",
    )(v3(x), ya_slabs, wglu, bglu, v3(ymla), v3(ymem), v3(gates), wbs, wbm, wbe, wo)
    return out.reshape(rows, D_MODEL)


def _rope_tables(pos):
    half = QK_ROPE // 2
    inv_freq = ROPE_THETA ** (-jnp.arange(half, dtype=F32) / half)
    ang = pos.astype(F32)[:, None] * inv_freq[None, :]
    cos, sin = jnp.cos(ang), jnp.sin(ang)
    zero = jnp.zeros((pos.shape[0], LANES - QK_ROPE), F32)
    k_tabs = (jnp.concatenate([cos, cos, zero], axis=1), jnp.concatenate([-sin, sin, zero], axis=1))
    q_tabs = (jnp.concatenate([cos, cos, cos, cos], axis=1), jnp.concatenate([-sin, sin, -sin, sin], axis=1))
    return k_tabs, q_tabs


def _s5_params(a_re, a_im, log_dt, b_re, b_im, c_re, c_im, d):
    dt = jnp.exp(log_dt)[:, None]
    mag = jnp.exp(a_re * dt)
    phase = a_im * dt
    lb_re, lb_im = mag * jnp.cos(phase), mag * jnp.sin(phase)
    den = a_re * a_re + a_im * a_im
    nr, ni = lb_re - 1.0, lb_im
    z_re = (nr * a_re + ni * a_im) / den
    z_im = (ni * a_re - nr * a_im) / den
    bb_re = z_re[..., None] * b_re - z_im[..., None] * b_im
    bb_im = z_re[..., None] * b_im + z_im[..., None] * b_re
    nc = SSM_GROUPS // SSM_CG
    eye = jnp.eye(SSM_CG, dtype=F32)

    def blk_b(t):
        t = t.reshape(nc, SSM_CG, SSM_STATE, SSM_GROUP)
        return jnp.einsum("cgph,gk->cghkp", t, eye).reshape(nc, SSM_CH, SSM_CS)

    def blk_c(t):
        t = t.reshape(nc, SSM_CG, SSM_GROUP, SSM_STATE)
        return jnp.einsum("cghp,gk->cgpkh", t, eye).reshape(nc, SSM_CS, SSM_CH)

    bm = jnp.concatenate([blk_b(bb_re), blk_b(bb_im)], axis=2).astype(BF16)
    cm = jnp.concatenate([blk_c(c_re), -blk_c(c_im)], axis=1).astype(BF16)
    return (bm, cm, d.reshape(1, SSM_WIDTH),
            lb_re.reshape(1, SSM_GROUPS * SSM_STATE), lb_im.reshape(1, SSM_GROUPS * SSM_STATE))


def _layer(x, bsz, seqlen, pos0, mem_k, mem_v, lat_past, kr_past, h0re, h0im, p, final_g):
    rows = bsz * seqlen
    tm = min(512, seqlen)
    rep = max(1, min(512, rows) // seqlen)
    (cos, sin), (qcos, qsin) = jax.tree.map(lambda t: jnp.tile(t, (rep, 1)),
                                            _rope_tables(pos0 + jnp.arange(seqlen)))

    x1 = _ffn(x, p["ffn1_norm"], p["ffn1_wg"], p["ffn1_wu"], p["ffn1_wd"], tm=min(512, rows))
    u_tb, cq, lat, kr, krp, qm = _inproj(x1, p["mix_norm"], p["w_small"], p["q_norm"], p["kv_norm"],
                                         cos, sin, bsz=bsz // rep, seqlen=seqlen * rep, tm=tm * rep)
    gates = _nmm(x1, p["mix_norm"], p["w_gates"], act="sigmoid", out_dtype=BF16,
                 tm=min(1024, rows), tn=1024, name="gates")

    ya_tb, sre, sim = _s5(u_tb, p["s5_bm"], p["s5_cm"], p["s5_d"], p["s5_lre"], p["s5_lim"],
                          h0re, h0im, bsz=bsz, seqlen=seqlen, tb_len=min(256, seqlen))

    q = _qproj(cq, p["w_uq"], qcos, qsin, seqlen=seqlen * rep, tm=min(256, seqlen * rep))
    if lat_past is None:
        kt, v = _kvproj_t(lat, krp, p["w_ukt"], p["w_uv"], bsz=bsz, seqlen=seqlen, tm=tm)
        ymla = _attn(q, kt, v, bsz=bsz, seqlen=seqlen, tq=256, hg=2)
    else:
        past = lat_past.shape[1]
        n_k = past + seqlen
        lat_all = jnp.concatenate([lat_past, lat.reshape(bsz, seqlen, KV_LORA)], axis=1)
        kr_all = jnp.concatenate([kr_past, kr.reshape(bsz, seqlen, QK_ROPE)], axis=1)
        k_lat = jnp.concatenate([lat_all, kr_all, kr_all], axis=-1).astype(BF16).reshape(bsz * n_k, LAT_PAD)
        o_lat = _attn_cached(_qabsorb(q, p["w_ukt"]), k_lat, bsz=bsz, n_q=seqlen, n_k=n_k, past=past)
        ymla = _vabsorb(o_lat, p["w_uv"])

    ymem = _memattn(qm, mem_k, mem_v, bsz=bsz, seqlen=seqlen, tq=tm)

    x2 = _mergeout(x1, ya_tb, p["w_glu"], p["b_glu"], ymla, ymem, gates,
                   p["w_br_ssm"], p["w_br_mla"], p["w_br_mem"], p["w_out"],
                   bsz=bsz, seqlen=seqlen, tt=32)
    y = _ffn(x2, p["ffn2_norm"], p["ffn2_wg"], p["ffn2_wu"], p["ffn2_wd"], final_g, tm=min(512, rows))
    return y, lat, kr, sre, sim


def kernel(x_prompt, x_sample, cache_kv_latent, cache_k_rope, cache_mem_k, cache_mem_v, state_ssm_re, state_ssm_im, mem_prompt, ffn1_norm, ffn1_w_gate, ffn1_w_up, ffn1_w_down, mix_norm, w_in, q_norm, w_uq, kv_norm, w_uk, w_uv, ssm_a_re, ssm_a_im, ssm_log_dt, ssm_b_re, ssm_b_im, ssm_c_re, ssm_c_im, ssm_d, ssm_w_glu, ssm_b_glu, mem_norm, w_mem_k, w_mem_v, w_br_ssm, w_br_mla, w_br_mem, w_out, ffn2_norm, ffn2_w_gate, ffn2_w_up, ffn2_w_down, final_norm):
    bp, lp, _ = x_prompt.shape
    bs, ls, _ = x_sample.shape
    past = cache_kv_latent.shape[2]
    l = 0
    bf = lambda t: t.astype(BF16)

    wi = bf(w_in[l])
    c0 = SSM_WIDTH + Q_LORA + KV_LORA
    w_kr = wi[:, c0:c0 + QK_ROPE]
    half = QK_ROPE // 2
    zpad = jnp.zeros((D_MODEL, LANES - QK_ROPE), BF16)
    w_small = jnp.concatenate([
        wi[:, :c0], w_kr, zpad, w_kr[:, half:], w_kr[:, :half], zpad,
        wi[:, c0 + QK_ROPE:c0 + QK_ROPE + MEM_WIDTH]], axis=1)
    w_gates = wi[:, c0 + QK_ROPE + MEM_WIDTH:]

    wq = w_uq[l].reshape(Q_LORA, MLA_HEADS, QK_NOPE + QK_ROPE)
    wq_r = wq[:, :, QK_NOPE:]
    hw = MLA_HEADS * LANES
    w_uq3 = jnp.concatenate([
        wq[:, :, :QK_NOPE].reshape(Q_LORA, hw),
        wq_r.reshape(Q_LORA, hw // 2),
        jnp.concatenate([wq_r[..., half:], wq_r[..., :half]], axis=-1).reshape(Q_LORA, hw // 2)], axis=1)

    bm, cm, d, lre, lim = _s5_params(ssm_a_re[l], ssm_a_im[l], ssm_log_dt[l], ssm_b_re[l], ssm_b_im[l],
                                     ssm_c_re[l], ssm_c_im[l], ssm_d[l])
    p = {
        "ffn1_norm": ffn1_norm[l][None], "ffn1_wg": bf(ffn1_w_gate[l]), "ffn1_wu": bf(ffn1_w_up[l]),
        "ffn1_wd": bf(ffn1_w_down[l]),
        "mix_norm": mix_norm[l][None], "w_small": bf(w_small), "w_gates": bf(w_gates),
        "q_norm": q_norm[l][None], "kv_norm": kv_norm[l][None],
        "w_uq": bf(w_uq3), "w_ukt": bf(w_uk[l].T), "w_uv": bf(w_uv[l]),
        "s5_bm": bm, "s5_cm": cm, "s5_d": d, "s5_lre": lre, "s5_lim": lim,
        "w_glu": bf(ssm_w_glu[l]), "b_glu": ssm_b_glu[l][None],
        "w_br_ssm": bf(w_br_ssm[l]), "w_br_mla": bf(w_br_mla[l]), "w_br_mem": bf(w_br_mem[l]),
        "w_out": bf(w_out[l]),
        "ffn2_norm": ffn2_norm[l][None], "ffn2_wg": bf(ffn2_w_gate[l]), "ffn2_wu": bf(ffn2_w_up[l]),
        "ffn2_wd": bf(ffn2_w_down[l]),
    }
    fg = final_norm[None]

    w_mem = bf(jnp.concatenate([w_mem_k[l], w_mem_v[l]], axis=1))
    mkv = _nmm(mem_prompt.reshape(bp * MEM_TOKENS, D_MODEL), mem_norm[l][None], w_mem,
               act=None, out_dtype=F32, tm=512, tn=2 * MEM_WIDTH, name="memkv")
    mk_p = mkv[:, :MEM_WIDTH].reshape(bp, MEM_TOKENS, MEM_WIDTH)
    mv_p = mkv[:, MEM_WIDTH:].reshape(bp, MEM_TOKENS, MEM_WIDTH)

    n_state = SSM_GROUPS * SSM_STATE
    zero_state = jnp.zeros((bp, n_state), F32)
    yp, lat_p, kr_p, sre_p, sim_p = _layer(
        x_prompt.reshape(bp * lp, D_MODEL), bp, lp, 0, mk_p, mv_p, None, None,
        zero_state, zero_state, p, fg)

    ys, lat_s, kr_s, sre_s, sim_s = _layer(
        x_sample.reshape(bs * ls, D_MODEL), bs, ls, past,
        cache_mem_k[l].reshape(bs, MEM_TOKENS, MEM_WIDTH), cache_mem_v[l].reshape(bs, MEM_TOKENS, MEM_WIDTH),
        cache_kv_latent[l], cache_k_rope[l],
        state_ssm_re[l].reshape(bs, n_state), state_ssm_im[l].reshape(bs, n_state), p, fg)

    st = lambda t, b: t.reshape(1, b, SSM_GROUPS, SSM_STATE)
    return (yp.reshape(bp, lp, D_MODEL), ys.reshape(bs, ls, D_MODEL),
            lat_p.reshape(1, bp, lp, KV_LORA), kr_p.reshape(1, bp, lp, QK_ROPE),
            mk_p.reshape(1, bp, MEM_TOKENS, MEM_HEADS, MEM_HEAD_DIM),
            mv_p.reshape(1, bp, MEM_TOKENS, MEM_HEADS, MEM_HEAD_DIM),
            st(sre_p, bp), st(sim_p, bp),
            lat_s.reshape(1, bs, ls, KV_LORA), kr_s.reshape(1, bs, ls, QK_ROPE),
            st(sre_s, bs), st(sim_s, bs))
```

```python
import functools
import math

import jax
import jax.numpy as jnp
from jax import lax
from jax.experimental import pallas as pl
from jax.experimental.pallas import tpu as pltpu

F32 = jnp.float32
BF16 = jnp.bfloat16

D_MODEL = 2048
D_FF = 5632
CHUNK = 64
SSM_WIDTH = D_MODEL // 2
SSM_GROUP = 16
SSM_GROUPS = SSM_WIDTH // SSM_GROUP
SSM_STATE = 64
MLA_HEADS = 16
QK_NOPE = 128
QK_ROPE = 64
V_DIM = 128
Q_LORA = 768
KV_LORA = 512
ROPE_THETA = 10000.0
MEM_TOKENS = 256
MEM_HEADS = 4
MEM_HEAD_DIM = 128
MEM_WIDTH = MEM_HEADS * MEM_HEAD_DIM
RMS_EPS = 1e-6
NEG_INF = -1e30
MLA_SCALE = (QK_NOPE + QK_ROPE) ** -0.5
MEM_SCALE = MEM_HEAD_DIM ** -0.5

LANES = 128
HEAD_PAD = 2 * LANES
SSM_CH = 128
SSM_CG = SSM_CH // SSM_GROUP
SSM_CS = SSM_CG * SSM_STATE
MIB = 1024 * 1024


def _cp(sem, vmem_mib):
    return pltpu.CompilerParams(dimension_semantics=sem, vmem_limit_bytes=int(vmem_mib * MIB))


def _dot(a, b):
    return jnp.dot(a, b, preferred_element_type=F32)


def _rms(x, g):
    return x * lax.rsqrt(jnp.mean(x * x, axis=-1, keepdims=True) + RMS_EPS) * g


def _sigmoid(x):
    return 1.0 / (1.0 + jnp.exp(-x))


def _gelu_tanh(x):
    cdf = 0.5 * (1.0 + jnp.tanh(math.sqrt(2.0 / math.pi) * (x + 0.044715 * (x * x * x))))
    return x * cdf


def _ffn_body(x_ref, g_ref, wg_ref, wu_ref, wd_ref, *rest, final, tf, tf_last):
    if final:
        fg_ref, o_ref, h_scr = rest
    else:
        o_ref, h_scr = rest
    acc_scr = o_ref
    j = pl.program_id(1)
    last = pl.num_programs(1) - 1

    def contribution(h, width):
        a = _dot(h, wg_ref[:, 0:width])
        b = _dot(h, wu_ref[:, 0:width])
        act = ((a * _sigmoid(a)) * b).astype(BF16)
        return _dot(act, wd_ref[0:width, :])

    @pl.when(j == 0)
    def _():
        h = _rms(x_ref[...], g_ref[...]).astype(BF16)
        h_scr[...] = h
        acc_scr[...] = contribution(h, tf)

    @pl.when(jnp.logical_and(j > 0, j < last))
    def _():
        acc_scr[...] += contribution(h_scr[...], tf)

    @pl.when(j == last)
    def _():
        y = x_ref[...] + 0.5 * (acc_scr[...] + contribution(h_scr[...], tf_last))
        if final:
            y = _rms(y, fg_ref[...])
        o_ref[...] = y


def _ffn(x, g, wg, wu, wd, final_g=None, *, tm=512, tf=512):
    nf = pl.cdiv(D_FF, tf)
    tf_last = D_FF - (nf - 1) * tf
    rows = x.shape[0]
    final = final_g is not None
    in_specs = [
        pl.BlockSpec((tm, D_MODEL), lambda i, j: (i, 0)),
        pl.BlockSpec((1, D_MODEL), lambda i, j: (0, 0)),
        pl.BlockSpec((D_MODEL, tf), lambda i, j: (0, j)),
        pl.BlockSpec((D_MODEL, tf), lambda i, j: (0, j)),
        pl.BlockSpec((tf, D_MODEL), lambda i, j: (j, 0)),
    ]
    args = [x, g, wg, wu, wd]
    if final:
        in_specs.append(pl.BlockSpec((1, D_MODEL), lambda i, j: (0, 0)))
        args.append(final_g)
    return pl.pallas_call(
        functools.partial(_ffn_body, final=final, tf=tf, tf_last=tf_last),
        grid=(rows // tm, nf),
        in_specs=in_specs,
        out_specs=pl.BlockSpec((tm, D_MODEL), lambda i, j: (i, 0)),
        out_shape=jax.ShapeDtypeStruct((rows, D_MODEL), F32),
        scratch_shapes=[pltpu.VMEM((tm, D_MODEL), BF16)],
        compiler_params=_cp(("parallel", "arbitrary"), 58),
        name="ffn",
    )(*args)


def _nmm_body(x_ref, g_ref, w_ref, o_ref, h_scr, *, act):
    def tile(h):
        z = _dot(h, w_ref[...])
        if act == "sigmoid":
            z = _sigmoid(z)
        o_ref[...] = z.astype(o_ref.dtype)

    @pl.when(pl.program_id(1) == 0)
    def _():
        h = _rms(x_ref[...], g_ref[...]).astype(BF16)
        h_scr[...] = h
        tile(h)

    @pl.when(pl.program_id(1) > 0)
    def _():
        tile(h_scr[...])


def _nmm(x, g, w, *, act, out_dtype, tm, tn, name):
    rows, k = x.shape
    n = w.shape[1]
    return pl.pallas_call(
        functools.partial(_nmm_body, act=act),
        grid=(rows // tm, n // tn),
        in_specs=[
            pl.BlockSpec((tm, k), lambda i, j: (i, 0)),
            pl.BlockSpec((1, k), lambda i, j: (0, 0)),
            pl.BlockSpec((k, tn), lambda i, j: (0, j)),
        ],
        out_specs=pl.BlockSpec((tm, tn), lambda i, j: (i, j)),
        out_shape=jax.ShapeDtypeStruct((rows, n), out_dtype),
        scratch_shapes=[pltpu.VMEM((tm, k), BF16)],
        compiler_params=_cp(("parallel", "arbitrary"), 40),
        name=name,
    )(x, g, w)


_C_U = SSM_WIDTH
_C_Q = _C_U + Q_LORA
_C_KV = _C_Q + KV_LORA
_C_KX = _C_KV + LANES
_C_KS = _C_KX + LANES
_C_QM = _C_KS + MEM_WIDTH


def _inproj_body(x_ref, g_ref, w_ref, qg_ref, kvg_ref, cos_ref, sin_ref,
                 u_ref, cq_ref, lat_ref, kr_ref, krp_ref, qm_ref):
    h = _rms(x_ref[...], g_ref[...]).astype(BF16)
    u_ref[...] = _dot(h, w_ref[:, 0:_C_U])
    cq_ref[...] = _rms(_dot(h, w_ref[:, _C_U:_C_Q]), qg_ref[...]).astype(BF16)
    lat_ref[...] = _rms(_dot(h, w_ref[:, _C_Q:_C_KV]), kvg_ref[...])
    kx = _dot(h, w_ref[:, _C_KV:_C_KX])
    ks = _dot(h, w_ref[:, _C_KX:_C_KS])
    r = kx * cos_ref[...] + ks * sin_ref[...]
    krp_ref[...] = r
    kr_ref[...] = r[:, :QK_ROPE]
    qm_ref[...] = _dot(h, w_ref[:, _C_KS:_C_QM]).astype(BF16)


def _inproj(x, g, w, qg, kvg, cos, sin, *, bsz, seqlen, tm):
    rows = bsz * seqlen
    nt = seqlen // tm
    full = lambda i: (0, 0)
    row = lambda i: (i, 0)
    return pl.pallas_call(
        _inproj_body,
        grid=(rows // tm,),
        in_specs=[
            pl.BlockSpec((tm, D_MODEL), row),
            pl.BlockSpec((1, D_MODEL), full),
            pl.BlockSpec((D_MODEL, _C_QM), full),
            pl.BlockSpec((1, Q_LORA), full),
            pl.BlockSpec((1, KV_LORA), full),
            pl.BlockSpec((tm, LANES), lambda i: (i % nt, 0)),
            pl.BlockSpec((tm, LANES), lambda i: (i % nt, 0)),
        ],
        out_specs=[
            pl.BlockSpec((tm, SSM_WIDTH), row),
            pl.BlockSpec((tm, Q_LORA), row),
            pl.BlockSpec((tm, KV_LORA), row),
            pl.BlockSpec((tm, QK_ROPE), row),
            pl.BlockSpec((tm, LANES), row),
            pl.BlockSpec((tm, MEM_WIDTH), row),
        ],
        out_shape=[
            jax.ShapeDtypeStruct((rows, SSM_WIDTH), F32),
            jax.ShapeDtypeStruct((rows, Q_LORA), BF16),
            jax.ShapeDtypeStruct((rows, KV_LORA), F32),
            jax.ShapeDtypeStruct((rows, QK_ROPE), F32),
            jax.ShapeDtypeStruct((rows, LANES), F32),
            jax.ShapeDtypeStruct((rows, MEM_WIDTH), BF16),
        ],
        compiler_params=_cp(("parallel",), 56),
        name="inproj",
    )(x, g, w, qg, kvg, cos, sin)


def _qproj_body(cq_ref, w_ref, cos_ref, sin_ref, q_ref):
    cq = cq_ref[...]
    hw = MLA_HEADS * LANES
    nope = _dot(cq, w_ref[:, 0:hw])
    for h in range(MLA_HEADS):
        q_ref[:, h * HEAD_PAD:h * HEAD_PAD + LANES] = nope[:, h * LANES:(h + 1) * LANES].astype(BF16)
    pw = hw // 2
    rx = _dot(cq, w_ref[:, hw:hw + pw])
    rs = _dot(cq, w_ref[:, hw + pw:hw + 2 * pw])
    c = cos_ref[...]
    s = sin_ref[...]
    for j in range(MLA_HEADS // 2):
        sl = slice(j * LANES, (j + 1) * LANES)
        r = (rx[:, sl] * c + rs[:, sl] * s).astype(BF16)
        for h in (2 * j, 2 * j + 1):
            q_ref[:, h * HEAD_PAD + LANES:(h + 1) * HEAD_PAD] = r


def _qproj(cq, w, cos, sin, *, seqlen, tm):
    rows = cq.shape[0]
    nt = seqlen // tm
    return pl.pallas_call(
        _qproj_body,
        grid=(rows // tm,),
        in_specs=[
            pl.BlockSpec((tm, Q_LORA), lambda i: (i, 0)),
            pl.BlockSpec((Q_LORA, 2 * MLA_HEADS * LANES), lambda i: (0, 0)),
            pl.BlockSpec((tm, LANES), lambda i: (i % nt, 0)),
            pl.BlockSpec((tm, LANES), lambda i: (i % nt, 0)),
        ],
        out_specs=pl.BlockSpec((tm, MLA_HEADS * HEAD_PAD), lambda i: (i, 0)),
        out_shape=jax.ShapeDtypeStruct((rows, MLA_HEADS * HEAD_PAD), BF16),
        compiler_params=_cp(("parallel",), 48),
        name="qproj",
    )(cq, w, cos, sin)


def _kvproj_t_body(lat_ref, krp_ref, wukt_ref, wuv_ref, kt_ref, v_ref):
    lat = lat_ref[...]
    v_ref[...] = _dot(lat.astype(BF16), wuv_ref[...]).astype(BF16)
    kt = _dot(wukt_ref[...], lat.T.astype(BF16))
    krt = krp_ref[...].T.astype(BF16)
    krt_odd = jnp.concatenate([krt[QK_ROPE:], krt[:QK_ROPE]], axis=0)
    for h in range(MLA_HEADS):
        kt_ref[h * HEAD_PAD:h * HEAD_PAD + LANES, :] = kt[h * LANES:(h + 1) * LANES, :].astype(BF16)
        kt_ref[h * HEAD_PAD + LANES:(h + 1) * HEAD_PAD, :] = krt_odd if h % 2 else krt


def _kvproj_t(lat, krp, wukt, wuv, *, bsz, seqlen, tm):
    rows = bsz * seqlen
    nt = seqlen // tm
    return pl.pallas_call(
        _kvproj_t_body,
        grid=(rows // tm,),
        in_specs=[
            pl.BlockSpec((tm, KV_LORA), lambda i: (i, 0)),
            pl.BlockSpec((tm, LANES), lambda i: (i, 0)),
            pl.BlockSpec((MLA_HEADS * QK_NOPE, KV_LORA), lambda i: (0, 0)),
            pl.BlockSpec((KV_LORA, MLA_HEADS * V_DIM), lambda i: (0, 0)),
        ],
        out_specs=[
            pl.BlockSpec((None, MLA_HEADS * HEAD_PAD, tm), lambda i: (i // nt, 0, i % nt)),
            pl.BlockSpec((tm, MLA_HEADS * V_DIM), lambda i: (i, 0)),
        ],
        out_shape=[
            jax.ShapeDtypeStruct((bsz, MLA_HEADS * HEAD_PAD, seqlen), BF16),
            jax.ShapeDtypeStruct((rows, MLA_HEADS * V_DIM), BF16),
        ],
        compiler_params=_cp(("parallel",), 48),
        name="kvproj_t",
    )(lat, krp, wukt, wuv)


LAT_PAD = KV_LORA + LANES


def _qabsorb_body(q_ref, wukt_ref, o_ref):
    ql = _dot(q_ref[:, 0:LANES], wukt_ref[...])
    o_ref[:, 0:KV_LORA] = ql.astype(BF16)
    rope = q_ref[:, LANES:HEAD_PAD]
    lane = lax.broadcasted_iota(jnp.int32, rope.shape, 1)
    own = (lane < QK_ROPE) == (pl.program_id(0) % 2 == 0)
    o_ref[:, KV_LORA:LAT_PAD] = jnp.where(own, rope, jnp.zeros_like(rope))


def _qabsorb(q, wukt):
    rows = q.shape[0]
    return pl.pallas_call(
        _qabsorb_body,
        grid=(MLA_HEADS,),
        in_specs=[
            pl.BlockSpec((rows, HEAD_PAD), lambda h: (0, h)),
            pl.BlockSpec((QK_NOPE, KV_LORA), lambda h: (h, 0)),
        ],
        out_specs=pl.BlockSpec((None, rows, LAT_PAD), lambda h: (h, 0, 0)),
        out_shape=jax.ShapeDtypeStruct((MLA_HEADS, rows, LAT_PAD), BF16),
        compiler_params=_cp(("parallel",), 24),
        name="qabsorb",
    )(q, wukt)


def _vabsorb_body(o_ref, wuv_ref, y_ref):
    y_ref[...] = _dot(o_ref[...], wuv_ref[...]).astype(BF16)


def _vabsorb(o_lat, wuv):
    rows = o_lat.shape[1]
    return pl.pallas_call(
        _vabsorb_body,
        grid=(MLA_HEADS,),
        in_specs=[
            pl.BlockSpec((None, rows, KV_LORA), lambda h: (h, 0, 0)),
            pl.BlockSpec((KV_LORA, V_DIM), lambda h: (0, h)),
        ],
        out_specs=pl.BlockSpec((rows, V_DIM), lambda h: (0, h)),
        out_shape=jax.ShapeDtypeStruct((rows, MLA_HEADS * V_DIM), BF16),
        compiler_params=_cp(("parallel",), 24),
        name="vabsorb",
    )(o_lat, wuv)


def _chunk_mask(tq, tk, q0, k0):
    r = lax.broadcasted_iota(jnp.int32, (tq, tk), 0) + q0
    c = lax.broadcasted_iota(jnp.int32, (tq, tk), 1) + k0
    return (c // CHUNK) <= (r // CHUNK)


_MLA_EXP2_SCALE = MLA_SCALE * math.log2(math.e)


def _attn_body(q_ref, kt_ref, v_ref, o_ref, *, tq, ts, hg, nq):
    visible = _chunk_mask(tq, tq, 0, 0)
    for vq in range(nq):
        n_past = vq * tq
        rows = slice(n_past, n_past + tq)
        for h in range(hg):
            q = q_ref[rows, h * HEAD_PAD:(h + 1) * HEAD_PAD]
            hk = slice(h * HEAD_PAD, (h + 1) * HEAD_PAD)
            hv = slice(h * V_DIM, (h + 1) * V_DIM)
            m = l = acc = None
            strips = [(n_past, tq)] + [(k0, min(ts, n_past - k0)) for k0 in range(0, n_past, ts)]
            for k0, kw in strips:
                s = _dot(q, kt_ref[hk, k0:k0 + kw]) * _MLA_EXP2_SCALE
                if k0 == n_past:
                    s = jnp.where(visible, s, NEG_INF)
                m_strip = jnp.max(s, axis=-1, keepdims=True)
                m_new = m_strip if m is None else jnp.maximum(m, m_strip)
                p = jnp.exp2(s - m_new)
                l_strip = p[:, 0:LANES]
                for c0 in range(LANES, kw, LANES):
                    l_strip = l_strip + p[:, c0:c0 + LANES]
                pv = _dot(p.astype(BF16), v_ref[k0:k0 + kw, hv])
                if m is None:
                    l, acc = l_strip, pv
                else:
                    alpha = jnp.exp2(m - m_new)
                    l = alpha * l + l_strip
                    acc = alpha * acc + pv
                m = m_new
            o_ref[rows, hv] = (acc / jnp.sum(l, axis=-1, keepdims=True)).astype(BF16)


def _attn(q, kt, v, *, bsz, seqlen, tq, hg):
    nq = seqlen // tq
    return pl.pallas_call(
        functools.partial(_attn_body, tq=tq, ts=tq, hg=hg, nq=nq),
        grid=(bsz, MLA_HEADS // hg),
        in_specs=[
            pl.BlockSpec((seqlen, hg * HEAD_PAD), lambda b, g: (b, g)),
            pl.BlockSpec((None, hg * HEAD_PAD, seqlen), lambda b, g: (b, g, 0)),
            pl.BlockSpec((seqlen, hg * V_DIM), lambda b, g: (b, g)),
        ],
        out_specs=pl.BlockSpec((seqlen, hg * V_DIM), lambda b, g: (b, g)),
        out_shape=jax.ShapeDtypeStruct((bsz * seqlen, MLA_HEADS * V_DIM), BF16),
        compiler_params=_cp(("parallel", "parallel"), 48),
        name="attn",
    )(q, kt, v)


def _attn_cached_body(q_ref, k_ref, o_ref, *, n_q, n_k, past):
    q = q_ref[...].reshape(MLA_HEADS * n_q, LAT_PAD)
    k = k_ref[...]
    s = lax.dot_general(q, k, (((1,), (1,)), ((), ())), preferred_element_type=F32) * _MLA_EXP2_SCALE
    s = s.reshape(MLA_HEADS, n_q, n_k)
    s = jnp.where(_chunk_mask(n_q, n_k, past, 0)[None], s, NEG_INF)
    m = jnp.max(s, axis=-1, keepdims=True)
    p = jnp.exp2(s - m)
    l = jnp.sum(p, axis=-1, keepdims=True)
    acc = _dot(p.reshape(MLA_HEADS * n_q, n_k).astype(BF16), k[:, 0:KV_LORA])
    o_ref[...] = (acc.reshape(MLA_HEADS, n_q, KV_LORA) / l).astype(BF16)


def _attn_cached(q_lat, k_lat, *, bsz, n_q, n_k, past):
    return pl.pallas_call(
        functools.partial(_attn_cached_body, n_q=n_q, n_k=n_k, past=past),
        grid=(bsz,),
        in_specs=[
            pl.BlockSpec((MLA_HEADS, n_q, LAT_PAD), lambda b: (0, b, 0)),
            pl.BlockSpec((n_k, LAT_PAD), lambda b: (b, 0)),
        ],
        out_specs=pl.BlockSpec((MLA_HEADS, n_q, KV_LORA), lambda b: (0, b, 0)),
        out_shape=jax.ShapeDtypeStruct((MLA_HEADS, bsz * n_q, KV_LORA), BF16),
        compiler_params=_cp(("parallel",), 48),
        name="attn_cached",
    )(q_lat, k_lat)


def _memattn_body(q_ref, k_ref, v_ref, o_ref):
    kt = k_ref[...].T.astype(BF16)
    vb = v_ref[...].astype(BF16)
    for h in range(MEM_HEADS):
        sl = slice(h * MEM_HEAD_DIM, (h + 1) * MEM_HEAD_DIM)
        s = _dot(q_ref[:, sl], kt[sl, :]) * MEM_SCALE
        m = jnp.max(s, axis=-1, keepdims=True)
        p = jnp.exp(s - m)
        l = jnp.sum(p, axis=-1, keepdims=True)
        o_ref[:, sl] = (_dot(p.astype(BF16), vb[:, sl]) / l).astype(BF16)


def _memattn(q, k, v, *, bsz, seqlen, tq):
    nt = seqlen // tq
    return pl.pallas_call(
        _memattn_body,
        grid=(bsz, nt),
        in_specs=[
            pl.BlockSpec((tq, MEM_WIDTH), lambda b, i: (b * nt + i, 0)),
            pl.BlockSpec((None, MEM_TOKENS, MEM_WIDTH), lambda b, i: (b, 0, 0)),
            pl.BlockSpec((None, MEM_TOKENS, MEM_WIDTH), lambda b, i: (b, 0, 0)),
        ],
        out_specs=pl.BlockSpec((tq, MEM_WIDTH), lambda b, i: (b * nt + i, 0)),
        out_shape=jax.ShapeDtypeStruct((bsz * seqlen, MEM_WIDTH), BF16),
        compiler_params=_cp(("parallel", "parallel"), 24),
        name="memattn",
    )(q, k, v)


S5_PAIR = 2


def _s5_body(*refs, tb_len, bsz):
    u_refs = refs[:bsz]
    (bm_ref, cm_ref, d_ref, lre_ref, lim_ref, h0re_ref, h0im_ref,
     y_ref, sre_ref, sim_ref, u_scr, bu_scr, xb_scr, st_scr) = refs[bsz:]
    tb = pl.program_id(1)

    @pl.when(tb == 0)
    def _():
        st_scr[0] = h0re_ref[...]
        st_scr[1] = h0im_ref[...]

    for k in range(S5_PAIR):
        for b in range(bsz):
            u_scr[k, pl.ds(b, tb_len, stride=bsz), :] = u_refs[b][:, k * SSM_CH:(k + 1) * SSM_CH]
    for k in range(S5_PAIR):
        bu_scr[k] = _dot(u_scr[k].astype(BF16), bm_ref[k])

    for k in range(S5_PAIR):
        cs = slice(k * SSM_CS, (k + 1) * SSM_CS)
        lre = jnp.broadcast_to(lre_ref[:, cs], (bsz, SSM_CS))
        lim = jnp.broadcast_to(lim_ref[:, cs], (bsz, SSM_CS))
        re = st_scr[0, :, cs]
        im = st_scr[1, :, cs]
        for t in range(0, tb_len, 2):
            pair = []
            for r0 in (t * bsz, (t + 1) * bsz):
                nre = (lre * re - lim * im) + bu_scr[k, r0:r0 + bsz, 0:SSM_CS]
                nim = (lre * im + lim * re) + bu_scr[k, r0:r0 + bsz, SSM_CS:2 * SSM_CS]
                re, im = nre, nim
                pair.append((nre, nim))
            rows = slice(t * bsz, (t + 2) * bsz)
            xb_scr[k, rows, 0:SSM_CS] = jnp.concatenate([pair[0][0], pair[1][0]], axis=0).astype(BF16)
            xb_scr[k, rows, SSM_CS:2 * SSM_CS] = jnp.concatenate([pair[0][1], pair[1][1]], axis=0).astype(BF16)
        st_scr[0, :, cs] = re
        st_scr[1, :, cs] = im
        y = _dot(xb_scr[k], cm_ref[k]) + d_ref[:, k * SSM_CH:(k + 1) * SSM_CH] * u_scr[k]
        y_ref[k] = _gelu_tanh(y)

    @pl.when(tb == pl.num_programs(1) - 1)
    def _():
        sre_ref[...] = st_scr[0]
        sim_ref[...] = st_scr[1]


def _s5(u, bm, cm, d, lre, lim, h0re, h0im, *, bsz, seqlen, tb_len):
    nc = SSM_WIDTH // SSM_CH
    nt = seqlen // tb_len
    pch, pcs = S5_PAIR * SSM_CH, S5_PAIR * SSM_CS
    u_specs = [pl.BlockSpec((tb_len, pch), lambda g, t, b=b: (b * nt + t, g)) for b in range(bsz)]
    return pl.pallas_call(
        functools.partial(_s5_body, tb_len=tb_len, bsz=bsz),
        grid=(nc // S5_PAIR, nt),
        in_specs=u_specs + [
            pl.BlockSpec((S5_PAIR, SSM_CH, 2 * SSM_CS), lambda g, t: (g, 0, 0)),
            pl.BlockSpec((S5_PAIR, 2 * SSM_CS, SSM_CH), lambda g, t: (g, 0, 0)),
            pl.BlockSpec((1, pch), lambda g, t: (0, g)),
            pl.BlockSpec((1, pcs), lambda g, t: (0, g)),
            pl.BlockSpec((1, pcs), lambda g, t: (0, g)),
            pl.BlockSpec((bsz, pcs), lambda g, t: (0, g)),
            pl.BlockSpec((bsz, pcs), lambda g, t: (0, g)),
        ],
        out_specs=[
            pl.BlockSpec((S5_PAIR, tb_len * bsz, SSM_CH), lambda g, t: (g, t, 0)),
            pl.BlockSpec((bsz, pcs), lambda g, t: (0, g)),
            pl.BlockSpec((bsz, pcs), lambda g, t: (0, g)),
        ],
        out_shape=[
            jax.ShapeDtypeStruct((nc, seqlen * bsz, SSM_CH), F32),
            jax.ShapeDtypeStruct((bsz, SSM_GROUPS * SSM_STATE), F32),
            jax.ShapeDtypeStruct((bsz, SSM_GROUPS * SSM_STATE), F32),
        ],
        scratch_shapes=[
            pltpu.VMEM((S5_PAIR, tb_len * bsz, SSM_CH), F32),
            pltpu.VMEM((S5_PAIR, tb_len * bsz, 2 * SSM_CS), F32),
            pltpu.VMEM((S5_PAIR, tb_len * bsz, 2 * SSM_CS), BF16),
            pltpu.VMEM((2, bsz, pcs), F32),
        ],
        compiler_params=_cp(("parallel", "arbitrary"), 48),
        name="s5",
    )(*([u] * bsz), bm, cm, d, lre, lim, h0re, h0im)


def _mergeout_body(x_ref, ya_ref, wglu_ref, bglu_ref, ymla_ref, ymem_ref, g_ref,
                   wbs_ref, wbm_ref, wbe_ref, wo_ref, o_ref, ya_scr, *, bsz, tt):
    rows = bsz * tt
    for c in range(SSM_WIDTH // SSM_CH):
        for b in range(bsz):
            ya_scr[b * tt:(b + 1) * tt, c * SSM_CH:(c + 1) * SSM_CH] = ya_ref[c, pl.ds(b, tt, stride=bsz), :]
    y = ya_scr[...]
    gate = _sigmoid(_dot(y.astype(BF16), wglu_ref[...]) + bglu_ref[...])
    yg = (y * gate).astype(BF16)
    g = g_ref[...].reshape(rows, 3 * D_MODEL)
    ymla = ymla_ref[...].reshape(rows, MLA_HEADS * V_DIM)
    ymem = ymem_ref[...].reshape(rows, MEM_WIDTH)
    m = (g[:, 0:D_MODEL].astype(F32) * _dot(yg, wbs_ref[...])
         + g[:, D_MODEL:2 * D_MODEL].astype(F32) * _dot(ymla, wbm_ref[...])
         + g[:, 2 * D_MODEL:3 * D_MODEL].astype(F32) * _dot(ymem, wbe_ref[...]))
    out = x_ref[...].reshape(rows, D_MODEL) + _dot(m.astype(BF16), wo_ref[...])
    o_ref[...] = out.reshape(bsz, tt, D_MODEL)


def _mergeout(x, ya_slabs, wglu, bglu, ymla, ymem, gates, wbs, wbm, wbe, wo, *, bsz, seqlen, tt):
    rows = bsz * seqlen
    nc = SSM_WIDTH // SSM_CH
    tile = lambda width: pl.BlockSpec((bsz, tt, width), lambda i: (0, i, 0))
    resident = lambda shape: pl.BlockSpec(shape, lambda i: (0, 0), pipeline_mode=pl.Buffered(1))
    v3 = lambda t: t.reshape(bsz, seqlen, t.shape[-1])
    out = pl.pallas_call(
        functools.partial(_mergeout_body, bsz=bsz, tt=tt),
        grid=(seqlen // tt,),
        in_specs=[
            tile(D_MODEL),
            pl.BlockSpec((nc, tt * bsz, SSM_CH), lambda i: (0, i, 0)),
            resident((SSM_WIDTH, SSM_WIDTH)),
            resident((1, SSM_WIDTH)),
            tile(MLA_HEADS * V_DIM),
            tile(MEM_WIDTH),
            tile(3 * D_MODEL),
            resident((SSM_WIDTH, D_MODEL)),
            resident((MLA_HEADS * V_DIM, D_MODEL)),
            resident((MEM_WIDTH, D_MODEL)),
            resident((D_MODEL, D_MODEL)),
        ],
        out_specs=tile(D_MODEL),
        out_shape=jax.ShapeDtypeStruct((bsz, seqlen, D_MODEL), F32),
        scratch_shapes=[pltpu.VMEM((bsz * tt, SSM_WIDTH), F32)],
        compiler_params=_cp(("parallel",), 56),
        name="mergeout",
    )(v3(x), ya_slabs, wglu, bglu, v3(ymla), v3(ymem), v3(gates), wbs, wbm, wbe, wo)
    return out.reshape(rows, D_MODEL)


def _rope_tables(pos):
    half = QK_ROPE // 2
    inv_freq = ROPE_THETA ** (-jnp.arange(half, dtype=F32) / half)
    ang = pos.astype(F32)[:, None] * inv_freq[None, :]
    cos, sin = jnp.cos(ang), jnp.sin(ang)
    zero = jnp.zeros((pos.shape[0], LANES - QK_ROPE), F32)
    k_tabs = (jnp.concatenate([cos, cos, zero], axis=1), jnp.concatenate([-sin, sin, zero], axis=1))
    q_tabs = (jnp.concatenate([cos, cos, cos, cos], axis=1), jnp.concatenate([-sin, sin, -sin, sin], axis=1))
    return k_tabs, q_tabs


def _s5_params(a_re, a_im, log_dt, b_re, b_im, c_re, c_im, d):
    dt = jnp.exp(log_dt)[:, None]
    mag = jnp.exp(a_re * dt)
    phase = a_im * dt
    lb_re, lb_im = mag * jnp.cos(phase), mag * jnp.sin(phase)
    den = a_re * a_re + a_im * a_im
    nr, ni = lb_re - 1.0, lb_im
    z_re = (nr * a_re + ni * a_im) / den
    z_im = (ni * a_re - nr * a_im) / den
    bb_re = z_re[..., None] * b_re - z_im[..., None] * b_im
    bb_im = z_re[..., None] * b_im + z_im[..., None] * b_re
    nc = SSM_GROUPS // SSM_CG
    eye = jnp.eye(SSM_CG, dtype=F32)

    def blk_b(t):
        t = t.reshape(nc, SSM_CG, SSM_STATE, SSM_GROUP)
        return jnp.einsum("cgph,gk->cghkp", t, eye).reshape(nc, SSM_CH, SSM_CS)

    def blk_c(t):
        t = t.reshape(nc, SSM_CG, SSM_GROUP, SSM_STATE)
        return jnp.einsum("cghp,gk->cgpkh", t, eye).reshape(nc, SSM_CS, SSM_CH)

    bm = jnp.concatenate([blk_b(bb_re), blk_b(bb_im)], axis=2).astype(BF16)
    cm = jnp.concatenate([blk_c(c_re), -blk_c(c_im)], axis=1).astype(BF16)
    return (bm, cm, d.reshape(1, SSM_WIDTH),
            lb_re.reshape(1, SSM_GROUPS * SSM_STATE), lb_im.reshape(1, SSM_GROUPS * SSM_STATE))


def _layer(x, bsz, seqlen, pos0, mem_k, mem_v, lat_past, kr_past, h0re, h0im, p, final_g):
    rows = bsz * seqlen
    tm = min(512, seqlen)
    rep = max(1, min(512, rows) // seqlen)
    (cos, sin), (qcos, qsin) = jax.tree.map(lambda t: jnp.tile(t, (rep, 1)),
                                            _rope_tables(pos0 + jnp.arange(seqlen)))

    ffn_tm, ffn_tf = (1024, 256) if rows % 1024 == 0 else (512, 512)
    x1 = _ffn(x, p["ffn1_norm"], p["ffn1_wg"], p["ffn1_wu"], p["ffn1_wd"], tm=ffn_tm, tf=ffn_tf)
    u_tb, cq, lat, kr, krp, qm = _inproj(x1, p["mix_norm"], p["w_small"], p["q_norm"], p["kv_norm"],
                                         cos, sin, bsz=bsz // rep, seqlen=seqlen * rep, tm=tm * rep)
    gates = _nmm(x1, p["mix_norm"], p["w_gates"], act="sigmoid", out_dtype=BF16,
                 tm=min(1024, rows), tn=1024, name="gates")

    ya_tb, sre, sim = _s5(u_tb, p["s5_bm"], p["s5_cm"], p["s5_d"], p["s5_lre"], p["s5_lim"],
                          h0re, h0im, bsz=bsz, seqlen=seqlen, tb_len=min(256, seqlen))

    q = _qproj(cq, p["w_uq"], qcos, qsin, seqlen=seqlen * rep, tm=min(256, seqlen * rep))
    if lat_past is None:
        kt, v = _kvproj_t(lat, krp, p["w_ukt"], p["w_uv"], bsz=bsz, seqlen=seqlen, tm=tm)
        ymla = _attn(q, kt, v, bsz=bsz, seqlen=seqlen, tq=256, hg=2)
    else:
        past = lat_past.shape[1]
        n_k = past + seqlen
        lat_all = jnp.concatenate([lat_past, lat.reshape(bsz, seqlen, KV_LORA)], axis=1)
        kr_all = jnp.concatenate([kr_past, kr.reshape(bsz, seqlen, QK_ROPE)], axis=1)
        k_lat = jnp.concatenate([lat_all, kr_all, kr_all], axis=-1).astype(BF16).reshape(bsz * n_k, LAT_PAD)
        o_lat = _attn_cached(_qabsorb(q, p["w_ukt"]), k_lat, bsz=bsz, n_q=seqlen, n_k=n_k, past=past)
        ymla = _vabsorb(o_lat, p["w_uv"])

    ymem = _memattn(qm, mem_k, mem_v, bsz=bsz, seqlen=seqlen, tq=tm)

    x2 = _mergeout(x1, ya_tb, p["w_glu"], p["b_glu"], ymla, ymem, gates,
                   p["w_br_ssm"], p["w_br_mla"], p["w_br_mem"], p["w_out"],
                   bsz=bsz, seqlen=seqlen, tt=32)
    y = _ffn(x2, p["ffn2_norm"], p["ffn2_wg"], p["ffn2_wu"], p["ffn2_wd"], final_g, tm=ffn_tm, tf=ffn_tf)
    return y, lat, kr, sre, sim


def kernel(x_prompt, x_sample, cache_kv_latent, cache_k_rope, cache_mem_k, cache_mem_v, state_ssm_re, state_ssm_im, mem_prompt, ffn1_norm, ffn1_w_gate, ffn1_w_up, ffn1_w_down, mix_norm, w_in, q_norm, w_uq, kv_norm, w_uk, w_uv, ssm_a_re, ssm_a_im, ssm_log_dt, ssm_b_re, ssm_b_im, ssm_c_re, ssm_c_im, ssm_d, ssm_w_glu, ssm_b_glu, mem_norm, w_mem_k, w_mem_v, w_br_ssm, w_br_mla, w_br_mem, w_out, ffn2_norm, ffn2_w_gate, ffn2_w_up, ffn2_w_down, final_norm):
    bp, lp, _ = x_prompt.shape
    bs, ls, _ = x_sample.shape
    past = cache_kv_latent.shape[2]
    l = 0
    bf = lambda t: t.astype(BF16)

    wi = bf(w_in[l])
    c0 = SSM_WIDTH + Q_LORA + KV_LORA
    w_kr = wi[:, c0:c0 + QK_ROPE]
    half = QK_ROPE // 2
    zpad = jnp.zeros((D_MODEL, LANES - QK_ROPE), BF16)
    w_small = jnp.concatenate([
        wi[:, :c0], w_kr, zpad, w_kr[:, half:], w_kr[:, :half], zpad,
        wi[:, c0 + QK_ROPE:c0 + QK_ROPE + MEM_WIDTH]], axis=1)
    w_gates = wi[:, c0 + QK_ROPE + MEM_WIDTH:]

    wq = w_uq[l].reshape(Q_LORA, MLA_HEADS, QK_NOPE + QK_ROPE)
    wq_r = wq[:, :, QK_NOPE:]
    hw = MLA_HEADS * LANES
    w_uq3 = jnp.concatenate([
        wq[:, :, :QK_NOPE].reshape(Q_LORA, hw),
        wq_r.reshape(Q_LORA, hw // 2),
        jnp.concatenate([wq_r[..., half:], wq_r[..., :half]], axis=-1).reshape(Q_LORA, hw // 2)], axis=1)

    bm, cm, d, lre, lim = _s5_params(ssm_a_re[l], ssm_a_im[l], ssm_log_dt[l], ssm_b_re[l], ssm_b_im[l],
                                     ssm_c_re[l], ssm_c_im[l], ssm_d[l])
    p = {
        "ffn1_norm": ffn1_norm[l][None], "ffn1_wg": bf(ffn1_w_gate[l]), "ffn1_wu": bf(ffn1_w_up[l]),
        "ffn1_wd": bf(ffn1_w_down[l]),
        "mix_norm": mix_norm[l][None], "w_small": bf(w_small), "w_gates": bf(w_gates),
        "q_norm": q_norm[l][None], "kv_norm": kv_norm[l][None],
        "w_uq": bf(w_uq3), "w_ukt": bf(w_uk[l].T), "w_uv": bf(w_uv[l]),
        "s5_bm": bm, "s5_cm": cm, "s5_d": d, "s5_lre": lre, "s5_lim": lim,
        "w_glu": bf(ssm_w_glu[l]), "b_glu": ssm_b_glu[l][None],
        "w_br_ssm": bf(w_br_ssm[l]), "w_br_mla": bf(w_br_mla[l]), "w_br_mem": bf(w_br_mem[l]),
        "w_out": bf(w_out[l]),
        "ffn2_norm": ffn2_norm[l][None], "ffn2_wg": bf(ffn2_w_gate[l]), "ffn2_wu": bf(ffn2_w_up[l]),
        "ffn2_wd": bf(ffn2_w_down[l]),
    }
    fg = final_norm[None]

    w_mem = bf(jnp.concatenate([w_mem_k[l], w_mem_v[l]], axis=1))
    mkv = _nmm(mem_prompt.reshape(bp * MEM_TOKENS, D_MODEL), mem_norm[l][None], w_mem,
               act=None, out_dtype=F32, tm=512, tn=2 * MEM_WIDTH, name="memkv")
    mk_p = mkv[:, :MEM_WIDTH].reshape(bp, MEM_TOKENS, MEM_WIDTH)
    mv_p = mkv[:, MEM_WIDTH:].reshape(bp, MEM_TOKENS, MEM_WIDTH)

    n_state = SSM_GROUPS * SSM_STATE
    zero_state = jnp.zeros((bp, n_state), F32)
    yp, lat_p, kr_p, sre_p, sim_p = _layer(
        x_prompt.reshape(bp * lp, D_MODEL), bp, lp, 0, mk_p, mv_p, None, None,
        zero_state, zero_state, p, fg)

    ys, lat_s, kr_s, sre_s, sim_s = _layer(
        x_sample.reshape(bs * ls, D_MODEL), bs, ls, past,
        cache_mem_k[l].reshape(bs, MEM_TOKENS, MEM_WIDTH), cache_mem_v[l].reshape(bs, MEM_TOKENS, MEM_WIDTH),
        cache_kv_latent[l], cache_k_rope[l],
        state_ssm_re[l].reshape(bs, n_state), state_ssm_im[l].reshape(bs, n_state), p, fg)

    st = lambda t, b: t.reshape(1, b, SSM_GROUPS, SSM_STATE)
    return (yp.reshape(bp, lp, D_MODEL), ys.reshape(bs, ls, D_MODEL),
            lat_p.reshape(1, bp, lp, KV_LORA), kr_p.reshape(1, bp, lp, QK_ROPE),
            mk_p.reshape(1, bp, MEM_TOKENS, MEM_HEADS, MEM_HEAD_DIM),
            mv_p.reshape(1, bp, MEM_TOKENS, MEM_HEADS, MEM_HEAD_DIM),
            st(sre_p, bp), st(sim_p, bp),
            lat_s.reshape(1, bs, ls, KV_LORA), kr_s.reshape(1, bs, ls, QK_ROPE),
            st(sre_s, bs), st(sim_s, bs))
```

```python
import functools
import math

import jax
import jax.numpy as jnp
from jax import lax
from jax.experimental import pallas as pl
from jax.experimental.pallas import tpu as pltpu

F32 = jnp.float32
BF16 = jnp.bfloat16

D_MODEL = 2048
D_FF = 5632
CHUNK = 64
SSM_WIDTH = D_MODEL // 2
SSM_GROUP = 16
SSM_GROUPS = SSM_WIDTH // SSM_GROUP
SSM_STATE = 64
MLA_HEADS = 16
QK_NOPE = 128
QK_ROPE = 64
V_DIM = 128
Q_LORA = 768
KV_LORA = 512
ROPE_THETA = 10000.0
MEM_TOKENS = 256
MEM_HEADS = 4
MEM_HEAD_DIM = 128
MEM_WIDTH = MEM_HEADS * MEM_HEAD_DIM
RMS_EPS = 1e-6
NEG_INF = -1e30
MLA_SCALE = (QK_NOPE + QK_ROPE) ** -0.5
MEM_SCALE = MEM_HEAD_DIM ** -0.5

LANES = 128
HEAD_PAD = 2 * LANES
SSM_CH = 128
SSM_CG = SSM_CH // SSM_GROUP
SSM_CS = SSM_CG * SSM_STATE
MIB = 1024 * 1024


def _cp(sem, vmem_mib):
    return pltpu.CompilerParams(dimension_semantics=sem, vmem_limit_bytes=int(vmem_mib * MIB))


def _dot(a, b):
    return jnp.dot(a, b, preferred_element_type=F32)


def _rms(x, g):
    return x * lax.rsqrt(jnp.mean(x * x, axis=-1, keepdims=True) + RMS_EPS) * g


def _sigmoid(x):
    return 1.0 / (1.0 + jnp.exp(-x))


def _gelu_tanh(x):
    cdf = 0.5 * (1.0 + jnp.tanh(math.sqrt(2.0 / math.pi) * (x + 0.044715 * (x * x * x))))
    return x * cdf


def _ffn_body(x_ref, g_ref, wg_ref, wu_ref, wd_ref, *rest, final, tf, tf_last):
    if final:
        fg_ref, o_ref, h_scr = rest
    else:
        o_ref, h_scr = rest
    acc_scr = o_ref
    j = pl.program_id(1)
    last = pl.num_programs(1) - 1

    def contribution(h, width):
        a = _dot(h, wg_ref[:, 0:width].astype(BF16))
        b = _dot(h, wu_ref[:, 0:width].astype(BF16))
        act = ((a * _sigmoid(a)) * b).astype(BF16)
        return _dot(act, wd_ref[0:width, :].astype(BF16))

    @pl.when(j == 0)
    def _():
        h = _rms(x_ref[...], g_ref[...]).astype(BF16)
        h_scr[...] = h
        acc_scr[...] = contribution(h, tf)

    @pl.when(jnp.logical_and(j > 0, j < last))
    def _():
        acc_scr[...] += contribution(h_scr[...], tf)

    @pl.when(j == last)
    def _():
        y = x_ref[...] + 0.5 * (acc_scr[...] + contribution(h_scr[...], tf_last))
        if final:
            y = _rms(y, fg_ref[...])
        o_ref[...] = y


def _ffn(x, g, wg, wu, wd, final_g=None, *, tm=512, tf=512):
    nf = pl.cdiv(D_FF, tf)
    tf_last = D_FF - (nf - 1) * tf
    rows = x.shape[0]
    final = final_g is not None
    in_specs = [
        pl.BlockSpec((tm, D_MODEL), lambda i, j: (i, 0)),
        pl.BlockSpec((1, D_MODEL), lambda i, j: (0, 0)),
        pl.BlockSpec((D_MODEL, tf), lambda i, j: (0, j)),
        pl.BlockSpec((D_MODEL, tf), lambda i, j: (0, j)),
        pl.BlockSpec((tf, D_MODEL), lambda i, j: (j, 0)),
    ]
    args = [x, g, wg, wu, wd]
    if final:
        in_specs.append(pl.BlockSpec((1, D_MODEL), lambda i, j: (0, 0)))
        args.append(final_g)
    return pl.pallas_call(
        functools.partial(_ffn_body, final=final, tf=tf, tf_last=tf_last),
        grid=(rows // tm, nf),
        in_specs=in_specs,
        out_specs=pl.BlockSpec((tm, D_MODEL), lambda i, j: (i, 0)),
        out_shape=jax.ShapeDtypeStruct((rows, D_MODEL), F32),
        scratch_shapes=[pltpu.VMEM((tm, D_MODEL), BF16)],
        compiler_params=_cp(("parallel", "arbitrary"), 58),
        name="ffn",
    )(*args)


def _nmm_body(x_ref, g_ref, w_ref, o_ref, h_scr, *, act):
    def tile(h):
        z = _dot(h, w_ref[...])
        if act == "sigmoid":
            z = _sigmoid(z)
        o_ref[...] = z.astype(o_ref.dtype)

    @pl.when(pl.program_id(1) == 0)
    def _():
        h = _rms(x_ref[...], g_ref[...]).astype(BF16)
        h_scr[...] = h
        tile(h)

    @pl.when(pl.program_id(1) > 0)
    def _():
        tile(h_scr[...])


def _nmm(x, g, w, *, act, out_dtype, tm, tn, name):
    rows, k = x.shape
    n = w.shape[1]
    return pl.pallas_call(
        functools.partial(_nmm_body, act=act),
        grid=(rows // tm, n // tn),
        in_specs=[
            pl.BlockSpec((tm, k), lambda i, j: (i, 0)),
            pl.BlockSpec((1, k), lambda i, j: (0, 0)),
            pl.BlockSpec((k, tn), lambda i, j: (0, j)),
        ],
        out_specs=pl.BlockSpec((tm, tn), lambda i, j: (i, j)),
        out_shape=jax.ShapeDtypeStruct((rows, n), out_dtype),
        scratch_shapes=[pltpu.VMEM((tm, k), BF16)],
        compiler_params=_cp(("parallel", "arbitrary"), 40),
        name=name,
    )(x, g, w)


_C_U = SSM_WIDTH
_C_Q = _C_U + Q_LORA
_C_KV = _C_Q + KV_LORA
_C_KX = _C_KV + LANES
_C_KS = _C_KX + LANES
_C_QM = _C_KS + MEM_WIDTH


def _inproj_body(x_ref, g_ref, w_ref, qg_ref, kvg_ref, cos_ref, sin_ref,
                 u_ref, cq_ref, lat_ref, kr_ref, krp_ref, qm_ref):
    h = _rms(x_ref[...], g_ref[...]).astype(BF16)
    u_ref[...] = _dot(h, w_ref[:, 0:_C_U])
    cq_ref[...] = _rms(_dot(h, w_ref[:, _C_U:_C_Q]), qg_ref[...]).astype(BF16)
    lat_ref[...] = _rms(_dot(h, w_ref[:, _C_Q:_C_KV]), kvg_ref[...])
    kx = _dot(h, w_ref[:, _C_KV:_C_KX])
    ks = _dot(h, w_ref[:, _C_KX:_C_KS])
    r = kx * cos_ref[...] + ks * sin_ref[...]
    krp_ref[...] = r
    kr_ref[...] = r[:, :QK_ROPE]
    qm_ref[...] = _dot(h, w_ref[:, _C_KS:_C_QM]).astype(BF16)


def _inproj(x, g, w, qg, kvg, cos, sin, *, bsz, seqlen, tm):
    rows = bsz * seqlen
    nt = seqlen // tm
    full = lambda i: (0, 0)
    row = lambda i: (i, 0)
    return pl.pallas_call(
        _inproj_body,
        grid=(rows // tm,),
        in_specs=[
            pl.BlockSpec((tm, D_MODEL), row),
            pl.BlockSpec((1, D_MODEL), full),
            pl.BlockSpec((D_MODEL, _C_QM), full),
            pl.BlockSpec((1, Q_LORA), full),
            pl.BlockSpec((1, KV_LORA), full),
            pl.BlockSpec((tm, LANES), lambda i: (i % nt, 0)),
            pl.BlockSpec((tm, LANES), lambda i: (i % nt, 0)),
        ],
        out_specs=[
            pl.BlockSpec((tm, SSM_WIDTH), row),
            pl.BlockSpec((tm, Q_LORA), row),
            pl.BlockSpec((tm, KV_LORA), row),
            pl.BlockSpec((tm, QK_ROPE), row),
            pl.BlockSpec((tm, LANES), row),
            pl.BlockSpec((tm, MEM_WIDTH), row),
        ],
        out_shape=[
            jax.ShapeDtypeStruct((rows, SSM_WIDTH), F32),
            jax.ShapeDtypeStruct((rows, Q_LORA), BF16),
            jax.ShapeDtypeStruct((rows, KV_LORA), F32),
            jax.ShapeDtypeStruct((rows, QK_ROPE), F32),
            jax.ShapeDtypeStruct((rows, LANES), F32),
            jax.ShapeDtypeStruct((rows, MEM_WIDTH), BF16),
        ],
        compiler_params=_cp(("parallel",), 56),
        name="inproj",
    )(x, g, w, qg, kvg, cos, sin)


def _qproj_body(cq_ref, w_ref, cos_ref, sin_ref, q_ref):
    cq = cq_ref[...]
    hw = MLA_HEADS * LANES
    nope = _dot(cq, w_ref[:, 0:hw])
    for h in range(MLA_HEADS):
        q_ref[:, h * HEAD_PAD:h * HEAD_PAD + LANES] = nope[:, h * LANES:(h + 1) * LANES].astype(BF16)
    pw = hw // 2
    rx = _dot(cq, w_ref[:, hw:hw + pw])
    rs = _dot(cq, w_ref[:, hw + pw:hw + 2 * pw])
    c = cos_ref[...]
    s = sin_ref[...]
    for j in range(MLA_HEADS // 2):
        sl = slice(j * LANES, (j + 1) * LANES)
        r = (rx[:, sl] * c + rs[:, sl] * s).astype(BF16)
        for h in (2 * j, 2 * j + 1):
            q_ref[:, h * HEAD_PAD + LANES:(h + 1) * HEAD_PAD] = r


def _qproj(cq, w, cos, sin, *, seqlen, tm):
    rows = cq.shape[0]
    nt = seqlen // tm
    return pl.pallas_call(
        _qproj_body,
        grid=(rows // tm,),
        in_specs=[
            pl.BlockSpec((tm, Q_LORA), lambda i: (i, 0)),
            pl.BlockSpec((Q_LORA, 2 * MLA_HEADS * LANES), lambda i: (0, 0)),
            pl.BlockSpec((tm, LANES), lambda i: (i % nt, 0)),
            pl.BlockSpec((tm, LANES), lambda i: (i % nt, 0)),
        ],
        out_specs=pl.BlockSpec((tm, MLA_HEADS * HEAD_PAD), lambda i: (i, 0)),
        out_shape=jax.ShapeDtypeStruct((rows, MLA_HEADS * HEAD_PAD), BF16),
        compiler_params=_cp(("parallel",), 48),
        name="qproj",
    )(cq, w, cos, sin)


def _kvproj_t_body(lat_ref, krp_ref, wukt_ref, wuv_ref, kt_ref, v_ref):
    lat = lat_ref[...]
    v_ref[...] = _dot(lat.astype(BF16), wuv_ref[...]).astype(BF16)
    kt = _dot(wukt_ref[...], lat.T.astype(BF16))
    krt = krp_ref[...].T.astype(BF16)
    krt_odd = jnp.concatenate([krt[QK_ROPE:], krt[:QK_ROPE]], axis=0)
    for h in range(MLA_HEADS):
        kt_ref[h * HEAD_PAD:h * HEAD_PAD + LANES, :] = kt[h * LANES:(h + 1) * LANES, :].astype(BF16)
        kt_ref[h * HEAD_PAD + LANES:(h + 1) * HEAD_PAD, :] = krt_odd if h % 2 else krt


def _kvproj_t(lat, krp, wukt, wuv, *, bsz, seqlen, tm):
    rows = bsz * seqlen
    nt = seqlen // tm
    return pl.pallas_call(
        _kvproj_t_body,
        grid=(rows // tm,),
        in_specs=[
            pl.BlockSpec((tm, KV_LORA), lambda i: (i, 0)),
            pl.BlockSpec((tm, LANES), lambda i: (i, 0)),
            pl.BlockSpec((MLA_HEADS * QK_NOPE, KV_LORA), lambda i: (0, 0)),
            pl.BlockSpec((KV_LORA, MLA_HEADS * V_DIM), lambda i: (0, 0)),
        ],
        out_specs=[
            pl.BlockSpec((None, MLA_HEADS * HEAD_PAD, tm), lambda i: (i // nt, 0, i % nt)),
            pl.BlockSpec((tm, MLA_HEADS * V_DIM), lambda i: (i, 0)),
        ],
        out_shape=[
            jax.ShapeDtypeStruct((bsz, MLA_HEADS * HEAD_PAD, seqlen), BF16),
            jax.ShapeDtypeStruct((rows, MLA_HEADS * V_DIM), BF16),
        ],
        compiler_params=_cp(("parallel",), 48),
        name="kvproj_t",
    )(lat, krp, wukt, wuv)


LAT_PAD = KV_LORA + LANES


def _qabsorb_body(q_ref, wukt_ref, o_ref):
    ql = _dot(q_ref[:, 0:LANES], wukt_ref[...])
    o_ref[:, 0:KV_LORA] = ql.astype(BF16)
    rope = q_ref[:, LANES:HEAD_PAD]
    lane = lax.broadcasted_iota(jnp.int32, rope.shape, 1)
    own = (lane < QK_ROPE) == (pl.program_id(0) % 2 == 0)
    o_ref[:, KV_LORA:LAT_PAD] = jnp.where(own, rope, jnp.zeros_like(rope))


def _qabsorb(q, wukt):
    rows = q.shape[0]
    return pl.pallas_call(
        _qabsorb_body,
        grid=(MLA_HEADS,),
        in_specs=[
            pl.BlockSpec((rows, HEAD_PAD), lambda h: (0, h)),
            pl.BlockSpec((QK_NOPE, KV_LORA), lambda h: (h, 0)),
        ],
        out_specs=pl.BlockSpec((None, rows, LAT_PAD), lambda h: (h, 0, 0)),
        out_shape=jax.ShapeDtypeStruct((MLA_HEADS, rows, LAT_PAD), BF16),
        compiler_params=_cp(("parallel",), 24),
        name="qabsorb",
    )(q, wukt)


def _vabsorb_body(o_ref, wuv_ref, y_ref):
    y_ref[...] = _dot(o_ref[...], wuv_ref[...]).astype(BF16)


def _vabsorb(o_lat, wuv):
    rows = o_lat.shape[1]
    return pl.pallas_call(
        _vabsorb_body,
        grid=(MLA_HEADS,),
        in_specs=[
            pl.BlockSpec((None, rows, KV_LORA), lambda h: (h, 0, 0)),
            pl.BlockSpec((KV_LORA, V_DIM), lambda h: (0, h)),
        ],
        out_specs=pl.BlockSpec((rows, V_DIM), lambda h: (0, h)),
        out_shape=jax.ShapeDtypeStruct((rows, MLA_HEADS * V_DIM), BF16),
        compiler_params=_cp(("parallel",), 24),
        name="vabsorb",
    )(o_lat, wuv)


def _chunk_mask(tq, tk, q0, k0):
    r = lax.broadcasted_iota(jnp.int32, (tq, tk), 0) + q0
    c = lax.broadcasted_iota(jnp.int32, (tq, tk), 1) + k0
    return (c // CHUNK) <= (r // CHUNK)


_MLA_EXP2_SCALE = MLA_SCALE * math.log2(math.e)


def _attn_body(q_ref, kt_ref, v_ref, o_ref, *, tq, ts, hg, nq):
    visible = _chunk_mask(tq, tq, 0, 0)
    for vq in range(nq):
        n_past = vq * tq
        rows = slice(n_past, n_past + tq)
        for h in range(hg):
            q = q_ref[rows, h * HEAD_PAD:(h + 1) * HEAD_PAD]
            hk = slice(h * HEAD_PAD, (h + 1) * HEAD_PAD)
            hv = slice(h * V_DIM, (h + 1) * V_DIM)
            m = l = acc = None
            strips = [(n_past, tq)] + [(k0, min(ts, n_past - k0)) for k0 in range(0, n_past, ts)]
            for k0, kw in strips:
                s = _dot(q, kt_ref[hk, k0:k0 + kw]) * _MLA_EXP2_SCALE
                if k0 == n_past:
                    s = jnp.where(visible, s, NEG_INF)
                m_strip = jnp.max(s, axis=-1, keepdims=True)
                m_new = m_strip if m is None else jnp.maximum(m, m_strip)
                p = jnp.exp2(s - m_new)
                l_strip = p[:, 0:LANES]
                for c0 in range(LANES, kw, LANES):
                    l_strip = l_strip + p[:, c0:c0 + LANES]
                pv = _dot(p.astype(BF16), v_ref[k0:k0 + kw, hv])
                if m is None:
                    l, acc = l_strip, pv
                else:
                    alpha = jnp.exp2(m - m_new)
                    l = alpha * l + l_strip
                    acc = alpha * acc + pv
                m = m_new
            o_ref[rows, hv] = (acc / jnp.sum(l, axis=-1, keepdims=True)).astype(BF16)


def _attn(q, kt, v, *, bsz, seqlen, tq, hg):
    nq = seqlen // tq
    return pl.pallas_call(
        functools.partial(_attn_body, tq=tq, ts=tq, hg=hg, nq=nq),
        grid=(bsz, MLA_HEADS // hg),
        in_specs=[
            pl.BlockSpec((seqlen, hg * HEAD_PAD), lambda b, g: (b, g)),
            pl.BlockSpec((None, hg * HEAD_PAD, seqlen), lambda b, g: (b, g, 0)),
            pl.BlockSpec((seqlen, hg * V_DIM), lambda b, g: (b, g)),
        ],
        out_specs=pl.BlockSpec((seqlen, hg * V_DIM), lambda b, g: (b, g)),
        out_shape=jax.ShapeDtypeStruct((bsz * seqlen, MLA_HEADS * V_DIM), BF16),
        compiler_params=_cp(("parallel", "parallel"), 48),
        name="attn",
    )(q, kt, v)


def _attn_cached_body(q_ref, k_ref, o_ref, *, n_q, n_k, past):
    q = q_ref[...].reshape(MLA_HEADS * n_q, LAT_PAD)
    k = k_ref[...]
    s = lax.dot_general(q, k, (((1,), (1,)), ((), ())), preferred_element_type=F32) * _MLA_EXP2_SCALE
    s = s.reshape(MLA_HEADS, n_q, n_k)
    s = jnp.where(_chunk_mask(n_q, n_k, past, 0)[None], s, NEG_INF)
    m = jnp.max(s, axis=-1, keepdims=True)
    p = jnp.exp2(s - m)
    l = jnp.sum(p, axis=-1, keepdims=True)
    acc = _dot(p.reshape(MLA_HEADS * n_q, n_k).astype(BF16), k[:, 0:KV_LORA])
    o_ref[...] = (acc.reshape(MLA_HEADS, n_q, KV_LORA) / l).astype(BF16)


def _attn_cached(q_lat, k_lat, *, bsz, n_q, n_k, past):
    return pl.pallas_call(
        functools.partial(_attn_cached_body, n_q=n_q, n_k=n_k, past=past),
        grid=(bsz,),
        in_specs=[
            pl.BlockSpec((MLA_HEADS, n_q, LAT_PAD), lambda b: (0, b, 0)),
            pl.BlockSpec((n_k, LAT_PAD), lambda b: (b, 0)),
        ],
        out_specs=pl.BlockSpec((MLA_HEADS, n_q, KV_LORA), lambda b: (0, b, 0)),
        out_shape=jax.ShapeDtypeStruct((MLA_HEADS, bsz * n_q, KV_LORA), BF16),
        compiler_params=_cp(("parallel",), 48),
        name="attn_cached",
    )(q_lat, k_lat)


def _memattn_body(q_ref, k_ref, v_ref, o_ref):
    kt = k_ref[...].T.astype(BF16)
    vb = v_ref[...].astype(BF16)
    for h in range(MEM_HEADS):
        sl = slice(h * MEM_HEAD_DIM, (h + 1) * MEM_HEAD_DIM)
        s = _dot(q_ref[:, sl], kt[sl, :]) * MEM_SCALE
        m = jnp.max(s, axis=-1, keepdims=True)
        p = jnp.exp(s - m)
        l = jnp.sum(p, axis=-1, keepdims=True)
        o_ref[:, sl] = (_dot(p.astype(BF16), vb[:, sl]) / l).astype(BF16)


def _memattn(q, k, v, *, bsz, seqlen, tq):
    nt = seqlen // tq
    return pl.pallas_call(
        _memattn_body,
        grid=(bsz, nt),
        in_specs=[
            pl.BlockSpec((tq, MEM_WIDTH), lambda b, i: (b * nt + i, 0)),
            pl.BlockSpec((None, MEM_TOKENS, MEM_WIDTH), lambda b, i: (b, 0, 0)),
            pl.BlockSpec((None, MEM_TOKENS, MEM_WIDTH), lambda b, i: (b, 0, 0)),
        ],
        out_specs=pl.BlockSpec((tq, MEM_WIDTH), lambda b, i: (b * nt + i, 0)),
        out_shape=jax.ShapeDtypeStruct((bsz * seqlen, MEM_WIDTH), BF16),
        compiler_params=_cp(("parallel", "parallel"), 24),
        name="memattn",
    )(q, k, v)


S5_PAIR = 2


def _s5_body(*refs, tb_len, bsz):
    u_refs = refs[:bsz]
    (bm_ref, cm_ref, d_ref, lre_ref, lim_ref, h0re_ref, h0im_ref,
     y_ref, sre_ref, sim_ref, u_scr, bu_scr, xb_scr, st_scr) = refs[bsz:]
    tb = pl.program_id(1)

    @pl.when(tb == 0)
    def _():
        st_scr[0] = h0re_ref[...]
        st_scr[1] = h0im_ref[...]

    for k in range(S5_PAIR):
        for b in range(bsz):
            u_scr[k, pl.ds(b, tb_len, stride=bsz), :] = u_refs[b][:, k * SSM_CH:(k + 1) * SSM_CH]
    for k in range(S5_PAIR):
        bu_scr[k] = _dot(u_scr[k].astype(BF16), bm_ref[k])

    for k in range(S5_PAIR):
        cs = slice(k * SSM_CS, (k + 1) * SSM_CS)
        lre = jnp.broadcast_to(lre_ref[:, cs], (bsz, SSM_CS))
        lim = jnp.broadcast_to(lim_ref[:, cs], (bsz, SSM_CS))
        re = st_scr[0, :, cs]
        im = st_scr[1, :, cs]
        for t in range(0, tb_len, 2):
            pair = []
            for r0 in (t * bsz, (t + 1) * bsz):
                nre = (lre * re - lim * im) + bu_scr[k, r0:r0 + bsz, 0:SSM_CS]
                nim = (lre * im + lim * re) + bu_scr[k, r0:r0 + bsz, SSM_CS:2 * SSM_CS]
                re, im = nre, nim
                pair.append((nre, nim))
            rows = slice(t * bsz, (t + 2) * bsz)
            xb_scr[k, rows, 0:SSM_CS] = jnp.concatenate([pair[0][0], pair[1][0]], axis=0).astype(BF16)
            xb_scr[k, rows, SSM_CS:2 * SSM_CS] = jnp.concatenate([pair[0][1], pair[1][1]], axis=0).astype(BF16)
        st_scr[0, :, cs] = re
        st_scr[1, :, cs] = im
        y = _dot(xb_scr[k], cm_ref[k]) + d_ref[:, k * SSM_CH:(k + 1) * SSM_CH] * u_scr[k]
        y_ref[k] = _gelu_tanh(y)

    @pl.when(tb == pl.num_programs(1) - 1)
    def _():
        sre_ref[...] = st_scr[0]
        sim_ref[...] = st_scr[1]


def _s5(u, bm, cm, d, lre, lim, h0re, h0im, *, bsz, seqlen, tb_len):
    nc = SSM_WIDTH // SSM_CH
    nt = seqlen // tb_len
    pch, pcs = S5_PAIR * SSM_CH, S5_PAIR * SSM_CS
    u_specs = [pl.BlockSpec((tb_len, pch), lambda g, t, b=b: (b * nt + t, g)) for b in range(bsz)]
    return pl.pallas_call(
        functools.partial(_s5_body, tb_len=tb_len, bsz=bsz),
        grid=(nc // S5_PAIR, nt),
        in_specs=u_specs + [
            pl.BlockSpec((S5_PAIR, SSM_CH, 2 * SSM_CS), lambda g, t: (g, 0, 0)),
            pl.BlockSpec((S5_PAIR, 2 * SSM_CS, SSM_CH), lambda g, t: (g, 0, 0)),
            pl.BlockSpec((1, pch), lambda g, t: (0, g)),
            pl.BlockSpec((1, pcs), lambda g, t: (0, g)),
            pl.BlockSpec((1, pcs), lambda g, t: (0, g)),
            pl.BlockSpec((bsz, pcs), lambda g, t: (0, g)),
            pl.BlockSpec((bsz, pcs), lambda g, t: (0, g)),
        ],
        out_specs=[
            pl.BlockSpec((S5_PAIR, tb_len * bsz, SSM_CH), lambda g, t: (g, t, 0)),
            pl.BlockSpec((bsz, pcs), lambda g, t: (0, g)),
            pl.BlockSpec((bsz, pcs), lambda g, t: (0, g)),
        ],
        out_shape=[
            jax.ShapeDtypeStruct((nc, seqlen * bsz, SSM_CH), F32),
            jax.ShapeDtypeStruct((bsz, SSM_GROUPS * SSM_STATE), F32),
            jax.ShapeDtypeStruct((bsz, SSM_GROUPS * SSM_STATE), F32),
        ],
        scratch_shapes=[
            pltpu.VMEM((S5_PAIR, tb_len * bsz, SSM_CH), F32),
            pltpu.VMEM((S5_PAIR, tb_len * bsz, 2 * SSM_CS), F32),
            pltpu.VMEM((S5_PAIR, tb_len * bsz, 2 * SSM_CS), BF16),
            pltpu.VMEM((2, bsz, pcs), F32),
        ],
        compiler_params=_cp(("parallel", "arbitrary"), 48),
        name="s5",
    )(*([u] * bsz), bm, cm, d, lre, lim, h0re, h0im)


def _mergeout_body(x_ref, ya_ref, wglu_ref, bglu_ref, ymla_ref, ymem_ref, g_ref,
                   wbs_ref, wbm_ref, wbe_ref, wo_ref, o_ref, ya_scr, *, bsz, tt):
    rows = bsz * tt
    for c in range(SSM_WIDTH // SSM_CH):
        for b in range(bsz):
            ya_scr[b * tt:(b + 1) * tt, c * SSM_CH:(c + 1) * SSM_CH] = ya_ref[c, pl.ds(b, tt, stride=bsz), :]
    y = ya_scr[...]
    gate = _sigmoid(_dot(y.astype(BF16), wglu_ref[...]) + bglu_ref[...])
    yg = (y * gate).astype(BF16)
    g = g_ref[...].reshape(rows, 3 * D_MODEL)
    ymla = ymla_ref[...].reshape(rows, MLA_HEADS * V_DIM)
    ymem = ymem_ref[...].reshape(rows, MEM_WIDTH)
    m = (g[:, 0:D_MODEL].astype(F32) * _dot(yg, wbs_ref[...])
         + g[:, D_MODEL:2 * D_MODEL].astype(F32) * _dot(ymla, wbm_ref[...])
         + g[:, 2 * D_MODEL:3 * D_MODEL].astype(F32) * _dot(ymem, wbe_ref[...]))
    out = x_ref[...].reshape(rows, D_MODEL) + _dot(m.astype(BF16), wo_ref[...])
    o_ref[...] = out.reshape(bsz, tt, D_MODEL)


def _mergeout(x, ya_slabs, wglu, bglu, ymla, ymem, gates, wbs, wbm, wbe, wo, *, bsz, seqlen, tt):
    rows = bsz * seqlen
    nc = SSM_WIDTH // SSM_CH
    tile = lambda width: pl.BlockSpec((bsz, tt, width), lambda i: (0, i, 0))
    resident = lambda shape: pl.BlockSpec(shape, lambda i: (0, 0), pipeline_mode=pl.Buffered(1))
    v3 = lambda t: t.reshape(bsz, seqlen, t.shape[-1])
    out = pl.pallas_call(
        functools.partial(_mergeout_body, bsz=bsz, tt=tt),
        grid=(seqlen // tt,),
        in_specs=[
            tile(D_MODEL),
            pl.BlockSpec((nc, tt * bsz, SSM_CH), lambda i: (0, i, 0)),
            resident((SSM_WIDTH, SSM_WIDTH)),
            resident((1, SSM_WIDTH)),
            tile(MLA_HEADS * V_DIM),
            tile(MEM_WIDTH),
            tile(3 * D_MODEL),
            resident((SSM_WIDTH, D_MODEL)),
            resident((MLA_HEADS * V_DIM, D_MODEL)),
            resident((MEM_WIDTH, D_MODEL)),
            resident((D_MODEL, D_MODEL)),
        ],
        out_specs=tile(D_MODEL),
        out_shape=jax.ShapeDtypeStruct((bsz, seqlen, D_MODEL), F32),
        scratch_shapes=[pltpu.VMEM((bsz * tt, SSM_WIDTH), F32)],
        compiler_params=_cp(("parallel",), 56),
        name="mergeout",
    )(v3(x), ya_slabs, wglu, bglu, v3(ymla), v3(ymem), v3(gates), wbs, wbm, wbe, wo)
    return out.reshape(rows, D_MODEL)


def _rope_tables(pos):
    half = QK_ROPE // 2
    inv_freq = ROPE_THETA ** (-jnp.arange(half, dtype=F32) / half)
    ang = pos.astype(F32)[:, None] * inv_freq[None, :]
    cos, sin = jnp.cos(ang), jnp.sin(ang)
    zero = jnp.zeros((pos.shape[0], LANES - QK_ROPE), F32)
    k_tabs = (jnp.concatenate([cos, cos, zero], axis=1), jnp.concatenate([-sin, sin, zero], axis=1))
    q_tabs = (jnp.concatenate([cos, cos, cos, cos], axis=1), jnp.concatenate([-sin, sin, -sin, sin], axis=1))
    return k_tabs, q_tabs


def _s5_params(a_re, a_im, log_dt, b_re, b_im, c_re, c_im, d):
    dt = jnp.exp(log_dt)[:, None]
    mag = jnp.exp(a_re * dt)
    phase = a_im * dt
    lb_re, lb_im = mag * jnp.cos(phase), mag * jnp.sin(phase)
    den = a_re * a_re + a_im * a_im
    nr, ni = lb_re - 1.0, lb_im
    z_re = (nr * a_re + ni * a_im) / den
    z_im = (ni * a_re - nr * a_im) / den
    bb_re = z_re[..., None] * b_re - z_im[..., None] * b_im
    bb_im = z_re[..., None] * b_im + z_im[..., None] * b_re
    nc = SSM_GROUPS // SSM_CG
    eye = jnp.eye(SSM_CG, dtype=F32)

    def blk_b(t):
        t = t.reshape(nc, SSM_CG, SSM_STATE, SSM_GROUP)
        return jnp.einsum("cgph,gk->cghkp", t, eye).reshape(nc, SSM_CH, SSM_CS)

    def blk_c(t):
        t = t.reshape(nc, SSM_CG, SSM_GROUP, SSM_STATE)
        return jnp.einsum("cghp,gk->cgpkh", t, eye).reshape(nc, SSM_CS, SSM_CH)

    bm = jnp.concatenate([blk_b(bb_re), blk_b(bb_im)], axis=2).astype(BF16)
    cm = jnp.concatenate([blk_c(c_re), -blk_c(c_im)], axis=1).astype(BF16)
    return (bm, cm, d.reshape(1, SSM_WIDTH),
            lb_re.reshape(1, SSM_GROUPS * SSM_STATE), lb_im.reshape(1, SSM_GROUPS * SSM_STATE))


def _layer(x, bsz, seqlen, pos0, mem_k, mem_v, lat_past, kr_past, h0re, h0im, p, final_g):
    rows = bsz * seqlen
    tm = min(512, seqlen)
    rep = max(1, min(512, rows) // seqlen)
    (cos, sin), (qcos, qsin) = jax.tree.map(lambda t: jnp.tile(t, (rep, 1)),
                                            _rope_tables(pos0 + jnp.arange(seqlen)))

    ffn_tm, ffn_tf = (1024, 256) if rows % 1024 == 0 else (512, 512)
    x1 = _ffn(x, p["ffn1_norm"], p["ffn1_wg"], p["ffn1_wu"], p["ffn1_wd"], tm=ffn_tm, tf=ffn_tf)
    u_tb, cq, lat, kr, krp, qm = _inproj(x1, p["mix_norm"], p["w_small"], p["q_norm"], p["kv_norm"],
                                         cos, sin, bsz=bsz // rep, seqlen=seqlen * rep, tm=tm * rep)
    gates = _nmm(x1, p["mix_norm"], p["w_gates"], act="sigmoid", out_dtype=BF16,
                 tm=min(1024, rows), tn=1024, name="gates")

    ya_tb, sre, sim = _s5(u_tb, p["s5_bm"], p["s5_cm"], p["s5_d"], p["s5_lre"], p["s5_lim"],
                          h0re, h0im, bsz=bsz, seqlen=seqlen, tb_len=min(256, seqlen))

    q = _qproj(cq, p["w_uq"], qcos, qsin, seqlen=seqlen * rep, tm=min(256, seqlen * rep))
    if lat_past is None:
        kt, v = _kvproj_t(lat, krp, p["w_ukt"], p["w_uv"], bsz=bsz, seqlen=seqlen, tm=tm)
        ymla = _attn(q, kt, v, bsz=bsz, seqlen=seqlen, tq=256, hg=2)
    else:
        past = lat_past.shape[1]
        n_k = past + seqlen
        lat_all = jnp.concatenate([lat_past, lat.reshape(bsz, seqlen, KV_LORA)], axis=1)
        kr_all = jnp.concatenate([kr_past, kr.reshape(bsz, seqlen, QK_ROPE)], axis=1)
        k_lat = jnp.concatenate([lat_all, kr_all, kr_all], axis=-1).astype(BF16).reshape(bsz * n_k, LAT_PAD)
        o_lat = _attn_cached(_qabsorb(q, p["w_ukt"]), k_lat, bsz=bsz, n_q=seqlen, n_k=n_k, past=past)
        ymla = _vabsorb(o_lat, p["w_uv"])

    ymem = _memattn(qm, mem_k, mem_v, bsz=bsz, seqlen=seqlen, tq=tm)

    x2 = _mergeout(x1, ya_tb, p["w_glu"], p["b_glu"], ymla, ymem, gates,
                   p["w_br_ssm"], p["w_br_mla"], p["w_br_mem"], p["w_out"],
                   bsz=bsz, seqlen=seqlen, tt=32)
    y = _ffn(x2, p["ffn2_norm"], p["ffn2_wg"], p["ffn2_wu"], p["ffn2_wd"], final_g, tm=ffn_tm, tf=ffn_tf)
    return y, lat, kr, sre, sim


def kernel(x_prompt, x_sample, cache_kv_latent, cache_k_rope, cache_mem_k, cache_mem_v, state_ssm_re, state_ssm_im, mem_prompt, ffn1_norm, ffn1_w_gate, ffn1_w_up, ffn1_w_down, mix_norm, w_in, q_norm, w_uq, kv_norm, w_uk, w_uv, ssm_a_re, ssm_a_im, ssm_log_dt, ssm_b_re, ssm_b_im, ssm_c_re, ssm_c_im, ssm_d, ssm_w_glu, ssm_b_glu, mem_norm, w_mem_k, w_mem_v, w_br_ssm, w_br_mla, w_br_mem, w_out, ffn2_norm, ffn2_w_gate, ffn2_w_up, ffn2_w_down, final_norm):
    bp, lp, _ = x_prompt.shape
    bs, ls, _ = x_sample.shape
    past = cache_kv_latent.shape[2]
    l = 0
    bf = lambda t: t.astype(BF16)

    wi = bf(w_in[l])
    c0 = SSM_WIDTH + Q_LORA + KV_LORA
    w_kr = wi[:, c0:c0 + QK_ROPE]
    half = QK_ROPE // 2
    zpad = jnp.zeros((D_MODEL, LANES - QK_ROPE), BF16)
    w_small = jnp.concatenate([
        wi[:, :c0], w_kr, zpad, w_kr[:, half:], w_kr[:, :half], zpad,
        wi[:, c0 + QK_ROPE:c0 + QK_ROPE + MEM_WIDTH]], axis=1)
    w_gates = wi[:, c0 + QK_ROPE + MEM_WIDTH:]

    wq = w_uq[l].reshape(Q_LORA, MLA_HEADS, QK_NOPE + QK_ROPE)
    wq_r = wq[:, :, QK_NOPE:]
    hw = MLA_HEADS * LANES
    w_uq3 = jnp.concatenate([
        wq[:, :, :QK_NOPE].reshape(Q_LORA, hw),
        wq_r.reshape(Q_LORA, hw // 2),
        jnp.concatenate([wq_r[..., half:], wq_r[..., :half]], axis=-1).reshape(Q_LORA, hw // 2)], axis=1)

    bm, cm, d, lre, lim = _s5_params(ssm_a_re[l], ssm_a_im[l], ssm_log_dt[l], ssm_b_re[l], ssm_b_im[l],
                                     ssm_c_re[l], ssm_c_im[l], ssm_d[l])
    p = {
        "ffn1_norm": ffn1_norm[l][None], "ffn1_wg": ffn1_w_gate[l], "ffn1_wu": ffn1_w_up[l],
        "ffn1_wd": ffn1_w_down[l],
        "mix_norm": mix_norm[l][None], "w_small": bf(w_small), "w_gates": bf(w_gates),
        "q_norm": q_norm[l][None], "kv_norm": kv_norm[l][None],
        "w_uq": bf(w_uq3), "w_ukt": bf(w_uk[l].T), "w_uv": bf(w_uv[l]),
        "s5_bm": bm, "s5_cm": cm, "s5_d": d, "s5_lre": lre, "s5_lim": lim,
        "w_glu": bf(ssm_w_glu[l]), "b_glu": ssm_b_glu[l][None],
        "w_br_ssm": bf(w_br_ssm[l]), "w_br_mla": bf(w_br_mla[l]), "w_br_mem": bf(w_br_mem[l]),
        "w_out": bf(w_out[l]),
        "ffn2_norm": ffn2_norm[l][None], "ffn2_wg": ffn2_w_gate[l], "ffn2_wu": ffn2_w_up[l],
        "ffn2_wd": ffn2_w_down[l],
    }
    fg = final_norm[None]

    w_mem = bf(jnp.concatenate([w_mem_k[l], w_mem_v[l]], axis=1))
    mkv = _nmm(mem_prompt.reshape(bp * MEM_TOKENS, D_MODEL), mem_norm[l][None], w_mem,
               act=None, out_dtype=F32, tm=512, tn=2 * MEM_WIDTH, name="memkv")
    mk_p = mkv[:, :MEM_WIDTH].reshape(bp, MEM_TOKENS, MEM_WIDTH)
    mv_p = mkv[:, MEM_WIDTH:].reshape(bp, MEM_TOKENS, MEM_WIDTH)

    n_state = SSM_GROUPS * SSM_STATE
    zero_state = jnp.zeros((bp, n_state), F32)
    yp, lat_p, kr_p, sre_p, sim_p = _layer(
        x_prompt.reshape(bp * lp, D_MODEL), bp, lp, 0, mk_p, mv_p, None, None,
        zero_state, zero_state, p, fg)

    ys, lat_s, kr_s, sre_s, sim_s = _layer(
        x_sample.reshape(bs * ls, D_MODEL), bs, ls, past,
        cache_mem_k[l].reshape(bs, MEM_TOKENS, MEM_WIDTH), cache_mem_v[l].reshape(bs, MEM_TOKENS, MEM_WIDTH),
        cache_kv_latent[l], cache_k_rope[l],
        state_ssm_re[l].reshape(bs, n_state), state_ssm_im[l].reshape(bs, n_state), p, fg)

    st = lambda t, b: t.reshape(1, b, SSM_GROUPS, SSM_STATE)
    return (yp.reshape(bp, lp, D_MODEL), ys.reshape(bs, ls, D_MODEL),
            lat_p.reshape(1, bp, lp, KV_LORA), kr_p.reshape(1, bp, lp, QK_ROPE),
            mk_p.reshape(1, bp, MEM_TOKENS, MEM_HEADS, MEM_HEAD_DIM),
            mv_p.reshape(1, bp, MEM_TOKENS, MEM_HEADS, MEM_HEAD_DIM),
            st(sre_p, bp), st(sim_p, bp),
            lat_s.reshape(1, bs, ls, KV_LORA), kr_s.reshape(1, bs, ls, QK_ROPE),
            st(sre_s, bs), st(sim_s, bs))
```

```python
import functools
import math

import jax
import jax.numpy as jnp
from jax import lax
from jax.experimental import pallas as pl
from jax.experimental.pallas import tpu as pltpu

F32 = jnp.float32
BF16 = jnp.bfloat16

D_MODEL = 2048
D_FF = 5632
CHUNK = 64
SSM_WIDTH = D_MODEL // 2
SSM_GROUP = 16
SSM_GROUPS = SSM_WIDTH // SSM_GROUP
SSM_STATE = 64
MLA_HEADS = 16
QK_NOPE = 128
QK_ROPE = 64
V_DIM = 128
Q_LORA = 768
KV_LORA = 512
ROPE_THETA = 10000.0
MEM_TOKENS = 256
MEM_HEADS = 4
MEM_HEAD_DIM = 128
MEM_WIDTH = MEM_HEADS * MEM_HEAD_DIM
RMS_EPS = 1e-6
NEG_INF = -1e30
MLA_SCALE = (QK_NOPE + QK_ROPE) ** -0.5
MEM_SCALE = MEM_HEAD_DIM ** -0.5

LANES = 128
HEAD_PAD = 2 * LANES
SSM_CH = 128
SSM_CG = SSM_CH // SSM_GROUP
SSM_CS = SSM_CG * SSM_STATE
MIB = 1024 * 1024


def _cp(sem, vmem_mib):
    return pltpu.CompilerParams(dimension_semantics=sem, vmem_limit_bytes=int(vmem_mib * MIB))


def _dot(a, b):
    return jnp.dot(a, b, preferred_element_type=F32)


def _rms(x, g):
    return x * lax.rsqrt(jnp.mean(x * x, axis=-1, keepdims=True) + RMS_EPS) * g


def _sigmoid(x):
    return 1.0 / (1.0 + jnp.exp(-x))


def _gelu_tanh(x):
    cdf = 0.5 * (1.0 + jnp.tanh(math.sqrt(2.0 / math.pi) * (x + 0.044715 * (x * x * x))))
    return x * cdf


def _ffn_body(x_ref, g_ref, wg_ref, wu_ref, wd_ref, *rest, final, tf, tf_last):
    if final:
        fg_ref, o_ref, h_scr = rest
    else:
        o_ref, h_scr = rest
    acc_scr = o_ref
    j = pl.program_id(1)
    last = pl.num_programs(1) - 1

    def contribution(h, width):
        a = _dot(h, wg_ref[:, 0:width].astype(BF16))
        b = _dot(h, wu_ref[:, 0:width].astype(BF16))
        act = ((a * _sigmoid(a)) * b).astype(BF16)
        return _dot(act, wd_ref[0:width, :].astype(BF16))

    @pl.when(j == 0)
    def _():
        h = _rms(x_ref[...], g_ref[...]).astype(BF16)
        h_scr[...] = h
        acc_scr[...] = contribution(h, tf)

    @pl.when(jnp.logical_and(j > 0, j < last))
    def _():
        acc_scr[...] += contribution(h_scr[...], tf)

    @pl.when(j == last)
    def _():
        y = x_ref[...] + 0.5 * (acc_scr[...] + contribution(h_scr[...], tf_last))
        if final:
            y = _rms(y, fg_ref[...])
        o_ref[...] = y


def _ffn(x, g, wg, wu, wd, final_g=None, *, tm=512, tf=512):
    nf = pl.cdiv(D_FF, tf)
    tf_last = D_FF - (nf - 1) * tf
    rows = x.shape[0]
    final = final_g is not None
    in_specs = [
        pl.BlockSpec((tm, D_MODEL), lambda i, j: (i, 0)),
        pl.BlockSpec((1, D_MODEL), lambda i, j: (0, 0)),
        pl.BlockSpec((D_MODEL, tf), lambda i, j: (0, j)),
        pl.BlockSpec((D_MODEL, tf), lambda i, j: (0, j)),
        pl.BlockSpec((tf, D_MODEL), lambda i, j: (j, 0)),
    ]
    args = [x, g, wg, wu, wd]
    if final:
        in_specs.append(pl.BlockSpec((1, D_MODEL), lambda i, j: (0, 0)))
        args.append(final_g)
    return pl.pallas_call(
        functools.partial(_ffn_body, final=final, tf=tf, tf_last=tf_last),
        grid=(rows // tm, nf),
        in_specs=in_specs,
        out_specs=pl.BlockSpec((tm, D_MODEL), lambda i, j: (i, 0)),
        out_shape=jax.ShapeDtypeStruct((rows, D_MODEL), F32),
        scratch_shapes=[pltpu.VMEM((tm, D_MODEL), BF16)],
        compiler_params=_cp(("parallel", "arbitrary"), 58),
        name="ffn",
    )(*args)


def _nmm_body(x_ref, g_ref, w_ref, o_ref, h_scr, *, act):
    def tile(h):
        z = _dot(h, w_ref[...])
        if act == "sigmoid":
            z = _sigmoid(z)
        o_ref[...] = z.astype(o_ref.dtype)

    @pl.when(pl.program_id(1) == 0)
    def _():
        h = _rms(x_ref[...], g_ref[...]).astype(BF16)
        h_scr[...] = h
        tile(h)

    @pl.when(pl.program_id(1) > 0)
    def _():
        tile(h_scr[...])


def _nmm(x, g, w, *, act, out_dtype, tm, tn, name):
    rows, k = x.shape
    n = w.shape[1]
    return pl.pallas_call(
        functools.partial(_nmm_body, act=act),
        grid=(rows // tm, n // tn),
        in_specs=[
            pl.BlockSpec((tm, k), lambda i, j: (i, 0)),
            pl.BlockSpec((1, k), lambda i, j: (0, 0)),
            pl.BlockSpec((k, tn), lambda i, j: (0, j)),
        ],
        out_specs=pl.BlockSpec((tm, tn), lambda i, j: (i, j)),
        out_shape=jax.ShapeDtypeStruct((rows, n), out_dtype),
        scratch_shapes=[pltpu.VMEM((tm, k), BF16)],
        compiler_params=_cp(("parallel", "arbitrary"), 40),
        name=name,
    )(x, g, w)


def _gates_body(h_ref, w_ref, o_ref):
    o_ref[...] = _sigmoid(_dot(h_ref[...], w_ref[...])).astype(BF16)


def _gates(h, w, *, tm, tn):
    rows, k = h.shape
    n = w.shape[1]
    return pl.pallas_call(
        _gates_body,
        grid=(rows // tm, n // tn),
        in_specs=[
            pl.BlockSpec((tm, k), lambda i, j: (i, 0)),
            pl.BlockSpec((k, tn), lambda i, j: (0, j)),
        ],
        out_specs=pl.BlockSpec((tm, tn), lambda i, j: (i, j)),
        out_shape=jax.ShapeDtypeStruct((rows, n), BF16),
        compiler_params=_cp(("parallel", "arbitrary"), 56),
        name="gates",
    )(h, w)


_C_U = SSM_WIDTH
_C_Q = _C_U + Q_LORA
_C_KV = _C_Q + KV_LORA
_C_KX = _C_KV + LANES
_C_KS = _C_KX + LANES
_C_QM = _C_KS + MEM_WIDTH


def _inproj_body(x_ref, g_ref, w_ref, qg_ref, kvg_ref, cos_ref, sin_ref,
                 u_ref, cq_ref, lat_ref, kr_ref, krp_ref, qm_ref, h_ref):
    h = _rms(x_ref[...], g_ref[...]).astype(BF16)
    h_ref[...] = h
    u_ref[...] = _dot(h, w_ref[:, 0:_C_U])
    cq_ref[...] = _rms(_dot(h, w_ref[:, _C_U:_C_Q]), qg_ref[...]).astype(BF16)
    lat_ref[...] = _rms(_dot(h, w_ref[:, _C_Q:_C_KV]), kvg_ref[...])
    kx = _dot(h, w_ref[:, _C_KV:_C_KX])
    ks = _dot(h, w_ref[:, _C_KX:_C_KS])
    r = kx * cos_ref[...] + ks * sin_ref[...]
    krp_ref[...] = r
    kr_ref[...] = r[:, :QK_ROPE]
    qm_ref[...] = _dot(h, w_ref[:, _C_KS:_C_QM]).astype(BF16)


def _inproj(x, g, w, qg, kvg, cos, sin, *, bsz, seqlen, tm):
    rows = bsz * seqlen
    nt = seqlen // tm
    full = lambda i: (0, 0)
    row = lambda i: (i, 0)
    return pl.pallas_call(
        _inproj_body,
        grid=(rows // tm,),
        in_specs=[
            pl.BlockSpec((tm, D_MODEL), row),
            pl.BlockSpec((1, D_MODEL), full),
            pl.BlockSpec((D_MODEL, _C_QM), full),
            pl.BlockSpec((1, Q_LORA), full),
            pl.BlockSpec((1, KV_LORA), full),
            pl.BlockSpec((tm, LANES), lambda i: (i % nt, 0)),
            pl.BlockSpec((tm, LANES), lambda i: (i % nt, 0)),
        ],
        out_specs=[
            pl.BlockSpec((tm, SSM_WIDTH), row),
            pl.BlockSpec((tm, Q_LORA), row),
            pl.BlockSpec((tm, KV_LORA), row),
            pl.BlockSpec((tm, QK_ROPE), row),
            pl.BlockSpec((tm, LANES), row),
            pl.BlockSpec((tm, MEM_WIDTH), row),
            pl.BlockSpec((tm, D_MODEL), row),
        ],
        out_shape=[
            jax.ShapeDtypeStruct((rows, SSM_WIDTH), F32),
            jax.ShapeDtypeStruct((rows, Q_LORA), BF16),
            jax.ShapeDtypeStruct((rows, KV_LORA), F32),
            jax.ShapeDtypeStruct((rows, QK_ROPE), F32),
            jax.ShapeDtypeStruct((rows, LANES), F32),
            jax.ShapeDtypeStruct((rows, MEM_WIDTH), BF16),
            jax.ShapeDtypeStruct((rows, D_MODEL), BF16),
        ],
        compiler_params=_cp(("parallel",), 56),
        name="inproj",
    )(x, g, w, qg, kvg, cos, sin)


def _qproj_body(cq_ref, w_ref, cos_ref, sin_ref, q_ref):
    cq = cq_ref[...]
    hw = MLA_HEADS * LANES
    nope = _dot(cq, w_ref[:, 0:hw])
    for h in range(MLA_HEADS):
        q_ref[:, h * HEAD_PAD:h * HEAD_PAD + LANES] = nope[:, h * LANES:(h + 1) * LANES].astype(BF16)
    pw = hw // 2
    rx = _dot(cq, w_ref[:, hw:hw + pw])
    rs = _dot(cq, w_ref[:, hw + pw:hw + 2 * pw])
    c = cos_ref[...]
    s = sin_ref[...]
    for j in range(MLA_HEADS // 2):
        sl = slice(j * LANES, (j + 1) * LANES)
        r = (rx[:, sl] * c + rs[:, sl] * s).astype(BF16)
        for h in (2 * j, 2 * j + 1):
            q_ref[:, h * HEAD_PAD + LANES:(h + 1) * HEAD_PAD] = r


def _qproj(cq, w, cos, sin, *, seqlen, tm):
    rows = cq.shape[0]
    nt = seqlen // tm
    return pl.pallas_call(
        _qproj_body,
        grid=(rows // tm,),
        in_specs=[
            pl.BlockSpec((tm, Q_LORA), lambda i: (i, 0)),
            pl.BlockSpec((Q_LORA, 2 * MLA_HEADS * LANES), lambda i: (0, 0)),
            pl.BlockSpec((tm, LANES), lambda i: (i % nt, 0)),
            pl.BlockSpec((tm, LANES), lambda i: (i % nt, 0)),
        ],
        out_specs=pl.BlockSpec((tm, MLA_HEADS * HEAD_PAD), lambda i: (i, 0)),
        out_shape=jax.ShapeDtypeStruct((rows, MLA_HEADS * HEAD_PAD), BF16),
        compiler_params=_cp(("parallel",), 48),
        name="qproj",
    )(cq, w, cos, sin)


def _kvproj_t_body(lat_ref, krp_ref, wukt_ref, wuv_ref, kt_ref, v_ref):
    lat = lat_ref[...]
    v_ref[...] = _dot(lat.astype(BF16), wuv_ref[...]).astype(BF16)
    kt = _dot(wukt_ref[...], lat.T.astype(BF16))
    krt = krp_ref[...].T.astype(BF16)
    krt_odd = jnp.concatenate([krt[QK_ROPE:], krt[:QK_ROPE]], axis=0)
    for h in range(MLA_HEADS):
        kt_ref[h * HEAD_PAD:h * HEAD_PAD + LANES, :] = kt[h * LANES:(h + 1) * LANES, :].astype(BF16)
        kt_ref[h * HEAD_PAD + LANES:(h + 1) * HEAD_PAD, :] = krt_odd if h % 2 else krt


def _kvproj_t(lat, krp, wukt, wuv, *, bsz, seqlen, tm):
    rows = bsz * seqlen
    nt = seqlen // tm
    return pl.pallas_call(
        _kvproj_t_body,
        grid=(rows // tm,),
        in_specs=[
            pl.BlockSpec((tm, KV_LORA), lambda i: (i, 0)),
            pl.BlockSpec((tm, LANES), lambda i: (i, 0)),
            pl.BlockSpec((MLA_HEADS * QK_NOPE, KV_LORA), lambda i: (0, 0)),
            pl.BlockSpec((KV_LORA, MLA_HEADS * V_DIM), lambda i: (0, 0)),
        ],
        out_specs=[
            pl.BlockSpec((None, MLA_HEADS * HEAD_PAD, tm), lambda i: (i // nt, 0, i % nt)),
            pl.BlockSpec((tm, MLA_HEADS * V_DIM), lambda i: (i, 0)),
        ],
        out_shape=[
            jax.ShapeDtypeStruct((bsz, MLA_HEADS * HEAD_PAD, seqlen), BF16),
            jax.ShapeDtypeStruct((rows, MLA_HEADS * V_DIM), BF16),
        ],
        compiler_params=_cp(("parallel",), 48),
        name="kvproj_t",
    )(lat, krp, wukt, wuv)


LAT_PAD = KV_LORA + LANES


def _qabsorb_body(q_ref, wukt_ref, o_ref):
    ql = _dot(q_ref[:, 0:LANES], wukt_ref[...])
    o_ref[:, 0:KV_LORA] = ql.astype(BF16)
    rope = q_ref[:, LANES:HEAD_PAD]
    lane = lax.broadcasted_iota(jnp.int32, rope.shape, 1)
    own = (lane < QK_ROPE) == (pl.program_id(0) % 2 == 0)
    o_ref[:, KV_LORA:LAT_PAD] = jnp.where(own, rope, jnp.zeros_like(rope))


def _qabsorb(q, wukt):
    rows = q.shape[0]
    return pl.pallas_call(
        _qabsorb_body,
        grid=(MLA_HEADS,),
        in_specs=[
            pl.BlockSpec((rows, HEAD_PAD), lambda h: (0, h)),
            pl.BlockSpec((QK_NOPE, KV_LORA), lambda h: (h, 0)),
        ],
        out_specs=pl.BlockSpec((None, rows, LAT_PAD), lambda h: (h, 0, 0)),
        out_shape=jax.ShapeDtypeStruct((MLA_HEADS, rows, LAT_PAD), BF16),
        compiler_params=_cp(("parallel",), 24),
        name="qabsorb",
    )(q, wukt)


def _vabsorb_body(o_ref, wuv_ref, y_ref):
    y_ref[...] = _dot(o_ref[...], wuv_ref[...]).astype(BF16)


def _vabsorb(o_lat, wuv):
    rows = o_lat.shape[1]
    return pl.pallas_call(
        _vabsorb_body,
        grid=(MLA_HEADS,),
        in_specs=[
            pl.BlockSpec((None, rows, KV_LORA), lambda h: (h, 0, 0)),
            pl.BlockSpec((KV_LORA, V_DIM), lambda h: (0, h)),
        ],
        out_specs=pl.BlockSpec((rows, V_DIM), lambda h: (0, h)),
        out_shape=jax.ShapeDtypeStruct((rows, MLA_HEADS * V_DIM), BF16),
        compiler_params=_cp(("parallel",), 24),
        name="vabsorb",
    )(o_lat, wuv)


def _chunk_mask(tq, tk, q0, k0):
    r = lax.broadcasted_iota(jnp.int32, (tq, tk), 0) + q0
    c = lax.broadcasted_iota(jnp.int32, (tq, tk), 1) + k0
    return (c // CHUNK) <= (r // CHUNK)


_MLA_EXP2_SCALE = MLA_SCALE * math.log2(math.e)


def _attn_body(q_ref, kt_ref, v_ref, o_ref, *, tq, ts, hg, nq):
    visible = _chunk_mask(tq, tq, 0, 0)
    for vq in range(nq):
        n_past = vq * tq
        rows = slice(n_past, n_past + tq)
        for h in range(hg):
            q = q_ref[rows, h * HEAD_PAD:(h + 1) * HEAD_PAD]
            hk = slice(h * HEAD_PAD, (h + 1) * HEAD_PAD)
            hv = slice(h * V_DIM, (h + 1) * V_DIM)
            m = l = acc = None
            strips = [(n_past, tq)] + [(k0, min(ts, n_past - k0)) for k0 in range(0, n_past, ts)]
            for k0, kw in strips:
                s = _dot(q, kt_ref[hk, k0:k0 + kw]) * _MLA_EXP2_SCALE
                if k0 == n_past:
                    s = jnp.where(visible, s, NEG_INF)
                m_strip = jnp.max(s, axis=-1, keepdims=True)
                m_new = m_strip if m is None else jnp.maximum(m, m_strip)
                p = jnp.exp2(s - m_new)
                l_strip = p[:, 0:LANES]
                for c0 in range(LANES, kw, LANES):
                    l_strip = l_strip + p[:, c0:c0 + LANES]
                pv = _dot(p.astype(BF16), v_ref[k0:k0 + kw, hv])
                if m is None:
                    l, acc = l_strip, pv
                else:
                    alpha = jnp.exp2(m - m_new)
                    l = alpha * l + l_strip
                    acc = alpha * acc + pv
                m = m_new
            o_ref[rows, hv] = (acc / jnp.sum(l, axis=-1, keepdims=True)).astype(BF16)


def _attn(q, kt, v, *, bsz, seqlen, tq, hg):
    nq = seqlen // tq
    return pl.pallas_call(
        functools.partial(_attn_body, tq=tq, ts=tq, hg=hg, nq=nq),
        grid=(bsz, MLA_HEADS // hg),
        in_specs=[
            pl.BlockSpec((seqlen, hg * HEAD_PAD), lambda b, g: (b, g)),
            pl.BlockSpec((None, hg * HEAD_PAD, seqlen), lambda b, g: (b, g, 0)),
            pl.BlockSpec((seqlen, hg * V_DIM), lambda b, g: (b, g)),
        ],
        out_specs=pl.BlockSpec((seqlen, hg * V_DIM), lambda b, g: (b, g)),
        out_shape=jax.ShapeDtypeStruct((bsz * seqlen, MLA_HEADS * V_DIM), BF16),
        compiler_params=_cp(("parallel", "parallel"), 48),
        name="attn",
    )(q, kt, v)


def _attn_cached_body(q_ref, k_ref, o_ref, *, n_q, n_k, past):
    q = q_ref[...].reshape(MLA_HEADS * n_q, LAT_PAD)
    k = k_ref[...]
    s = lax.dot_general(q, k, (((1,), (1,)), ((), ())), preferred_element_type=F32) * _MLA_EXP2_SCALE
    s = s.reshape(MLA_HEADS, n_q, n_k)
    s = jnp.where(_chunk_mask(n_q, n_k, past, 0)[None], s, NEG_INF)
    m = jnp.max(s, axis=-1, keepdims=True)
    p = jnp.exp2(s - m)
    l = jnp.sum(p, axis=-1, keepdims=True)
    acc = _dot(p.reshape(MLA_HEADS * n_q, n_k).astype(BF16), k[:, 0:KV_LORA])
    o_ref[...] = (acc.reshape(MLA_HEADS, n_q, KV_LORA) / l).astype(BF16)


def _attn_cached(q_lat, k_lat, *, bsz, n_q, n_k, past):
    return pl.pallas_call(
        functools.partial(_attn_cached_body, n_q=n_q, n_k=n_k, past=past),
        grid=(bsz,),
        in_specs=[
            pl.BlockSpec((MLA_HEADS, n_q, LAT_PAD), lambda b: (0, b, 0)),
            pl.BlockSpec((n_k, LAT_PAD), lambda b: (b, 0)),
        ],
        out_specs=pl.BlockSpec((MLA_HEADS, n_q, KV_LORA), lambda b: (0, b, 0)),
        out_shape=jax.ShapeDtypeStruct((MLA_HEADS, bsz * n_q, KV_LORA), BF16),
        compiler_params=_cp(("parallel",), 48),
        name="attn_cached",
    )(q_lat, k_lat)


def _memattn_body(q_ref, k_ref, v_ref, o_ref):
    kt = k_ref[...].T.astype(BF16)
    vb = v_ref[...].astype(BF16)
    for h in range(MEM_HEADS):
        sl = slice(h * MEM_HEAD_DIM, (h + 1) * MEM_HEAD_DIM)
        s = _dot(q_ref[:, sl], kt[sl, :]) * MEM_SCALE
        m = jnp.max(s, axis=-1, keepdims=True)
        p = jnp.exp(s - m)
        l = jnp.sum(p, axis=-1, keepdims=True)
        o_ref[:, sl] = (_dot(p.astype(BF16), vb[:, sl]) / l).astype(BF16)


def _memattn(q, k, v, *, bsz, seqlen, tq):
    nt = seqlen // tq
    return pl.pallas_call(
        _memattn_body,
        grid=(bsz, nt),
        in_specs=[
            pl.BlockSpec((tq, MEM_WIDTH), lambda b, i: (b * nt + i, 0)),
            pl.BlockSpec((None, MEM_TOKENS, MEM_WIDTH), lambda b, i: (b, 0, 0)),
            pl.BlockSpec((None, MEM_TOKENS, MEM_WIDTH), lambda b, i: (b, 0, 0)),
        ],
        out_specs=pl.BlockSpec((tq, MEM_WIDTH), lambda b, i: (b * nt + i, 0)),
        out_shape=jax.ShapeDtypeStruct((bsz * seqlen, MEM_WIDTH), BF16),
        compiler_params=_cp(("parallel", "parallel"), 24),
        name="memattn",
    )(q, k, v)


S5_PAIR = 2


def _s5_body(*refs, tb_len, bsz):
    u_refs = refs[:bsz]
    (bm_ref, cm_ref, d_ref, lre_ref, lim_ref, h0re_ref, h0im_ref,
     y_ref, sre_ref, sim_ref, u_scr, bu_scr, xb_scr, st_scr) = refs[bsz:]
    tb = pl.program_id(1)

    @pl.when(tb == 0)
    def _():
        st_scr[0] = h0re_ref[...]
        st_scr[1] = h0im_ref[...]

    for k in range(S5_PAIR):
        for b in range(bsz):
            u_scr[k, pl.ds(b, tb_len, stride=bsz), :] = u_refs[b][:, k * SSM_CH:(k + 1) * SSM_CH]
    for k in range(S5_PAIR):
        bu_scr[k] = _dot(u_scr[k].astype(BF16), bm_ref[k])

    for k in range(S5_PAIR):
        cs = slice(k * SSM_CS, (k + 1) * SSM_CS)
        lre = jnp.broadcast_to(lre_ref[:, cs], (bsz, SSM_CS))
        lim = jnp.broadcast_to(lim_ref[:, cs], (bsz, SSM_CS))
        re = st_scr[0, :, cs]
        im = st_scr[1, :, cs]
        for t in range(0, tb_len, 2):
            pair = []
            for r0 in (t * bsz, (t + 1) * bsz):
                nre = (lre * re - lim * im) + bu_scr[k, r0:r0 + bsz, 0:SSM_CS]
                nim = (lre * im + lim * re) + bu_scr[k, r0:r0 + bsz, SSM_CS:2 * SSM_CS]
                re, im = nre, nim
                pair.append((nre, nim))
            rows = slice(t * bsz, (t + 2) * bsz)
            xb_scr[k, rows, 0:SSM_CS] = jnp.concatenate([pair[0][0], pair[1][0]], axis=0).astype(BF16)
            xb_scr[k, rows, SSM_CS:2 * SSM_CS] = jnp.concatenate([pair[0][1], pair[1][1]], axis=0).astype(BF16)
        st_scr[0, :, cs] = re
        st_scr[1, :, cs] = im
        y = _dot(xb_scr[k], cm_ref[k]) + d_ref[:, k * SSM_CH:(k + 1) * SSM_CH] * u_scr[k]
        y_ref[k] = _gelu_tanh(y)

    @pl.when(tb == pl.num_programs(1) - 1)
    def _():
        sre_ref[...] = st_scr[0]
        sim_ref[...] = st_scr[1]


def _s5(u, bm, cm, d, lre, lim, h0re, h0im, *, bsz, seqlen, tb_len):
    nc = SSM_WIDTH // SSM_CH
    nt = seqlen // tb_len
    pch, pcs = S5_PAIR * SSM_CH, S5_PAIR * SSM_CS
    u_specs = [pl.BlockSpec((tb_len, pch), lambda g, t, b=b: (b * nt + t, g)) for b in range(bsz)]
    return pl.pallas_call(
        functools.partial(_s5_body, tb_len=tb_len, bsz=bsz),
        grid=(nc // S5_PAIR, nt),
        in_specs=u_specs + [
            pl.BlockSpec((S5_PAIR, SSM_CH, 2 * SSM_CS), lambda g, t: (g, 0, 0)),
            pl.BlockSpec((S5_PAIR, 2 * SSM_CS, SSM_CH), lambda g, t: (g, 0, 0)),
            pl.BlockSpec((1, pch), lambda g, t: (0, g)),
            pl.BlockSpec((1, pcs), lambda g, t: (0, g)),
            pl.BlockSpec((1, pcs), lambda g, t: (0, g)),
            pl.BlockSpec((bsz, pcs), lambda g, t: (0, g)),
            pl.BlockSpec((bsz, pcs), lambda g, t: (0, g)),
        ],
        out_specs=[
            pl.BlockSpec((S5_PAIR, tb_len * bsz, SSM_CH), lambda g, t: (g, t, 0)),
            pl.BlockSpec((bsz, pcs), lambda g, t: (0, g)),
            pl.BlockSpec((bsz, pcs), lambda g, t: (0, g)),
        ],
        out_shape=[
            jax.ShapeDtypeStruct((nc, seqlen * bsz, SSM_CH), F32),
            jax.ShapeDtypeStruct((bsz, SSM_GROUPS * SSM_STATE), F32),
            jax.ShapeDtypeStruct((bsz, SSM_GROUPS * SSM_STATE), F32),
        ],
        scratch_shapes=[
            pltpu.VMEM((S5_PAIR, tb_len * bsz, SSM_CH), F32),
            pltpu.VMEM((S5_PAIR, tb_len * bsz, 2 * SSM_CS), F32),
            pltpu.VMEM((S5_PAIR, tb_len * bsz, 2 * SSM_CS), BF16),
            pltpu.VMEM((2, bsz, pcs), F32),
        ],
        compiler_params=_cp(("parallel", "arbitrary"), 48),
        name="s5",
    )(*([u] * bsz), bm, cm, d, lre, lim, h0re, h0im)


def _mergeout_body(x_ref, ya_ref, wglu_ref, bglu_ref, ymla_ref, ymem_ref, g_ref,
                   wbs_ref, wbm_ref, wbe_ref, wo_ref, o_ref, ya_scr, *, bsz, tt):
    rows = bsz * tt
    for c in range(SSM_WIDTH // SSM_CH):
        for b in range(bsz):
            ya_scr[b * tt:(b + 1) * tt, c * SSM_CH:(c + 1) * SSM_CH] = ya_ref[c, pl.ds(b, tt, stride=bsz), :]
    y = ya_scr[...]
    gate = _sigmoid(_dot(y.astype(BF16), wglu_ref[...]) + bglu_ref[...])
    yg = (y * gate).astype(BF16)
    g = g_ref[...].reshape(rows, 3 * D_MODEL)
    ymla = ymla_ref[...].reshape(rows, MLA_HEADS * V_DIM)
    ymem = ymem_ref[...].reshape(rows, MEM_WIDTH)
    m = (g[:, 0:D_MODEL].astype(F32) * _dot(yg, wbs_ref[...])
         + g[:, D_MODEL:2 * D_MODEL].astype(F32) * _dot(ymla, wbm_ref[...])
         + g[:, 2 * D_MODEL:3 * D_MODEL].astype(F32) * _dot(ymem, wbe_ref[...]))
    out = x_ref[...].reshape(rows, D_MODEL) + _dot(m.astype(BF16), wo_ref[...])
    o_ref[...] = out.reshape(bsz, tt, D_MODEL)


def _mergeout(x, ya_slabs, wglu, bglu, ymla, ymem, gates, wbs, wbm, wbe, wo, *, bsz, seqlen, tt):
    rows = bsz * seqlen
    nc = SSM_WIDTH // SSM_CH
    tile = lambda width: pl.BlockSpec((bsz, tt, width), lambda i: (0, i, 0))
    resident = lambda shape: pl.BlockSpec(shape, lambda i: (0, 0), pipeline_mode=pl.Buffered(1))
    v3 = lambda t: t.reshape(bsz, seqlen, t.shape[-1])
    out = pl.pallas_call(
        functools.partial(_mergeout_body, bsz=bsz, tt=tt),
        grid=(seqlen // tt,),
        in_specs=[
            tile(D_MODEL),
            pl.BlockSpec((nc, tt * bsz, SSM_CH), lambda i: (0, i, 0)),
            resident((SSM_WIDTH, SSM_WIDTH)),
            resident((1, SSM_WIDTH)),
            tile(MLA_HEADS * V_DIM),
            tile(MEM_WIDTH),
            tile(3 * D_MODEL),
            resident((SSM_WIDTH, D_MODEL)),
            resident((MLA_HEADS * V_DIM, D_MODEL)),
            resident((MEM_WIDTH, D_MODEL)),
            resident((D_MODEL, D_MODEL)),
        ],
        out_specs=tile(D_MODEL),
        out_shape=jax.ShapeDtypeStruct((bsz, seqlen, D_MODEL), F32),
        scratch_shapes=[pltpu.VMEM((bsz * tt, SSM_WIDTH), F32)],
        compiler_params=_cp(("parallel",), 56),
        name="mergeout",
    )(v3(x), ya_slabs, wglu, bglu, v3(ymla), v3(ymem), v3(gates), wbs, wbm, wbe, wo)
    return out.reshape(rows, D_MODEL)


def _winprep_body(w_ref, ws_ref, wg_ref):
    w = w_ref[...]
    half = QK_ROPE // 2
    kr = w[:, _C_KV:_C_KV + QK_ROPE]
    zero = jnp.zeros((w.shape[0], LANES - QK_ROPE), F32)
    ws_ref[:, 0:_C_KV] = w[:, 0:_C_KV].astype(BF16)
    ws_ref[:, _C_KV:_C_KX] = jnp.concatenate([kr, zero], axis=1).astype(BF16)
    ws_ref[:, _C_KX:_C_KS] = jnp.concatenate([kr[:, half:], kr[:, :half], zero], axis=1).astype(BF16)
    qm0 = _C_KV + QK_ROPE
    ws_ref[:, _C_KS:_C_QM] = w[:, qm0:qm0 + MEM_WIDTH].astype(BF16)
    wg_ref[...] = w[:, qm0 + MEM_WIDTH:].astype(BF16)


def _winprep(w, *, tr=128):
    k, n = w.shape
    n_gates = n - (_C_KV + QK_ROPE + MEM_WIDTH)
    return pl.pallas_call(
        _winprep_body,
        grid=(k // tr,),
        in_specs=[pl.BlockSpec((tr, n), lambda i: (i, 0))],
        out_specs=[pl.BlockSpec((tr, _C_QM), lambda i: (i, 0)),
                   pl.BlockSpec((tr, n_gates), lambda i: (i, 0))],
        out_shape=[jax.ShapeDtypeStruct((k, _C_QM), BF16),
                   jax.ShapeDtypeStruct((k, n_gates), BF16)],
        compiler_params=_cp(("parallel",), 40),
        name="winprep",
    )(w)
def _rope_tables(pos):
    half = QK_ROPE // 2
    inv_freq = ROPE_THETA ** (-jnp.arange(half, dtype=F32) / half)
    ang = pos.astype(F32)[:, None] * inv_freq[None, :]
    cos, sin = jnp.cos(ang), jnp.sin(ang)
    zero = jnp.zeros((pos.shape[0], LANES - QK_ROPE), F32)
    k_tabs = (jnp.concatenate([cos, cos, zero], axis=1), jnp.concatenate([-sin, sin, zero], axis=1))
    q_tabs = (jnp.concatenate([cos, cos, cos, cos], axis=1), jnp.concatenate([-sin, sin, -sin, sin], axis=1))
    return k_tabs, q_tabs


def _s5_params(a_re, a_im, log_dt, b_re, b_im, c_re, c_im, d):
    dt = jnp.exp(log_dt)[:, None]
    mag = jnp.exp(a_re * dt)
    phase = a_im * dt
    lb_re, lb_im = mag * jnp.cos(phase), mag * jnp.sin(phase)
    den = a_re * a_re + a_im * a_im
    nr, ni = lb_re - 1.0, lb_im
    z_re = (nr * a_re + ni * a_im) / den
    z_im = (ni * a_re - nr * a_im) / den
    bb_re = z_re[..., None] * b_re - z_im[..., None] * b_im
    bb_im = z_re[..., None] * b_im + z_im[..., None] * b_re
    nc = SSM_GROUPS // SSM_CG
    eye = jnp.eye(SSM_CG, dtype=F32)

    def blk_b(t):
        t = t.reshape(nc, SSM_CG, SSM_STATE, SSM_GROUP)
        return jnp.einsum("cgph,gk->cghkp", t, eye).reshape(nc, SSM_CH, SSM_CS)

    def blk_c(t):
        t = t.reshape(nc, SSM_CG, SSM_GROUP, SSM_STATE)
        return jnp.einsum("cghp,gk->cgpkh", t, eye).reshape(nc, SSM_CS, SSM_CH)

    bm = jnp.concatenate([blk_b(bb_re), blk_b(bb_im)], axis=2).astype(BF16)
    cm = jnp.concatenate([blk_c(c_re), -blk_c(c_im)], axis=1).astype(BF16)
    return (bm, cm, d.reshape(1, SSM_WIDTH),
            lb_re.reshape(1, SSM_GROUPS * SSM_STATE), lb_im.reshape(1, SSM_GROUPS * SSM_STATE))


def _layer(x, bsz, seqlen, pos0, mem_k, mem_v, lat_past, kr_past, h0re, h0im, p, final_g):
    rows = bsz * seqlen
    tm = min(512, seqlen)
    rep = max(1, min(512, rows) // seqlen)
    (cos, sin), (qcos, qsin) = jax.tree.map(lambda t: jnp.tile(t, (rep, 1)),
                                            _rope_tables(pos0 + jnp.arange(seqlen)))

    ffn_tm, ffn_tf = (1024, 256) if rows % 1024 == 0 else (512, 512)
    x1 = _ffn(x, p["ffn1_norm"], p["ffn1_wg"], p["ffn1_wu"], p["ffn1_wd"], tm=ffn_tm, tf=ffn_tf)
    u_tb, cq, lat, kr, krp, qm, hmix = _inproj(x1, p["mix_norm"], p["w_small"], p["q_norm"], p["kv_norm"],
                                         cos, sin, bsz=bsz // rep, seqlen=seqlen * rep, tm=tm * rep)
    gates = _gates(hmix, p["w_gates"], tm=min(2048, rows), tn=1024)

    ya_tb, sre, sim = _s5(u_tb, p["s5_bm"], p["s5_cm"], p["s5_d"], p["s5_lre"], p["s5_lim"],
                          h0re, h0im, bsz=bsz, seqlen=seqlen, tb_len=min(256, seqlen))

    q = _qproj(cq, p["w_uq"], qcos, qsin, seqlen=seqlen * rep, tm=min(256, seqlen * rep))
    if lat_past is None:
        kt, v = _kvproj_t(lat, krp, p["w_ukt"], p["w_uv"], bsz=bsz, seqlen=seqlen, tm=tm)
        ymla = _attn(q, kt, v, bsz=bsz, seqlen=seqlen, tq=256, hg=2)
    else:
        past = lat_past.shape[1]
        n_k = past + seqlen
        lat_all = jnp.concatenate([lat_past, lat.reshape(bsz, seqlen, KV_LORA)], axis=1)
        kr_all = jnp.concatenate([kr_past, kr.reshape(bsz, seqlen, QK_ROPE)], axis=1)
        k_lat = jnp.concatenate([lat_all, kr_all, kr_all], axis=-1).astype(BF16).reshape(bsz * n_k, LAT_PAD)
        o_lat = _attn_cached(_qabsorb(q, p["w_ukt"]), k_lat, bsz=bsz, n_q=seqlen, n_k=n_k, past=past)
        ymla = _vabsorb(o_lat, p["w_uv"])

    ymem = _memattn(qm, mem_k, mem_v, bsz=bsz, seqlen=seqlen, tq=tm)

    x2 = _mergeout(x1, ya_tb, p["w_glu"], p["b_glu"], ymla, ymem, gates,
                   p["w_br_ssm"], p["w_br_mla"], p["w_br_mem"], p["w_out"],
                   bsz=bsz, seqlen=seqlen, tt=32)
    y = _ffn(x2, p["ffn2_norm"], p["ffn2_wg"], p["ffn2_wu"], p["ffn2_wd"], final_g, tm=ffn_tm, tf=ffn_tf)
    return y, lat, kr, sre, sim


def kernel(x_prompt, x_sample, cache_kv_latent, cache_k_rope, cache_mem_k, cache_mem_v, state_ssm_re, state_ssm_im, mem_prompt, ffn1_norm, ffn1_w_gate, ffn1_w_up, ffn1_w_down, mix_norm, w_in, q_norm, w_uq, kv_norm, w_uk, w_uv, ssm_a_re, ssm_a_im, ssm_log_dt, ssm_b_re, ssm_b_im, ssm_c_re, ssm_c_im, ssm_d, ssm_w_glu, ssm_b_glu, mem_norm, w_mem_k, w_mem_v, w_br_ssm, w_br_mla, w_br_mem, w_out, ffn2_norm, ffn2_w_gate, ffn2_w_up, ffn2_w_down, final_norm):
    bp, lp, _ = x_prompt.shape
    bs, ls, _ = x_sample.shape
    past = cache_kv_latent.shape[2]
    l = 0
    bf = lambda t: t.astype(BF16)

    half = QK_ROPE // 2
    w_small, w_gates = _winprep(w_in[l])

    wq = w_uq[l].reshape(Q_LORA, MLA_HEADS, QK_NOPE + QK_ROPE)
    wq_r = wq[:, :, QK_NOPE:]
    hw = MLA_HEADS * LANES
    w_uq3 = jnp.concatenate([
        wq[:, :, :QK_NOPE].reshape(Q_LORA, hw),
        wq_r.reshape(Q_LORA, hw // 2),
        jnp.concatenate([wq_r[..., half:], wq_r[..., :half]], axis=-1).reshape(Q_LORA, hw // 2)], axis=1)

    bm, cm, d, lre, lim = _s5_params(ssm_a_re[l], ssm_a_im[l], ssm_log_dt[l], ssm_b_re[l], ssm_b_im[l],
                                     ssm_c_re[l], ssm_c_im[l], ssm_d[l])
    p = {
        "ffn1_norm": ffn1_norm[l][None], "ffn1_wg": ffn1_w_gate[l], "ffn1_wu": ffn1_w_up[l],
        "ffn1_wd": ffn1_w_down[l],
        "mix_norm": mix_norm[l][None], "w_small": bf(w_small), "w_gates": bf(w_gates),
        "q_norm": q_norm[l][None], "kv_norm": kv_norm[l][None],
        "w_uq": bf(w_uq3), "w_ukt": bf(w_uk[l].T), "w_uv": bf(w_uv[l]),
        "s5_bm": bm, "s5_cm": cm, "s5_d": d, "s5_lre": lre, "s5_lim": lim,
        "w_glu": bf(ssm_w_glu[l]), "b_glu": ssm_b_glu[l][None],
        "w_br_ssm": bf(w_br_ssm[l]), "w_br_mla": bf(w_br_mla[l]), "w_br_mem": bf(w_br_mem[l]),
        "w_out": bf(w_out[l]),
        "ffn2_norm": ffn2_norm[l][None], "ffn2_wg": ffn2_w_gate[l], "ffn2_wu": ffn2_w_up[l],
        "ffn2_wd": ffn2_w_down[l],
    }
    fg = final_norm[None]

    w_mem = bf(jnp.concatenate([w_mem_k[l], w_mem_v[l]], axis=1))
    mkv = _nmm(mem_prompt.reshape(bp * MEM_TOKENS, D_MODEL), mem_norm[l][None], w_mem,
               act=None, out_dtype=F32, tm=512, tn=2 * MEM_WIDTH, name="memkv")
    mk_p = mkv[:, :MEM_WIDTH].reshape(bp, MEM_TOKENS, MEM_WIDTH)
    mv_p = mkv[:, MEM_WIDTH:].reshape(bp, MEM_TOKENS, MEM_WIDTH)

    n_state = SSM_GROUPS * SSM_STATE
    zero_state = jnp.zeros((bp, n_state), F32)
    yp, lat_p, kr_p, sre_p, sim_p = _layer(
        x_prompt.reshape(bp * lp, D_MODEL), bp, lp, 0, mk_p, mv_p, None, None,
        zero_state, zero_state, p, fg)

    ys, lat_s, kr_s, sre_s, sim_s = _layer(
        x_sample.reshape(bs * ls, D_MODEL), bs, ls, past,
        cache_mem_k[l].reshape(bs, MEM_TOKENS, MEM_WIDTH), cache_mem_v[l].reshape(bs, MEM_TOKENS, MEM_WIDTH),
        cache_kv_latent[l], cache_k_rope[l],
        state_ssm_re[l].reshape(bs, n_state), state_ssm_im[l].reshape(bs, n_state), p, fg)

    st = lambda t, b: t.reshape(1, b, SSM_GROUPS, SSM_STATE)
    return (yp.reshape(bp, lp, D_MODEL), ys.reshape(bs, ls, D_MODEL),
            lat_p.reshape(1, bp, lp, KV_LORA), kr_p.reshape(1, bp, lp, QK_ROPE),
            mk_p.reshape(1, bp, MEM_TOKENS, MEM_HEADS, MEM_HEAD_DIM),
            mv_p.reshape(1, bp, MEM_TOKENS, MEM_HEADS, MEM_HEAD_DIM),
            st(sre_p, bp), st(sim_p, bp),
            lat_s.reshape(1, bs, ls, KV_LORA), kr_s.reshape(1, bs, ls, QK_ROPE),
            st(sre_s, bs), st(sim_s, bs))
```
